```python
import math
import jax, jax.numpy as jnp
from jax import lax
import numpy as np

D_MODEL = 2048
BATCH = 4
SEQ = 8192
DEPTH = 1
DEC_BATCH = 4
DEC_SEQ = 2048
PAST_LEN = 128

HEAD_DIM = 128
N_HEADS_A = 8
N_KV_A = 2
N_HEADS_B = 8
N_KV_B = 2
WINDOW = 128
BLOCK = 128
GRID_W = 64
ROPE_THETA = 10000.0
N_BUCKETS = 32
MAX_DISTANCE = 128
N_GROUPS = 4
EXPERTS_PER_GROUP = 8
N_EXPERTS = N_GROUPS * EXPERTS_PER_GROUP
TOP_K = 2
D_EXPERT = 512
ROW_BLOCK = 128
LN_EPS = 1e-5
RMS_EPS = 1e-6
ALPHA = (2 * DEPTH) ** 0.25
BETA = (8 * DEPTH) ** -0.25
IN_COLS = N_HEADS_A * HEAD_DIM + 2 * N_KV_A * HEAD_DIM + N_HEADS_B * HEAD_DIM + 2 * N_KV_B * HEAD_DIM + 2 * D_MODEL

kernel_name = 'hybrid_gated_encoder'

F32 = jnp.float32


def _split_sizes():
    return [N_HEADS_A * HEAD_DIM, N_KV_A * HEAD_DIM, N_KV_A * HEAD_DIM,
            N_HEADS_B * HEAD_DIM, N_KV_B * HEAD_DIM, N_KV_B * HEAD_DIM, 2 * D_MODEL]


def layer_norm(x, g, b):
    xf = x.astype(F32)
    mu = jnp.mean(xf, -1, keepdims=True)
    xc = xf - mu
    var = jnp.mean(xc * xc, -1, keepdims=True)
    return (xc * lax.rsqrt(var + LN_EPS) * g.astype(F32) + b.astype(F32)).astype(x.dtype)


def rms_norm(x, g):
    xf = x.astype(F32)
    return (xf * lax.rsqrt(jnp.mean(xf * xf, -1, keepdims=True) + RMS_EPS) * g.astype(F32)).astype(x.dtype)


def t5_bucket(rel):
    half = N_BUCKETS // 2
    exact = half // 2
    n = jnp.abs(rel)
    nf = jnp.maximum(n, 1).astype(F32)
    large = exact + (jnp.log(nf / exact) / math.log(MAX_DISTANCE / exact) * (half - exact)).astype(jnp.int32)
    large = jnp.minimum(large, half - 1)
    return jnp.where(rel > 0, half, 0) + jnp.where(n < exact, n, large)


def local_sink_attention(q, k, v, sink, rel_bias):
    bsz, s_len = q.shape[0], q.shape[1]
    nb = s_len // BLOCK
    grp = N_HEADS_A // N_KV_A
    scale = HEAD_DIM ** -0.5
    qb = q.reshape(bsz, nb, BLOCK, N_KV_A, grp, HEAD_DIM)
    pad = ((0, 0), (BLOCK, BLOCK), (0, 0), (0, 0))
    kp = jnp.pad(k, pad).reshape(bsz, nb + 2, BLOCK, N_KV_A, HEAD_DIM)
    vp = jnp.pad(v, pad).reshape(bsz, nb + 2, BLOCK, N_KV_A, HEAD_DIM)
    kb = jnp.concatenate([kp[:, :-2], kp[:, 1:-1], kp[:, 2:]], axis=2)
    vb = jnp.concatenate([vp[:, :-2], vp[:, 1:-1], vp[:, 2:]], axis=2)
    rel = jnp.arange(3 * BLOCK)[None, :] - BLOCK - jnp.arange(BLOCK)[:, None]
    band = jnp.abs(rel) <= WINDOW
    key_pos = jnp.arange(nb)[:, None] * BLOCK + jnp.arange(3 * BLOCK)[None, :] - BLOCK
    inside = (key_pos >= 0) & (key_pos < s_len)
    valid = band[None, :, :] & inside[:, None, :]
    bias = rel_bias.astype(F32)[t5_bucket(rel)]
    bias = bias.transpose(2, 0, 1).reshape(N_KV_A, grp, BLOCK, 3 * BLOCK)
    s = jnp.einsum('bnqhgd,bnkhd->bnhgqk', qb, kb, preferred_element_type=F32) * scale + bias
    s = jnp.where(valid[None, :, None, None], s, jnp.float32(-1e30))
    sk = sink.astype(F32).reshape(1, 1, N_KV_A, grp, 1, 1)
    m = jnp.maximum(jnp.max(s, -1, keepdims=True), sk)
    p = jnp.exp(s - m)
    denom = jnp.sum(p, -1, keepdims=True) + jnp.exp(sk - m)
    o = jnp.einsum('bnhgqk,bnkhd->bnqhgd', (p / denom).astype(v.dtype), vb)
    return o.reshape(bsz, s_len, N_HEADS_A * HEAD_DIM)


def axial_rope_tables(s_len):
    rows = s_len // GRID_W
    row_ids = jnp.repeat(jnp.arange(rows), GRID_W).astype(F32)
    col_ids = jnp.tile(jnp.arange(GRID_W), rows).astype(F32)
    half = HEAD_DIM // 2
    inv = 1.0 / (ROPE_THETA ** (jnp.arange(0, half, 2, dtype=F32) / half))
    ang = jnp.concatenate([row_ids[:, None] * inv, col_ids[:, None] * inv], -1)
    return jnp.cos(ang), jnp.sin(ang)


def apply_rope(x, cos, sin):
    xf = x.astype(F32).reshape(x.shape[:-1] + (HEAD_DIM // 2, 2))
    x1, x2 = xf[..., 0], xf[..., 1]
    c = cos[None, :, None, :]
    s = sin[None, :, None, :]
    out = jnp.stack([x1 * c - x2 * s, x1 * s + x2 * c], -1)
    return out.reshape(x.shape).astype(x.dtype)


def grid_attention(q, k, v):
    bsz, s_len = q.shape[0], q.shape[1]
    nb = s_len // BLOCK
    grp = N_HEADS_B // N_KV_B
    scale = HEAD_DIM ** -0.5
    qb = q.reshape(bsz, nb, BLOCK, N_KV_B, grp, HEAD_DIM).transpose(1, 0, 2, 3, 4, 5)

    def one_block(qblk):
        s = jnp.einsum('bqhgd,bkhd->bhgqk', qblk, k, preferred_element_type=F32) * scale
        p = jax.nn.softmax(s, axis=-1)
        return jnp.einsum('bhgqk,bkhd->bqhgd', p.astype(v.dtype), v)

    o = lax.map(one_block, qb)
    return o.transpose(1, 0, 2, 3, 4, 5).reshape(bsz, s_len, N_HEADS_B * HEAD_DIM)


def hier_moe(x2, w_coarse, b_coarse, w_fine, b_fine, w_gate, w_up, w_down):
    n_tok, d = x2.shape
    cl = jnp.matmul(x2, w_coarse, preferred_element_type=F32) + b_coarse.astype(F32)
    cp = jax.nn.softmax(cl, -1)
    g_idx = jnp.argmax(cl, -1).astype(jnp.int32)
    g_prob = jnp.take_along_axis(cp, g_idx[:, None], -1)
    fl = (jnp.matmul(x2, w_fine, preferred_element_type=F32) + b_fine.astype(F32)).reshape(n_tok, N_GROUPS, EXPERTS_PER_GROUP)
    fl_g = jnp.take_along_axis(fl, g_idx[:, None, None], 1)[:, 0]
    fp = jax.nn.softmax(fl_g, -1)
    top_p, top_i = lax.top_k(fp, TOP_K)
    gates = g_prob * top_p / jnp.sum(top_p, -1, keepdims=True)
    experts = g_idx[:, None] * EXPERTS_PER_GROUP + top_i.astype(jnp.int32)

    n_assign = n_tok * TOP_K
    flat_e = experts.reshape(-1)
    flat_tok = jnp.repeat(jnp.arange(n_tok, dtype=jnp.int32), TOP_K)
    flat_w = gates.reshape(-1)
    order = jnp.argsort(flat_e)
    se, st, sw = flat_e[order], flat_tok[order], flat_w[order]
    counts = jnp.bincount(flat_e, length=N_EXPERTS).astype(jnp.int32)
    starts = jnp.cumsum(counts) - counts
    pcounts = (counts + ROW_BLOCK - 1) // ROW_BLOCK * ROW_BLOCK
    pends = jnp.cumsum(pcounts)
    pstarts = pends - pcounts
    dest = pstarts[se] + jnp.arange(n_assign, dtype=jnp.int32) - starts[se]
    n_blk = (n_assign + ROW_BLOCK - 1) // ROW_BLOCK + N_EXPERTS
    n_rows = n_blk * ROW_BLOCK
    row_tok = jnp.full((n_rows,), n_tok, jnp.int32).at[dest].set(st)
    xpad = jnp.concatenate([x2, jnp.zeros((1, d), x2.dtype)], 0)
    xrows = xpad[row_tok].reshape(n_blk, ROW_BLOCK, d)
    blk_e = jnp.minimum(jnp.searchsorted(pends, jnp.arange(n_blk, dtype=jnp.int32) * ROW_BLOCK, side='right'), N_EXPERTS - 1)

    def expert_block(args):
        xb, e = args
        hidden = jax.nn.silu(xb @ w_gate[e]) * (xb @ w_up[e])
        return hidden @ w_down[e]

    yrows = lax.map(expert_block, (xrows, blk_e)).reshape(n_rows, d)
    y_assign = yrows[dest] * sw[:, None].astype(x2.dtype)
    return jax.ops.segment_sum(y_assign, st, num_segments=n_tok)


def encoder_layer(x, rel_bias, w_in, b_gate, sink_a, q_norm_g, k_norm_g, w_branch_a, w_branch_b,
                  w_out, ln1_g, ln1_b, w_coarse, b_coarse, w_fine, b_fine, w_gate, w_up, w_down,
                  ln2_g, ln2_b):
    bsz, s_len, d = x.shape
    h = x @ w_in
    points = list(np.cumsum(_split_sizes())[:-1])
    qa, ka, va, qb, kb, vb, gts = jnp.split(h, points, axis=-1)
    oa = local_sink_attention(qa.reshape(bsz, s_len, N_HEADS_A, HEAD_DIM),
                              ka.reshape(bsz, s_len, N_KV_A, HEAD_DIM),
                              va.reshape(bsz, s_len, N_KV_A, HEAD_DIM), sink_a, rel_bias)
    ya = oa @ w_branch_a
    cos, sin = axial_rope_tables(s_len)
    qb = apply_rope(rms_norm(qb.reshape(bsz, s_len, N_HEADS_B, HEAD_DIM), q_norm_g), cos, sin)
    kb = apply_rope(rms_norm(kb.reshape(bsz, s_len, N_KV_B, HEAD_DIM), k_norm_g), cos, sin)
    ob = grid_attention(qb, kb, vb.reshape(bsz, s_len, N_KV_B, HEAD_DIM))
    yb = ob @ w_branch_b
    g = jax.nn.sigmoid((gts + b_gate).astype(F32)).astype(x.dtype)
    mix = (g[..., :d] * ya + g[..., d:] * yb) @ w_out
    x = layer_norm(ALPHA * x + mix, ln1_g, ln1_b)
    moe = hier_moe(x.reshape(bsz * s_len, d), w_coarse, b_coarse, w_fine, b_fine,
                   w_gate, w_up, w_down).reshape(bsz, s_len, d)
    return layer_norm(ALPHA * x + moe, ln2_g, ln2_b)


def trunk(x, emb_ln_g, emb_ln_b, rel_bias, layer_params):
    x = layer_norm(x, emb_ln_g, emb_ln_b)
    for l in range(DEPTH):
        lp = [p[l] for p in layer_params]
        x = encoder_layer(x, rel_bias, *lp)
    return x


def setup_inputs(seed: int = 0) -> dict:
    key = jax.random.key(seed)
    ks = jax.random.split(key, 26)
    D, L = D_MODEL, DEPTH

    def nrm(k, shape, scale):
        return jax.random.normal(k, shape, jnp.float32) * scale

    sizes = _split_sizes()
    offs = np.concatenate([[0], np.cumsum(sizes)])
    col_scale = np.ones((IN_COLS,), np.float32)
    col_scale[offs[2]:offs[3]] = BETA
    col_scale[offs[5]:offs[6]] = BETA
    return {
        'x_prompt': nrm(ks[0], (BATCH, SEQ, D), 1.0),
        'x_sample': nrm(ks[1], (DEC_BATCH, DEC_SEQ, D), 1.0),
        'emb_ln_g': 1.0 + nrm(ks[2], (D,), 0.05),
        'emb_ln_b': nrm(ks[3], (D,), 0.05),
        'rel_bias': nrm(ks[4], (N_BUCKETS, N_HEADS_A), 0.5),
        'w_in': nrm(ks[5], (L, D, IN_COLS), D ** -0.5) * jnp.asarray(col_scale),
        'b_gate': nrm(ks[6], (L, 2 * D), 0.1),
        'sink_a': nrm(ks[7], (L, N_HEADS_A), 1.0),
        'q_norm_g': 1.0 + nrm(ks[8], (L, HEAD_DIM), 0.05),
        'k_norm_g': 1.0 + nrm(ks[9], (L, HEAD_DIM), 0.05),
        'w_branch_a': nrm(ks[10], (L, N_HEADS_A * HEAD_DIM, D), (N_HEADS_A * HEAD_DIM) ** -0.5 * BETA),
        'w_branch_b': nrm(ks[11], (L, N_HEADS_B * HEAD_DIM, D), (N_HEADS_B * HEAD_DIM) ** -0.5 * BETA),
        'w_out': nrm(ks[12], (L, D, D), D ** -0.5 * BETA),
        'ln1_g': 1.0 + nrm(ks[13], (L, D), 0.05),
        'ln1_b': nrm(ks[14], (L, D), 0.05),
        'w_coarse': nrm(ks[15], (L, D, N_GROUPS), D ** -0.5),
        'b_coarse': nrm(ks[16], (L, N_GROUPS), 0.01),
        'w_fine': nrm(ks[17], (L, D, N_EXPERTS), D ** -0.5),
        'b_fine': nrm(ks[18], (L, N_EXPERTS), 0.01),
        'w_gate': nrm(ks[19], (L, N_EXPERTS, D, D_EXPERT), D ** -0.5),
        'w_up': nrm(ks[20], (L, N_EXPERTS, D, D_EXPERT), D ** -0.5),
        'w_down': nrm(ks[21], (L, N_EXPERTS, D_EXPERT, D), D_EXPERT ** -0.5 * BETA),
        'ln2_g': 1.0 + nrm(ks[22], (L, D), 0.05),
        'ln2_b': nrm(ks[23], (L, D), 0.05),
    }


def reference(x_prompt, x_sample, emb_ln_g, emb_ln_b, rel_bias, w_in, b_gate, sink_a, q_norm_g,
              k_norm_g, w_branch_a, w_branch_b, w_out, ln1_g, ln1_b, w_coarse, b_coarse, w_fine,
              b_fine, w_gate, w_up, w_down, ln2_g, ln2_b):
    layer_params = (w_in, b_gate, sink_a, q_norm_g, k_norm_g, w_branch_a, w_branch_b, w_out,
                    ln1_g, ln1_b, w_coarse, b_coarse, w_fine, b_fine, w_gate, w_up, w_down,
                    ln2_g, ln2_b)
    y_prompt = trunk(x_prompt, emb_ln_g, emb_ln_b, rel_bias, layer_params)
    y_sample = trunk(x_sample, emb_ln_g, emb_ln_b, rel_bias, layer_params)
    return (y_prompt, y_sample)
```

```python
import functools
import math

import numpy as np
import jax
import jax.numpy as jnp
from jax import lax
from jax.experimental import pallas as pl
from jax.experimental.pallas import tpu as pltpu

F32 = jnp.float32
BF16 = jnp.bfloat16

D_MODEL = 2048
HEAD_DIM = 128
N_HEADS = 8
N_KV = 2
GROUP = N_HEADS // N_KV
Q_COLS = N_HEADS * HEAD_DIM
KV_COLS = N_KV * HEAD_DIM
QKV_COLS = 2 * (Q_COLS + 2 * KV_COLS)
WINDOW = 128
BLOCK = 128
GRID_W = 64
ROPE_THETA = 10000.0
N_BUCKETS = 32
MAX_DISTANCE = 128
N_GROUPS = 4
EXPERTS_PER_GROUP = 8
N_EXPERTS = N_GROUPS * EXPERTS_PER_GROUP
TOP_K = 2
D_EXPERT = 512
LN_EPS = 1e-5
RMS_EPS = 1e-6
DEPTH = 1
ALPHA = (2 * DEPTH) ** 0.25
SCALE = HEAD_DIM ** -0.5
NEG_BIG = -1e30

LANES = 128
VMEM_LIMIT = 56 * 1024 * 1024

QKV_TM = 512
ATT_A_TQ = 512
ATT_B_TQ = 256
ATT_B_TK = 1024
MERGE_TM = 512
MERGE_TC = 256
ROUTE_TN = 2048
MOE_RB = 256
MOVE_ROWS = 512
FINAL_TM = 512
ROUTER_ROWS = 128
FINE_ROW0 = 8


def _layer_norm_rows(x, g, b):
    mu = jnp.mean(x, axis=-1, keepdims=True)
    xc = x - mu
    var = jnp.mean(xc * xc, axis=-1, keepdims=True)
    return xc * lax.rsqrt(var + LN_EPS) * g + b


def _dot(a, b):
    return jnp.dot(a, b, preferred_element_type=F32)


def _dot_nt(a, b):
    return lax.dot_general(a, b, (((1,), (1,)), ((), ())), preferred_element_type=F32)


def _qkv_kernel(x_ref, g_ref, b_ref, w_ref, cos_ref, sin_ref, qg_ref, kg_ref,
                xn_ref, qa_ref, ka_ref, va_ref, qb_ref, kb_ref, vb_ref):
    xn = _layer_norm_rows(x_ref[...], g_ref[...], b_ref[...])
    xn_ref[...] = xn
    xb = xn.astype(BF16)
    cos = cos_ref[...]
    sin = sin_ref[...]

    def proj(c0, width):
        return _dot(xb, w_ref[:, c0:c0 + width])

    def norm_rope(h, gain):
        y = h * lax.rsqrt(jnp.mean(h * h, axis=-1, keepdims=True) + RMS_EPS) * gain
        return y * cos + pltpu.roll(y, HEAD_DIM // 2, 1) * sin

    c = 0
    qa_ref[...] = (proj(c, Q_COLS) * SCALE).astype(BF16)
    c += Q_COLS
    ka_ref[...] = proj(c, KV_COLS).astype(BF16)
    c += KV_COLS
    va_ref[...] = proj(c, KV_COLS).astype(BF16)
    c += KV_COLS
    qg = qg_ref[...]
    for h in range(N_HEADS):
        hq = proj(c + h * HEAD_DIM, HEAD_DIM)
        qb_ref[:, h * HEAD_DIM:(h + 1) * HEAD_DIM] = (norm_rope(hq, qg) * SCALE).astype(BF16)
    c += Q_COLS
    kg = kg_ref[...]
    for h in range(N_KV):
        hk = proj(c + h * HEAD_DIM, HEAD_DIM)
        kb_ref[:, h * HEAD_DIM:(h + 1) * HEAD_DIM] = norm_rope(hk, kg).astype(BF16)
    c += KV_COLS
    vb_ref[...] = proj(c, KV_COLS).astype(BF16)


def _qkv_call(x2, g, b, w_qkv, cos_t, sin_t, qg, kg, seq_len):
    n = x2.shape[0]
    tm = QKV_TM
    pos_blocks = seq_len // tm
    row = lambda i: (i, 0)
    fixed = lambda i: (0, 0)
    pos = lambda i: (i % pos_blocks, 0)
    out_shapes = (
        jax.ShapeDtypeStruct((n, D_MODEL), F32),
        jax.ShapeDtypeStruct((n, Q_COLS), BF16),
        jax.ShapeDtypeStruct((n, KV_COLS), BF16),
        jax.ShapeDtypeStruct((n, KV_COLS), BF16),
        jax.ShapeDtypeStruct((n, Q_COLS), BF16),
        jax.ShapeDtypeStruct((n, KV_COLS), BF16),
        jax.ShapeDtypeStruct((n, KV_COLS), BF16),
    )
    return pl.pallas_call(
        _qkv_kernel,
        grid=(n // tm,),
        in_specs=[
            pl.BlockSpec((tm, D_MODEL), row),
            pl.BlockSpec((1, D_MODEL), fixed),
            pl.BlockSpec((1, D_MODEL), fixed),
            pl.BlockSpec((D_MODEL, QKV_COLS), fixed, pipeline_mode=pl.Buffered(1)),
            pl.BlockSpec((tm, HEAD_DIM), pos),
            pl.BlockSpec((tm, HEAD_DIM), pos),
            pl.BlockSpec((1, HEAD_DIM), fixed),
            pl.BlockSpec((1, HEAD_DIM), fixed),
        ],
        out_specs=(
            pl.BlockSpec((tm, D_MODEL), row),
            pl.BlockSpec((tm, Q_COLS), row),
            pl.BlockSpec((tm, KV_COLS), row),
            pl.BlockSpec((tm, KV_COLS), row),
            pl.BlockSpec((tm, Q_COLS), row),
            pl.BlockSpec((tm, KV_COLS), row),
            pl.BlockSpec((tm, KV_COLS), row),
        ),
        out_shape=out_shapes,
        compiler_params=pltpu.CompilerParams(
            dimension_semantics=("arbitrary",), vmem_limit_bytes=VMEM_LIMIT),
        name="ln_qkv",
    )(x2, g, b, w_qkv, cos_t, sin_t, qg, kg)


def _attn_a_kernel(sink_ref, q_ref, kc_ref, kp_ref, kn_ref, vc_ref, vp_ref, vn_ref, bias_ref,
                   o_ref, kcat_ref, vcat_ref, *, seq_len):
    tq = ATT_A_TQ
    i = pl.program_id(1)
    kcat_ref[0:BLOCK, :] = kp_ref[...]
    kcat_ref[BLOCK:BLOCK + tq, :] = kc_ref[...]
    kcat_ref[BLOCK + tq:, :] = kn_ref[...]
    vcat_ref[0:BLOCK, :] = vp_ref[...]
    vcat_ref[BLOCK:BLOCK + tq, :] = vc_ref[...]
    vcat_ref[BLOCK + tq:, :] = vn_ref[...]
    col = lax.broadcasted_iota(jnp.int32, (1, 3 * BLOCK), 1)

    def body(j, carry):
        r0 = pl.multiple_of(j * BLOCK, BLOCK)
        kw = kcat_ref[pl.ds(r0, 3 * BLOCK), :]
        vw = vcat_ref[pl.ds(r0, 3 * BLOCK), :]
        qj = q_ref[pl.ds(r0, BLOCK), :]
        key_pos = i * tq + j * BLOCK - BLOCK + col
        inside = (key_pos >= 0) & (key_pos < seq_len)
        for h in range(N_HEADS):
            g = h // GROUP
            s = _dot_nt(qj[:, h * HEAD_DIM:(h + 1) * HEAD_DIM], kw[:, g * HEAD_DIM:(g + 1) * HEAD_DIM])
            s = jnp.where(inside, s + bias_ref[h], NEG_BIG)
            sk = sink_ref[h]
            m = jnp.maximum(jnp.max(s, axis=-1, keepdims=True), sk)
            p = jnp.exp(s - m)
            denom = jnp.sum(p, axis=-1, keepdims=True) + jnp.exp(sk - m)
            pn = (p / denom).astype(BF16)
            o = _dot(pn, vw[:, g * HEAD_DIM:(g + 1) * HEAD_DIM])
            o_ref[pl.ds(r0, BLOCK), h * HEAD_DIM:(h + 1) * HEAD_DIM] = o.astype(BF16)
        return carry

    lax.fori_loop(0, tq // BLOCK, body, 0)


def _attn_a_call(sink, qa, ka, va, bias, bsz, seq_len):
    tq = ATT_A_TQ
    nb = seq_len // BLOCK
    bpt = tq // BLOCK
    q3 = qa.reshape(bsz, seq_len, Q_COLS)
    k3 = ka.reshape(bsz, seq_len, KV_COLS)
    v3 = va.reshape(bsz, seq_len, KV_COLS)
    cur = lambda b, i: (b, i, 0)
    prev = lambda b, i: (b, jnp.maximum(i * bpt - 1, 0), 0)
    nxt = lambda b, i: (b, jnp.minimum((i + 1) * bpt, nb - 1), 0)
    kv_cur = pl.BlockSpec((None, tq, KV_COLS), cur)
    kv_prev = pl.BlockSpec((None, BLOCK, KV_COLS), prev)
    kv_next = pl.BlockSpec((None, BLOCK, KV_COLS), nxt)
    out = pl.pallas_call(
        functools.partial(_attn_a_kernel, seq_len=seq_len),
        grid=(bsz, seq_len // tq),
        in_specs=[
            pl.BlockSpec(memory_space=pltpu.SMEM),
            pl.BlockSpec((None, tq, Q_COLS), cur),
            kv_cur, kv_prev, kv_next,
            kv_cur, kv_prev, kv_next,
            pl.BlockSpec((N_HEADS, BLOCK, 3 * BLOCK), lambda b, i: (0, 0, 0)),
        ],
        out_specs=pl.BlockSpec((None, tq, Q_COLS), cur),
        out_shape=jax.ShapeDtypeStruct((bsz, seq_len, Q_COLS), BF16),
        scratch_shapes=[
            pltpu.VMEM((tq + 2 * BLOCK, KV_COLS), BF16),
            pltpu.VMEM((tq + 2 * BLOCK, KV_COLS), BF16),
        ],
        compiler_params=pltpu.CompilerParams(
            dimension_semantics=("arbitrary", "arbitrary"), vmem_limit_bytes=VMEM_LIMIT),
        name="attn_window",
    )(sink, q3, k3, k3, k3, v3, v3, v3, bias)
    return out.reshape(bsz * seq_len, Q_COLS)


def _attn_b_kernel(q_ref, k_ref, v_ref, o_ref, qs_ref, m_ref, l_ref, acc_ref, *, seq_len):
    tq = ATT_B_TQ
    tk = min(ATT_B_TK, seq_len)
    for h in range(GROUP):
        qs_ref[h * tq:(h + 1) * tq, :] = q_ref[:, h * HEAD_DIM:(h + 1) * HEAD_DIM]
    m_ref[...] = jnp.full(m_ref.shape, NEG_BIG, F32)
    l_ref[...] = jnp.zeros(l_ref.shape, F32)
    acc_ref[...] = jnp.zeros(acc_ref.shape, F32)

    def body(c, carry):
        k0 = pl.multiple_of(c * tk, tk)
        k = k_ref[pl.ds(k0, tk), :]
        v = v_ref[pl.ds(k0, tk), :]
        s = _dot_nt(qs_ref[...], k)
        m_prev = m_ref[...]
        m_new = jnp.maximum(m_prev, jnp.max(s, axis=-1, keepdims=True))
        alpha = jnp.exp(m_prev - m_new)
        p = jnp.exp(s - m_new)
        l_ref[...] = alpha * l_ref[...] + jnp.sum(p, axis=-1, keepdims=True)
        acc_ref[...] = alpha * acc_ref[...] + _dot(p.astype(BF16), v)
        m_ref[...] = m_new
        return carry

    lax.fori_loop(0, seq_len // tk, body, 0)
    o = acc_ref[...] / l_ref[...]
    for h in range(GROUP):
        o_ref[:, h * HEAD_DIM:(h + 1) * HEAD_DIM] = o[h * tq:(h + 1) * tq, :].astype(BF16)


def _attn_b_call(qb, kb, vb, bsz, seq_len):
    tq = ATT_B_TQ
    q3 = qb.reshape(bsz, seq_len, Q_COLS)
    k3 = kb.reshape(bsz, seq_len, KV_COLS)
    v3 = vb.reshape(bsz, seq_len, KV_COLS)
    gcols = GROUP * HEAD_DIM
    out = pl.pallas_call(
        functools.partial(_attn_b_kernel, seq_len=seq_len),
        grid=(bsz, N_KV, seq_len // tq),
        in_specs=[
            pl.BlockSpec((None, tq, gcols), lambda b, g, i: (b, i, g)),
            pl.BlockSpec((None, seq_len, HEAD_DIM), lambda b, g, i: (b, 0, g)),
            pl.BlockSpec((None, seq_len, HEAD_DIM), lambda b, g, i: (b, 0, g)),
        ],
        out_specs=pl.BlockSpec((None, tq, gcols), lambda b, g, i: (b, i, g)),
        out_shape=jax.ShapeDtypeStruct((bsz, seq_len, Q_COLS), BF16),
        scratch_shapes=[
            pltpu.VMEM((GROUP * tq, HEAD_DIM), BF16),
            pltpu.VMEM((GROUP * tq, 1), F32),
            pltpu.VMEM((GROUP * tq, 1), F32),
            pltpu.VMEM((GROUP * tq, HEAD_DIM), F32),
        ],
        compiler_params=pltpu.CompilerParams(
            dimension_semantics=("arbitrary", "arbitrary", "arbitrary"), vmem_limit_bytes=VMEM_LIMIT),
        name="attn_dense",
    )(q3, k3, v3)
    return out.reshape(bsz * seq_len, Q_COLS)


def _merge_kernel(xn_ref, oa_ref, ob_ref, wa_ref, wb_ref, wga_ref, wgb_ref, bga_ref, bgb_ref,
                  wo_ref, g_ref, b_ref, wrh_ref, wrl_ref, br_ref,
                  x1_ref, lt_ref, xb_ref, acc_ref):
    c = pl.program_id(1)

    @pl.when(c == 0)
    def _():
        xb_ref[...] = xn_ref[...].astype(BF16)
        acc_ref[...] = jnp.zeros(acc_ref.shape, F32)

    xb = xb_ref[...]
    ya = _dot(oa_ref[...], wa_ref[...])
    yb = _dot(ob_ref[...], wb_ref[...])
    ga = jax.nn.sigmoid(_dot(xb, wga_ref[...]) + bga_ref[...])
    gb = jax.nn.sigmoid(_dot(xb, wgb_ref[...]) + bgb_ref[...])
    mixed = (ga * ya + gb * yb).astype(BF16)
    acc_ref[...] += _dot(mixed, wo_ref[...])

    @pl.when(c == pl.num_programs(1) - 1)
    def _():
        x1 = _layer_norm_rows(ALPHA * xn_ref[...] + acc_ref[...], g_ref[...], b_ref[...])
        x1_ref[...] = x1
        xh = x1.astype(BF16)
        xl = (x1 - xh.astype(F32)).astype(BF16)
        wrh = wrh_ref[...]
        lt = _dot_nt(wrh, xh) + (_dot_nt(wrh, xl) + _dot_nt(wrl_ref[...], xh))
        lt_ref[...] = lt + br_ref[...]


def _merge_call(xn, oa, ob, wa, wb, wg, bg, wo, g, b, wr_hi, wr_lo, br):
    n = xn.shape[0]
    tm, tc = MERGE_TM, MERGE_TC
    nc = D_MODEL // tc
    row = lambda i, c: (i, 0)
    fixed = lambda i, c: (0, 0)
    colc = lambda i, c: (0, c)
    colc_b = lambda i, c: (0, c + nc)
    return pl.pallas_call(
        _merge_kernel,
        grid=(n // tm, nc),
        in_specs=[
            pl.BlockSpec((tm, D_MODEL), row),
            pl.BlockSpec((tm, Q_COLS), row),
            pl.BlockSpec((tm, Q_COLS), row),
            pl.BlockSpec((Q_COLS, tc), colc),
            pl.BlockSpec((Q_COLS, tc), colc),
            pl.BlockSpec((D_MODEL, tc), colc),
            pl.BlockSpec((D_MODEL, tc), colc_b),
            pl.BlockSpec((1, tc), colc),
            pl.BlockSpec((1, tc), colc_b),
            pl.BlockSpec((tc, D_MODEL), lambda i, c: (c, 0)),
            pl.BlockSpec((1, D_MODEL), fixed),
            pl.BlockSpec((1, D_MODEL), fixed),
            pl.BlockSpec((ROUTER_ROWS, D_MODEL), fixed),
            pl.BlockSpec((ROUTER_ROWS, D_MODEL), fixed),
            pl.BlockSpec((ROUTER_ROWS, 1), fixed),
        ],
        out_specs=(
            pl.BlockSpec((tm, D_MODEL), row),
            pl.BlockSpec((ROUTER_ROWS, tm), lambda i, c: (0, i)),
        ),
        out_shape=(
            jax.ShapeDtypeStruct((n, D_MODEL), F32),
            jax.ShapeDtypeStruct((ROUTER_ROWS, n), F32),
        ),
        scratch_shapes=[
            pltpu.VMEM((tm, D_MODEL), BF16),
            pltpu.VMEM((tm, D_MODEL), F32),
        ],
        compiler_params=pltpu.CompilerParams(
            dimension_semantics=("arbitrary", "arbitrary"), vmem_limit_bytes=VMEM_LIMIT),
        name="merge_ln_router",
    )(xn, oa, ob, wa, wb, wg, wg, bg, bg, wo, g, b, wr_hi, wr_lo, br)


def _route_kernel(lt_ref, e_ref, w_ref):
    tn = lt_ref.shape[1]
    cl = lt_ref[0:N_GROUPS, :]
    gi = lax.broadcasted_iota(jnp.int32, (N_GROUPS, tn), 0)
    cmax = jnp.max(cl, axis=0, keepdims=True)
    ce = jnp.exp(cl - cmax)
    cp = ce / jnp.sum(ce, axis=0, keepdims=True)
    g_idx = jnp.min(jnp.where(cl == cmax, gi, N_GROUPS), axis=0, keepdims=True)
    g_prob = jnp.sum(jnp.where(gi == g_idx, cp, 0.0), axis=0, keepdims=True)

    fl = jnp.zeros((EXPERTS_PER_GROUP, tn), F32)
    for g in range(N_GROUPS):
        r0 = FINE_ROW0 + g * EXPERTS_PER_GROUP
        fl = jnp.where(g_idx == g, lt_ref[r0:r0 + EXPERTS_PER_GROUP, :], fl)
    fmax = jnp.max(fl, axis=0, keepdims=True)
    fe = jnp.exp(fl - fmax)
    fp = fe / jnp.sum(fe, axis=0, keepdims=True)
    ei = lax.broadcasted_iota(jnp.int32, (EXPERTS_PER_GROUP, tn), 0)
    p1 = jnp.max(fp, axis=0, keepdims=True)
    i1 = jnp.min(jnp.where(fp == p1, ei, EXPERTS_PER_GROUP), axis=0, keepdims=True)
    fp2 = jnp.where(ei == i1, -1.0, fp)
    p2 = jnp.max(fp2, axis=0, keepdims=True)
    i2 = jnp.min(jnp.where(fp2 == p2, ei, EXPERTS_PER_GROUP), axis=0, keepdims=True)
    tot = p1 + p2
    w1 = g_prob * p1 / tot
    w2 = g_prob * p2 / tot
    e1 = g_idx * EXPERTS_PER_GROUP + i1
    e2 = g_idx * EXPERTS_PER_GROUP + i2
    ri = lax.broadcasted_iota(jnp.int32, (8, tn), 0)
    e_ref[...] = jnp.where(ri == 0, e1, jnp.where(ri == 1, e2, 0))
    w_ref[...] = jnp.where(ri == 0, w1, jnp.where(ri == 1, w2, 0.0))


def _route_call(lt):
    n = lt.shape[1]
    tn = min(ROUTE_TN, n)
    return pl.pallas_call(
        _route_kernel,
        grid=(n // tn,),
        in_specs=[pl.BlockSpec((ROUTER_ROWS, tn), lambda i: (0, i))],
        out_specs=(pl.BlockSpec((8, tn), lambda i: (0, i)), pl.BlockSpec((8, tn), lambda i: (0, i))),
        out_shape=(jax.ShapeDtypeStruct((8, n), jnp.int32), jax.ShapeDtypeStruct((8, n), F32)),
        compiler_params=pltpu.CompilerParams(dimension_semantics=("arbitrary",)),
        name="router",
    )(lt)


def _move_kernel(sidx_ref, didx_ref, src_ref, dst_ref, sem):
    rows = sidx_ref.shape[-1]

    def copy(r):
        s = sidx_ref[0, 0, r]
        d = didx_ref[0, 0, r]
        return d, pltpu.make_async_copy(src_ref.at[pl.ds(s, 1), :], dst_ref.at[pl.ds(jnp.maximum(d, 0), 1), :], sem)

    def start(r, carry):
        d, cp = copy(r)

        @pl.when(d >= 0)
        def _():
            cp.start()
        return carry

    def wait(r, carry):
        d, cp = copy(r)

        @pl.when(d >= 0)
        def _():
            cp.wait()
        return carry

    lax.fori_loop(0, rows, start, 0)
    lax.fori_loop(0, rows, wait, 0)


def _move_rows(src, src_idx, dst_idx, n_dst):
    rows = MOVE_ROWS
    total = src_idx.shape[0]
    steps = total // rows
    s3 = src_idx.reshape(steps, 1, rows)
    d3 = dst_idx.reshape(steps, 1, rows)
    idx_spec = pl.BlockSpec((1, 1, rows), lambda i: (i, 0, 0), memory_space=pltpu.SMEM)
    return pl.pallas_call(
        _move_kernel,
        grid=(steps,),
        in_specs=[idx_spec, idx_spec, pl.BlockSpec(memory_space=pl.ANY)],
        out_specs=pl.BlockSpec(memory_space=pl.ANY),
        out_shape=jax.ShapeDtypeStruct((n_dst, src.shape[1]), src.dtype),
        scratch_shapes=[pltpu.SemaphoreType.DMA(())],
        compiler_params=pltpu.CompilerParams(dimension_semantics=("arbitrary",), has_side_effects=True),
        name="move_rows",
    )(s3, d3, src)


def _expert_kernel(blk_e_ref, x_ref, wg_ref, wu_ref, wd_ref, y_ref):
    x = x_ref[...].astype(BF16)
    hidden = jax.nn.silu(_dot(x, wg_ref[...])) * _dot(x, wu_ref[...])
    y_ref[...] = _dot(hidden.astype(BF16), wd_ref[...])


def _expert_call(blk_e, xrows, w_gate, w_up, w_down):
    n_rows = xrows.shape[0]
    rb = MOE_RB
    grid_spec = pltpu.PrefetchScalarGridSpec(
        num_scalar_prefetch=1,
        grid=(n_rows // rb,),
        in_specs=[
            pl.BlockSpec((rb, D_MODEL), lambda i, e: (i, 0)),
            pl.BlockSpec((None, D_MODEL, D_EXPERT), lambda i, e: (e[i], 0, 0)),
            pl.BlockSpec((None, D_MODEL, D_EXPERT), lambda i, e: (e[i], 0, 0)),
            pl.BlockSpec((None, D_EXPERT, D_MODEL), lambda i, e: (e[i], 0, 0)),
        ],
        out_specs=pl.BlockSpec((rb, D_MODEL), lambda i, e: (i, 0)),
    )
    return pl.pallas_call(
        _expert_kernel,
        grid_spec=grid_spec,
        out_shape=jax.ShapeDtypeStruct((n_rows, D_MODEL), F32),
        compiler_params=pltpu.CompilerParams(
            dimension_semantics=("arbitrary",), vmem_limit_bytes=VMEM_LIMIT),
        name="experts",
    )(blk_e, xrows, w_gate, w_up, w_down)


def _final_kernel(x_ref, y_ref, w_ref, g_ref, b_ref, o_ref):
    w = w_ref[...]
    moe = y_ref[:, 0:D_MODEL] * w[:, 0:1] + y_ref[:, D_MODEL:2 * D_MODEL] * w[:, 1:2]
    o_ref[...] = _layer_norm_rows(ALPHA * x_ref[...] + moe, g_ref[...], b_ref[...])


def _final_call(x1, y2, w_tok, g, b):
    n = x1.shape[0]
    tm = FINAL_TM
    row = lambda i: (i, 0)
    fixed = lambda i: (0, 0)
    return pl.pallas_call(
        _final_kernel,
        grid=(n // tm,),
        in_specs=[
            pl.BlockSpec((tm, D_MODEL), row),
            pl.BlockSpec((tm, TOP_K * D_MODEL), row),
            pl.BlockSpec((tm, TOP_K), row),
            pl.BlockSpec((1, D_MODEL), fixed),
            pl.BlockSpec((1, D_MODEL), fixed),
        ],
        out_specs=pl.BlockSpec((tm, D_MODEL), row),
        out_shape=jax.ShapeDtypeStruct((n, D_MODEL), F32),
        compiler_params=pltpu.CompilerParams(
            dimension_semantics=("arbitrary",), vmem_limit_bytes=VMEM_LIMIT),
        name="combine_ln",
    )(x1, y2, w_tok, g, b)


def _t5_bucket(rel):
    half = N_BUCKETS // 2
    exact = half // 2
    n = jnp.abs(rel)
    nf = jnp.maximum(n, 1).astype(F32)
    large = exact + (jnp.log(nf / exact) / math.log(MAX_DISTANCE / exact) * (half - exact)).astype(jnp.int32)
    large = jnp.minimum(large, half - 1)
    return jnp.where(rel > 0, half, 0) + jnp.where(n < exact, n, large)


def _window_bias(rel_bias):
    rel = jnp.arange(3 * BLOCK)[None, :] - BLOCK - jnp.arange(BLOCK)[:, None]
    bias = rel_bias.astype(F32)[_t5_bucket(rel)].transpose(2, 0, 1)
    return jnp.where((jnp.abs(rel) <= WINDOW)[None], bias, NEG_BIG)


def _rope_tables(seq_len):
    rows = seq_len // GRID_W
    row_ids = jnp.repeat(jnp.arange(rows), GRID_W).astype(F32)
    col_ids = jnp.tile(jnp.arange(GRID_W), rows).astype(F32)
    half = HEAD_DIM // 2
    inv = 1.0 / (ROPE_THETA ** (jnp.arange(0, half, 2, dtype=F32) / half))
    ang = jnp.concatenate([row_ids[:, None] * inv, col_ids[:, None] * inv], -1)
    cos, sin = jnp.cos(ang), jnp.sin(ang)
    return jnp.concatenate([cos, cos], -1), jnp.concatenate([-sin, sin], -1)


def _deinterleave_cols(w, n_heads):
    d = w.shape[0]
    return w.reshape(d, n_heads, HEAD_DIM // 2, 2).transpose(0, 1, 3, 2).reshape(d, n_heads * HEAD_DIM)


def _deinterleave_gain(g):
    return g.reshape(HEAD_DIM // 2, 2).T.reshape(1, HEAD_DIM)


def _dispatch_plan(experts, n_tok):
    rb = MOE_RB
    n_assign = n_tok * TOP_K
    n_blk = n_assign // rb + N_EXPERTS
    n_rows = n_blk * rb
    flat_e = experts.reshape(-1)
    se, order = lax.sort((flat_e, jnp.arange(n_assign, dtype=jnp.int32)), num_keys=1, is_stable=True)
    eids = jnp.arange(N_EXPERTS, dtype=jnp.int32)
    starts = jnp.searchsorted(se, eids, side="left").astype(jnp.int32)
    ends = jnp.searchsorted(se, eids, side="right").astype(jnp.int32)
    counts = ends - starts
    pcounts = (counts + rb - 1) // rb * rb
    pends = jnp.cumsum(pcounts)
    pstarts = pends - pcounts
    blk_e = jnp.minimum(
        jnp.searchsorted(pends, jnp.arange(n_blk, dtype=jnp.int32) * rb, side="right"), N_EXPERTS - 1
    ).astype(jnp.int32)
    r = jnp.arange(n_rows, dtype=jnp.int32)
    e_r = blk_e[r // rb]
    off = r - pstarts[e_r]
    valid = (off >= 0) & (off < counts[e_r])
    assign = order[jnp.clip(starts[e_r] + off, 0, n_assign - 1)]
    row_tok = jnp.where(valid, assign // TOP_K, 0).astype(jnp.int32)
    row_slot = jnp.where(valid, assign, -1).astype(jnp.int32)
    return blk_e, row_tok, row_slot, n_rows


def _trunk(x, p):
    bsz, seq_len, d = x.shape
    n = bsz * seq_len
    x2 = x.reshape(n, d)
    cos_t, sin_t = _rope_tables(seq_len)
    xn, qa, ka, va, qb, kb, vb = _qkv_call(x2, p["emb_g"], p["emb_b"], p["w_qkv"], cos_t, sin_t,
                                           p["q_gain"], p["k_gain"], seq_len)
    oa = _attn_a_call(p["sink"], qa, ka, va, p["bias_a"], bsz, seq_len)
    ob = _attn_b_call(qb, kb, vb, bsz, seq_len)
    x1, lt = _merge_call(xn, oa, ob, p["w_a"], p["w_b"], p["w_g"], p["b_g"], p["w_o"],
                         p["ln1_g"], p["ln1_b"], p["wr_hi"], p["wr_lo"], p["b_r"])
    e_rows, w_rows = _route_call(lt)
    experts = e_rows[:TOP_K].T
    w_tok = w_rows[:TOP_K].T
    blk_e, row_tok, row_slot, n_rows = _dispatch_plan(experts, n)
    xrows = _move_rows(x1, row_tok, jnp.arange(n_rows, dtype=jnp.int32), n_rows)
    yrows = _expert_call(blk_e, xrows, p["w_gate"], p["w_up"], p["w_down"])
    y2 = _move_rows(yrows, jnp.arange(n_rows, dtype=jnp.int32), row_slot, n * TOP_K)
    out = _final_call(x1, y2.reshape(n, TOP_K * d), w_tok, p["ln2_g"], p["ln2_b"])
    return out.reshape(bsz, seq_len, d)


def kernel(x_prompt, x_sample, emb_ln_g, emb_ln_b, rel_bias, w_in, b_gate, sink_a, q_norm_g, k_norm_g,
           w_branch_a, w_branch_b, w_out, ln1_g, ln1_b, w_coarse, b_coarse, w_fine, b_fine,
           w_gate, w_up, w_down, ln2_g, ln2_b):
    l = 0
    w_in_l = w_in[l]
    qa_end = Q_COLS
    kva_end = qa_end + 2 * KV_COLS
    qb_end = kva_end + Q_COLS
    kb_end = qb_end + KV_COLS
    vb_end = kb_end + KV_COLS
    w_qkv = jnp.concatenate([
        w_in_l[:, :kva_end],
        _deinterleave_cols(w_in_l[:, kva_end:qb_end], N_HEADS),
        _deinterleave_cols(w_in_l[:, qb_end:kb_end], N_KV),
        w_in_l[:, kb_end:vb_end],
    ], axis=1).astype(BF16)
    w_router = jnp.zeros((ROUTER_ROWS, D_MODEL), F32)
    w_router = w_router.at[0:N_GROUPS].set(w_coarse[l].T)
    w_router = w_router.at[FINE_ROW0:FINE_ROW0 + N_EXPERTS].set(w_fine[l].T)
    wr_hi = w_router.astype(BF16)
    wr_lo = (w_router - wr_hi.astype(F32)).astype(BF16)
    b_router = jnp.zeros((ROUTER_ROWS, 1), F32)
    b_router = b_router.at[0:N_GROUPS, 0].set(b_coarse[l].astype(F32))
    b_router = b_router.at[FINE_ROW0:FINE_ROW0 + N_EXPERTS, 0].set(b_fine[l].astype(F32))
    p = {
        "emb_g": emb_ln_g.reshape(1, D_MODEL), "emb_b": emb_ln_b.reshape(1, D_MODEL),
        "w_qkv": w_qkv,
        "q_gain": _deinterleave_gain(q_norm_g[l]), "k_gain": _deinterleave_gain(k_norm_g[l]),
        "sink": sink_a[l].astype(F32), "bias_a": _window_bias(rel_bias),
        "w_a": w_branch_a[l].astype(BF16), "w_b": w_branch_b[l].astype(BF16),
        "w_g": w_in_l[:, vb_end:].astype(BF16), "b_g": b_gate[l].reshape(1, 2 * D_MODEL),
        "w_o": w_out[l].astype(BF16),
        "ln1_g": ln1_g[l].reshape(1, D_MODEL), "ln1_b": ln1_b[l].reshape(1, D_MODEL),
        "wr_hi": wr_hi, "wr_lo": wr_lo, "b_r": b_router,
        "w_gate": w_gate[l].astype(BF16), "w_up": w_up[l].astype(BF16), "w_down": w_down[l].astype(BF16),
        "ln2_g": ln2_g[l].reshape(1, D_MODEL), "ln2_b": ln2_b[l].reshape(1, D_MODEL),
    }
    return _trunk(x_prompt, p), _trunk(x_sample, p)
```

```python
import functools
import math

import numpy as np
import jax
import jax.numpy as jnp
from jax import lax
from jax.experimental import pallas as pl
from jax.experimental.pallas import tpu as pltpu

F32 = jnp.float32
BF16 = jnp.bfloat16

D_MODEL = 2048
HEAD_DIM = 128
N_HEADS = 8
N_KV = 2
GROUP = N_HEADS // N_KV
Q_COLS = N_HEADS * HEAD_DIM
KV_COLS = N_KV * HEAD_DIM
QKV_COLS = 2 * (Q_COLS + 2 * KV_COLS)
WINDOW = 128
BLOCK = 128
GRID_W = 64
ROPE_THETA = 10000.0
N_BUCKETS = 32
MAX_DISTANCE = 128
N_GROUPS = 4
EXPERTS_PER_GROUP = 8
N_EXPERTS = N_GROUPS * EXPERTS_PER_GROUP
TOP_K = 2
D_EXPERT = 512
LN_EPS = 1e-5
RMS_EPS = 1e-6
DEPTH = 1
ALPHA = (2 * DEPTH) ** 0.25
SCALE = HEAD_DIM ** -0.5
NEG_BIG = -1e30

LANES = 128
VMEM_LIMIT = 56 * 1024 * 1024

QKV_TM = 512
ATT_A_TQ = 512
ATT_B_TQ = 256
ATT_B_TK = 1024
MERGE_TM = 512
MERGE_TC = 256
ROUTE_TN = 1024
MOE_RB = 256
MOVE_ROWS = 512
FINAL_TM = 512
ROUTER_ROWS = 128
FINE_ROW0 = 8


def _layer_norm_rows(x, g, b):
    mu = jnp.mean(x, axis=-1, keepdims=True)
    xc = x - mu
    var = jnp.mean(xc * xc, axis=-1, keepdims=True)
    return xc * lax.rsqrt(var + LN_EPS) * g + b


def _dot(a, b):
    return jnp.dot(a, b, preferred_element_type=F32)


def _dot_nt(a, b):
    return lax.dot_general(a, b, (((1,), (1,)), ((), ())), preferred_element_type=F32)


def _qkv_kernel(x_ref, g_ref, b_ref, w_ref, cos_ref, sin_ref, qg_ref, kg_ref,
                xn_ref, qa_ref, ka_ref, va_ref, qb_ref, kb_ref, vb_ref):
    xn = _layer_norm_rows(x_ref[...], g_ref[...], b_ref[...])
    xn_ref[...] = xn
    xb = xn.astype(BF16)
    cos = cos_ref[...]
    sin = sin_ref[...]

    def proj(c0, width):
        return _dot(xb, w_ref[:, c0:c0 + width])

    def norm_rope(h, gain):
        y = h * lax.rsqrt(jnp.mean(h * h, axis=-1, keepdims=True) + RMS_EPS) * gain
        return y * cos + pltpu.roll(y, HEAD_DIM // 2, 1) * sin

    c = 0
    qa_ref[...] = (proj(c, Q_COLS) * SCALE).astype(BF16)
    c += Q_COLS
    ka_ref[...] = proj(c, KV_COLS).astype(BF16)
    c += KV_COLS
    va_ref[...] = proj(c, KV_COLS).astype(BF16)
    c += KV_COLS
    qg = qg_ref[...]
    for h in range(N_HEADS):
        hq = proj(c + h * HEAD_DIM, HEAD_DIM)
        qb_ref[:, h * HEAD_DIM:(h + 1) * HEAD_DIM] = (norm_rope(hq, qg) * SCALE).astype(BF16)
    c += Q_COLS
    kg = kg_ref[...]
    for h in range(N_KV):
        hk = proj(c + h * HEAD_DIM, HEAD_DIM)
        kb_ref[:, h * HEAD_DIM:(h + 1) * HEAD_DIM] = norm_rope(hk, kg).astype(BF16)
    c += KV_COLS
    vb_ref[...] = proj(c, KV_COLS).astype(BF16)


def _qkv_call(x2, g, b, w_qkv, cos_t, sin_t, qg, kg, seq_len):
    n = x2.shape[0]
    tm = QKV_TM
    pos_blocks = seq_len // tm
    row = lambda i: (i, 0)
    fixed = lambda i: (0, 0)
    pos = lambda i: (i % pos_blocks, 0)
    out_shapes = (
        jax.ShapeDtypeStruct((n, D_MODEL), F32),
        jax.ShapeDtypeStruct((n, Q_COLS), BF16),
        jax.ShapeDtypeStruct((n, KV_COLS), BF16),
        jax.ShapeDtypeStruct((n, KV_COLS), BF16),
        jax.ShapeDtypeStruct((n, Q_COLS), BF16),
        jax.ShapeDtypeStruct((n, KV_COLS), BF16),
        jax.ShapeDtypeStruct((n, KV_COLS), BF16),
    )
    return pl.pallas_call(
        _qkv_kernel,
        grid=(n // tm,),
        in_specs=[
            pl.BlockSpec((tm, D_MODEL), row),
            pl.BlockSpec((1, D_MODEL), fixed),
            pl.BlockSpec((1, D_MODEL), fixed),
            pl.BlockSpec((D_MODEL, QKV_COLS), fixed, pipeline_mode=pl.Buffered(1)),
            pl.BlockSpec((tm, HEAD_DIM), pos),
            pl.BlockSpec((tm, HEAD_DIM), pos),
            pl.BlockSpec((1, HEAD_DIM), fixed),
            pl.BlockSpec((1, HEAD_DIM), fixed),
        ],
        out_specs=(
            pl.BlockSpec((tm, D_MODEL), row),
            pl.BlockSpec((tm, Q_COLS), row),
            pl.BlockSpec((tm, KV_COLS), row),
            pl.BlockSpec((tm, KV_COLS), row),
            pl.BlockSpec((tm, Q_COLS), row),
            pl.BlockSpec((tm, KV_COLS), row),
            pl.BlockSpec((tm, KV_COLS), row),
        ),
        out_shape=out_shapes,
        compiler_params=pltpu.CompilerParams(
            dimension_semantics=("arbitrary",), vmem_limit_bytes=VMEM_LIMIT),
        name="ln_qkv",
    )(x2, g, b, w_qkv, cos_t, sin_t, qg, kg)


def _attn_a_kernel(sink_ref, q_ref, kc_ref, kp_ref, kn_ref, vc_ref, vp_ref, vn_ref, bias_ref,
                   o_ref, kcat_ref, vcat_ref, *, seq_len):
    tq = ATT_A_TQ
    i = pl.program_id(1)
    kcat_ref[0:BLOCK, :] = kp_ref[...]
    kcat_ref[BLOCK:BLOCK + tq, :] = kc_ref[...]
    kcat_ref[BLOCK + tq:, :] = kn_ref[...]
    vcat_ref[0:BLOCK, :] = vp_ref[...]
    vcat_ref[BLOCK:BLOCK + tq, :] = vc_ref[...]
    vcat_ref[BLOCK + tq:, :] = vn_ref[...]
    col = lax.broadcasted_iota(jnp.int32, (1, 3 * BLOCK), 1)

    def body(j, carry):
        r0 = pl.multiple_of(j * BLOCK, BLOCK)
        kw = kcat_ref[pl.ds(r0, 3 * BLOCK), :]
        vw = vcat_ref[pl.ds(r0, 3 * BLOCK), :]
        qj = q_ref[pl.ds(r0, BLOCK), :]
        key_pos = i * tq + j * BLOCK - BLOCK + col
        inside = (key_pos >= 0) & (key_pos < seq_len)
        for h in range(N_HEADS):
            g = h // GROUP
            s = _dot_nt(qj[:, h * HEAD_DIM:(h + 1) * HEAD_DIM], kw[:, g * HEAD_DIM:(g + 1) * HEAD_DIM])
            s = jnp.where(inside, s + bias_ref[h], NEG_BIG)
            sk = sink_ref[h]
            m = jnp.maximum(jnp.max(s, axis=-1, keepdims=True), sk)
            p = jnp.exp(s - m)
            denom = jnp.sum(p, axis=-1, keepdims=True) + jnp.exp(sk - m)
            pn = (p / denom).astype(BF16)
            o = _dot(pn, vw[:, g * HEAD_DIM:(g + 1) * HEAD_DIM])
            o_ref[pl.ds(r0, BLOCK), h * HEAD_DIM:(h + 1) * HEAD_DIM] = o.astype(BF16)
        return carry

    lax.fori_loop(0, tq // BLOCK, body, 0)


def _attn_a_call(sink, qa, ka, va, bias, bsz, seq_len):
    tq = ATT_A_TQ
    nb = seq_len // BLOCK
    bpt = tq // BLOCK
    q3 = qa.reshape(bsz, seq_len, Q_COLS)
    k3 = ka.reshape(bsz, seq_len, KV_COLS)
    v3 = va.reshape(bsz, seq_len, KV_COLS)
    cur = lambda b, i: (b, i, 0)
    prev = lambda b, i: (b, jnp.maximum(i * bpt - 1, 0), 0)
    nxt = lambda b, i: (b, jnp.minimum((i + 1) * bpt, nb - 1), 0)
    kv_cur = pl.BlockSpec((None, tq, KV_COLS), cur)
    kv_prev = pl.BlockSpec((None, BLOCK, KV_COLS), prev)
    kv_next = pl.BlockSpec((None, BLOCK, KV_COLS), nxt)
    out = pl.pallas_call(
        functools.partial(_attn_a_kernel, seq_len=seq_len),
        grid=(bsz, seq_len // tq),
        in_specs=[
            pl.BlockSpec(memory_space=pltpu.SMEM),
            pl.BlockSpec((None, tq, Q_COLS), cur),
            kv_cur, kv_prev, kv_next,
            kv_cur, kv_prev, kv_next,
            pl.BlockSpec((N_HEADS, BLOCK, 3 * BLOCK), lambda b, i: (0, 0, 0)),
        ],
        out_specs=pl.BlockSpec((None, tq, Q_COLS), cur),
        out_shape=jax.ShapeDtypeStruct((bsz, seq_len, Q_COLS), BF16),
        scratch_shapes=[
            pltpu.VMEM((tq + 2 * BLOCK, KV_COLS), BF16),
            pltpu.VMEM((tq + 2 * BLOCK, KV_COLS), BF16),
        ],
        compiler_params=pltpu.CompilerParams(
            dimension_semantics=("arbitrary", "arbitrary"), vmem_limit_bytes=VMEM_LIMIT),
        name="attn_window",
    )(sink, q3, k3, k3, k3, v3, v3, v3, bias)
    return out.reshape(bsz * seq_len, Q_COLS)


def _attn_b_kernel(q_ref, k_ref, v_ref, o_ref, qs_ref, m_ref, l_ref, acc_ref,
                   s0_ref, s1_ref, p0_ref, p1_ref, a0_ref, a1_ref, *, seq_len):
    tq = ATT_B_TQ
    tk = ATT_B_TK
    n_chunks = seq_len // tk
    for h in range(GROUP):
        qs_ref[h * tq:(h + 1) * tq, :] = q_ref[:, h * HEAD_DIM:(h + 1) * HEAD_DIM]
    m_ref[...] = jnp.full(m_ref.shape, NEG_BIG, F32)
    l_ref[...] = jnp.zeros(l_ref.shape, F32)
    acc_ref[...] = jnp.zeros(acc_ref.shape, F32)

    def scores(c, s_ref):
        k0 = pl.multiple_of(c * tk, tk)
        s_ref[...] = _dot_nt(qs_ref[...], k_ref[pl.ds(k0, tk), :])

    def softmax(s_ref, p_ref, a_ref):
        s = s_ref[...]
        m_prev = m_ref[...]
        m_new = jnp.maximum(m_prev, jnp.max(s, axis=-1, keepdims=True))
        alpha = jnp.exp(m_prev - m_new)
        p = jnp.exp(s - m_new)
        l_ref[...] = alpha * l_ref[...] + jnp.sum(p, axis=-1, keepdims=True)
        m_ref[...] = m_new
        a_ref[...] = alpha
        p_ref[...] = p.astype(BF16)

    def weighted_values(c, p_ref, a_ref):
        k0 = pl.multiple_of(c * tk, tk)
        acc_ref[...] = a_ref[...] * acc_ref[...] + _dot(p_ref[...], v_ref[pl.ds(k0, tk), :])

    scores(0, s0_ref)
    scores(1, s1_ref)
    softmax(s0_ref, p0_ref, a0_ref)

    def body(t, carry):
        c = 2 * t + 1
        scores(c + 1, s0_ref)
        softmax(s1_ref, p1_ref, a1_ref)
        weighted_values(c - 1, p0_ref, a0_ref)
        scores(c + 2, s1_ref)
        softmax(s0_ref, p0_ref, a0_ref)
        weighted_values(c, p1_ref, a1_ref)
        return carry

    lax.fori_loop(0, (n_chunks - 2) // 2, body, 0)
    softmax(s1_ref, p1_ref, a1_ref)
    weighted_values(n_chunks - 2, p0_ref, a0_ref)
    weighted_values(n_chunks - 1, p1_ref, a1_ref)
    o = acc_ref[...] / l_ref[...]
    for h in range(GROUP):
        o_ref[:, h * HEAD_DIM:(h + 1) * HEAD_DIM] = o[h * tq:(h + 1) * tq, :].astype(BF16)


def _attn_b_call(qb, kb, vb, bsz, seq_len):
    tq = ATT_B_TQ
    tk = ATT_B_TK
    assert seq_len % tk == 0 and (seq_len // tk) % 2 == 0, "dense mixer pipeline needs an even number of key chunks"
    q3 = qb.reshape(bsz, seq_len, Q_COLS)
    k3 = kb.reshape(bsz, seq_len, KV_COLS)
    v3 = vb.reshape(bsz, seq_len, KV_COLS)
    gcols = GROUP * HEAD_DIM
    out = pl.pallas_call(
        functools.partial(_attn_b_kernel, seq_len=seq_len),
        grid=(bsz, N_KV, seq_len // tq),
        in_specs=[
            pl.BlockSpec((None, tq, gcols), lambda b, g, i: (b, i, g)),
            pl.BlockSpec((None, seq_len, HEAD_DIM), lambda b, g, i: (b, 0, g)),
            pl.BlockSpec((None, seq_len, HEAD_DIM), lambda b, g, i: (b, 0, g)),
        ],
        out_specs=pl.BlockSpec((None, tq, gcols), lambda b, g, i: (b, i, g)),
        out_shape=jax.ShapeDtypeStruct((bsz, seq_len, Q_COLS), BF16),
        scratch_shapes=[
            pltpu.VMEM((GROUP * tq, HEAD_DIM), BF16),
            pltpu.VMEM((GROUP * tq, 1), F32),
            pltpu.VMEM((GROUP * tq, 1), F32),
            pltpu.VMEM((GROUP * tq, HEAD_DIM), F32),
            pltpu.VMEM((GROUP * tq, tk), F32),
            pltpu.VMEM((GROUP * tq, tk), F32),
            pltpu.VMEM((GROUP * tq, tk), BF16),
            pltpu.VMEM((GROUP * tq, tk), BF16),
            pltpu.VMEM((GROUP * tq, 1), F32),
            pltpu.VMEM((GROUP * tq, 1), F32),
        ],
        compiler_params=pltpu.CompilerParams(
            dimension_semantics=("arbitrary", "arbitrary", "arbitrary"), vmem_limit_bytes=VMEM_LIMIT),
        name="attn_dense",
    )(q3, k3, v3)
    return out.reshape(bsz * seq_len, Q_COLS)


def _merge_kernel(xn_ref, oa_ref, ob_ref, wa_ref, wb_ref, wga_ref, wgb_ref, bga_ref, bgb_ref,
                  wo_ref, g_ref, b_ref, wrh_ref, wrl_ref, br_ref,
                  x1rows_ref, lt_ref, xb_ref, acc_ref):
    c = pl.program_id(1)

    @pl.when(c == 0)
    def _():
        xb_ref[...] = xn_ref[...].astype(BF16)
        acc_ref[...] = jnp.zeros(acc_ref.shape, F32)

    xb = xb_ref[...]
    ya = _dot(oa_ref[...], wa_ref[...])
    yb = _dot(ob_ref[...], wb_ref[...])
    ga = jax.nn.sigmoid(_dot(xb, wga_ref[...]) + bga_ref[...])
    gb = jax.nn.sigmoid(_dot(xb, wgb_ref[...]) + bgb_ref[...])
    mixed = (ga * ya + gb * yb).astype(BF16)
    acc_ref[...] += _dot(mixed, wo_ref[...])

    @pl.when(c == pl.num_programs(1) - 1)
    def _():
        x1 = _layer_norm_rows(ALPHA * xn_ref[...] + acc_ref[...], g_ref[...], b_ref[...])
        x1rows_ref[...] = x1.reshape(x1.shape[0], 1, D_MODEL)
        xh = x1.astype(BF16)
        xl = (x1 - xh.astype(F32)).astype(BF16)
        wrh = wrh_ref[...]
        lt = _dot_nt(wrh, xh) + (_dot_nt(wrh, xl) + _dot_nt(wrl_ref[...], xh))
        lt_ref[...] = lt + br_ref[...]


def _merge_call(xn, oa, ob, wa, wb, wg, bg, wo, g, b, wr_hi, wr_lo, br):
    n = xn.shape[0]
    tm, tc = MERGE_TM, MERGE_TC
    nc = D_MODEL // tc
    row = lambda i, c: (i, 0)
    fixed = lambda i, c: (0, 0)
    colc = lambda i, c: (0, c)
    colc_b = lambda i, c: (0, c + nc)
    return pl.pallas_call(
        _merge_kernel,
        grid=(n // tm, nc),
        in_specs=[
            pl.BlockSpec((tm, D_MODEL), row),
            pl.BlockSpec((tm, Q_COLS), row),
            pl.BlockSpec((tm, Q_COLS), row),
            pl.BlockSpec((Q_COLS, tc), colc),
            pl.BlockSpec((Q_COLS, tc), colc),
            pl.BlockSpec((D_MODEL, tc), colc),
            pl.BlockSpec((D_MODEL, tc), colc_b),
            pl.BlockSpec((1, tc), colc),
            pl.BlockSpec((1, tc), colc_b),
            pl.BlockSpec((tc, D_MODEL), lambda i, c: (c, 0)),
            pl.BlockSpec((1, D_MODEL), fixed),
            pl.BlockSpec((1, D_MODEL), fixed),
            pl.BlockSpec((ROUTER_ROWS, D_MODEL), fixed),
            pl.BlockSpec((ROUTER_ROWS, D_MODEL), fixed),
            pl.BlockSpec((ROUTER_ROWS, 1), fixed),
        ],
        out_specs=(
            pl.BlockSpec((tm, 1, D_MODEL), lambda i, c: (i, 0, 0)),
            pl.BlockSpec((ROUTER_ROWS, tm), lambda i, c: (0, i)),
        ),
        out_shape=(
            jax.ShapeDtypeStruct((n, 1, D_MODEL), F32),
            jax.ShapeDtypeStruct((ROUTER_ROWS, n), F32),
        ),
        scratch_shapes=[
            pltpu.VMEM((tm, D_MODEL), BF16),
            pltpu.VMEM((tm, D_MODEL), F32),
        ],
        compiler_params=pltpu.CompilerParams(
            dimension_semantics=("arbitrary", "arbitrary"), vmem_limit_bytes=VMEM_LIMIT),
        name="merge_ln_router",
    )(xn, oa, ob, wa, wb, wg, wg, bg, bg, wo, g, b, wr_hi, wr_lo, br)


def _route_kernel(lt_ref, e_ref, w_ref, rank_ref, cnt_ref, tri_ref, carry_ref):
    tn = lt_ref.shape[1]

    @pl.when(pl.program_id(0) == 0)
    def _():
        ri_t = lax.broadcasted_iota(jnp.int32, (tn, tn), 0)
        ci_t = lax.broadcasted_iota(jnp.int32, (tn, tn), 1)
        tri_ref[...] = jnp.where(ri_t <= ci_t, 1.0, 0.0).astype(BF16)
        carry_ref[...] = jnp.zeros(carry_ref.shape, F32)

    cl = lt_ref[0:N_GROUPS, :]
    gi = lax.broadcasted_iota(jnp.int32, (N_GROUPS, tn), 0)
    cmax = jnp.max(cl, axis=0, keepdims=True)
    ce = jnp.exp(cl - cmax)
    cp = ce / jnp.sum(ce, axis=0, keepdims=True)
    g_idx = jnp.min(jnp.where(cl == cmax, gi, N_GROUPS), axis=0, keepdims=True)
    g_prob = jnp.sum(jnp.where(gi == g_idx, cp, 0.0), axis=0, keepdims=True)

    fl = jnp.zeros((EXPERTS_PER_GROUP, tn), F32)
    for g in range(N_GROUPS):
        r0 = FINE_ROW0 + g * EXPERTS_PER_GROUP
        fl = jnp.where(g_idx == g, lt_ref[r0:r0 + EXPERTS_PER_GROUP, :], fl)
    fmax = jnp.max(fl, axis=0, keepdims=True)
    fe = jnp.exp(fl - fmax)
    fp = fe / jnp.sum(fe, axis=0, keepdims=True)
    ei = lax.broadcasted_iota(jnp.int32, (EXPERTS_PER_GROUP, tn), 0)
    p1 = jnp.max(fp, axis=0, keepdims=True)
    i1 = jnp.min(jnp.where(fp == p1, ei, EXPERTS_PER_GROUP), axis=0, keepdims=True)
    fp2 = jnp.where(ei == i1, -1.0, fp)
    p2 = jnp.max(fp2, axis=0, keepdims=True)
    i2 = jnp.min(jnp.where(fp2 == p2, ei, EXPERTS_PER_GROUP), axis=0, keepdims=True)
    tot = p1 + p2
    w1 = g_prob * p1 / tot
    w2 = g_prob * p2 / tot
    e1 = g_idx * EXPERTS_PER_GROUP + i1
    e2 = g_idx * EXPERTS_PER_GROUP + i2
    ri = lax.broadcasted_iota(jnp.int32, (8, tn), 0)
    e_ref[...] = jnp.where(ri == 0, e1, jnp.where(ri == 1, e2, 0))
    w_ref[...] = jnp.where(ri == 0, w1, jnp.where(ri == 1, w2, 0.0))

    xi = lax.broadcasted_iota(jnp.int32, (N_EXPERTS, tn), 0)
    oh1 = jnp.where(xi == e1, 1.0, 0.0)
    oh2 = jnp.where(xi == e2, 1.0, 0.0)
    tri = tri_ref[...]
    run1 = _dot(oh1.astype(BF16), tri)
    run2 = _dot(oh2.astype(BF16), tri)
    tot1 = jnp.sum(oh1, axis=1, keepdims=True)
    tot2 = jnp.sum(oh2, axis=1, keepdims=True)
    base = carry_ref[...]
    rank1 = jnp.sum(oh1 * (run1 + base), axis=0, keepdims=True) - 1.0
    rank2 = jnp.sum(oh2 * (run2 + (base + tot1)), axis=0, keepdims=True) - 1.0
    rank_ref[...] = jnp.where(ri == 0, rank1, jnp.where(ri == 1, rank2, 0.0)).astype(jnp.int32)
    total = base + tot1 + tot2
    carry_ref[...] = total
    cnt_ref[...] = jnp.broadcast_to(total, cnt_ref.shape)


def _route_call(lt):
    n = lt.shape[1]
    tn = min(ROUTE_TN, n)
    tok = lambda i: (0, i)
    return pl.pallas_call(
        _route_kernel,
        grid=(n // tn,),
        in_specs=[pl.BlockSpec((ROUTER_ROWS, tn), tok)],
        out_specs=(pl.BlockSpec((8, tn), tok), pl.BlockSpec((8, tn), tok), pl.BlockSpec((8, tn), tok),
                   pl.BlockSpec((N_EXPERTS, LANES), lambda i: (0, 0))),
        out_shape=(jax.ShapeDtypeStruct((8, n), jnp.int32), jax.ShapeDtypeStruct((8, n), F32),
                   jax.ShapeDtypeStruct((8, n), jnp.int32), jax.ShapeDtypeStruct((N_EXPERTS, LANES), F32)),
        scratch_shapes=[pltpu.VMEM((tn, tn), BF16), pltpu.VMEM((N_EXPERTS, 1), F32)],
        compiler_params=pltpu.CompilerParams(dimension_semantics=("arbitrary",)),
        name="router",
    )(lt)


def _move_kernel(sidx_ref, didx_ref, src_ref, dst_ref, sem):
    rows = sidx_ref.shape[-1]

    def row_copy(r):
        s = sidx_ref[0, 0, r]
        d = didx_ref[0, 0, r]
        return pltpu.make_async_copy(src_ref.at[pl.ds(s, 1)], dst_ref.at[pl.ds(d, 1)], sem)

    def start(r, carry):
        row_copy(r).start()
        return carry

    def wait(r, carry):
        row_copy(r).wait()
        return carry

    lax.fori_loop(0, rows, start, 0, unroll=8)
    lax.fori_loop(0, rows, wait, 0, unroll=8)


def _move_rows(src, src_idx, dst_idx, n_dst):
    rows = MOVE_ROWS
    total = src_idx.shape[0]
    steps = total // rows
    s3 = src_idx.reshape(steps, 1, rows)
    d3 = dst_idx.reshape(steps, 1, rows)
    idx_spec = pl.BlockSpec((1, 1, rows), lambda i: (i, 0, 0), memory_space=pltpu.SMEM)
    return pl.pallas_call(
        _move_kernel,
        grid=(steps,),
        in_specs=[idx_spec, idx_spec, pl.BlockSpec(memory_space=pl.ANY)],
        out_specs=pl.BlockSpec(memory_space=pl.ANY),
        out_shape=jax.ShapeDtypeStruct((n_dst,) + src.shape[1:], src.dtype),
        scratch_shapes=[pltpu.SemaphoreType.DMA(())],
        compiler_params=pltpu.CompilerParams(dimension_semantics=("arbitrary",)),
        name="move_rows",
    )(s3, d3, src)


def _plan_kernel(starts_ref, vb_ref, ve_ref, vlo_ref, vhi_ref, *, rb):
    n_visits = vb_ref.shape[0]
    shift = rb.bit_length() - 1

    def expert_body(e, v):
        end = starts_ref[e + 1]

        def cond(state):
            return state[0] < end

        def body(state):
            r, v = state
            b = lax.shift_right_logical(r, shift)
            hi = jnp.minimum(end, (b + 1) * rb)
            vb_ref[v] = b
            ve_ref[v] = e
            vlo_ref[v] = r - b * rb
            vhi_ref[v] = hi - b * rb
            return hi, v + 1

        return lax.while_loop(cond, body, (starts_ref[e], v))[1]

    used = lax.fori_loop(0, N_EXPERTS, expert_body, 0)
    last_b = vb_ref[used - 1]
    last_e = ve_ref[used - 1]

    def pad(i, carry):
        vb_ref[i] = last_b
        ve_ref[i] = last_e
        vlo_ref[i] = 0
        vhi_ref[i] = 0
        return carry

    lax.fori_loop(used, n_visits, pad, 0)


def _plan_call(starts, n_assign, rb):
    assert rb & (rb - 1) == 0 and n_assign % rb == 0
    n_visits = n_assign // rb + N_EXPERTS - 1
    smem = pl.BlockSpec(memory_space=pltpu.SMEM)
    sds = jax.ShapeDtypeStruct((n_visits,), jnp.int32)
    return pl.pallas_call(
        functools.partial(_plan_kernel, rb=rb),
        in_specs=[smem], out_specs=(smem,) * 4, out_shape=(sds,) * 4, name="visit_plan",
    )(starts)


def _expert_kernel(vb_ref, ve_ref, vlo_ref, vhi_ref, x_ref, wg_ref, wu_ref, wd_ref, y_ref,
                   xf_ref, xb_ref, ys_ref):
    rb = xb_ref.shape[0]
    v = pl.program_id(0)
    lo = vlo_ref[v]
    hi = vhi_ref[v]

    @pl.when(hi > lo)
    def _():
        @pl.when(lo == 0)
        def _():
            xf_ref[...] = x_ref[...].reshape(rb, D_MODEL)
            xb_ref[...] = xf_ref[...].astype(BF16)

        x = xb_ref[...]
        hidden = jax.nn.silu(_dot(x, wg_ref[...])) * _dot(x, wu_ref[...])
        y = _dot(hidden.astype(BF16), wd_ref[...])
        whole = (lo == 0) & (hi == rb)

        @pl.when(whole)
        def _():
            y_ref[...] = y.reshape(rb, 1, D_MODEL)

        @pl.when(jnp.logical_not(whole))
        def _():
            rows = lax.broadcasted_iota(jnp.int32, (rb, 1), 0)
            mine = (rows >= lo) & (rows < hi)

            @pl.when(lo == 0)
            def _():
                ys_ref[...] = jnp.where(mine, y, 0.0)

            @pl.when(lo > 0)
            def _():
                ys_ref[...] = jnp.where(mine, y, ys_ref[...])

            @pl.when(hi == rb)
            def _():
                y_ref[...] = ys_ref[...].reshape(rb, 1, D_MODEL)


def _expert_call(plan, xrows, w_gate, w_up, w_down, rb):
    n_rows = xrows.shape[0]
    n_visits = plan[0].shape[0]
    rows_map = lambda v, vb, ve, vlo, vhi: (vb[v], 0, 0)
    w_map = lambda v, vb, ve, vlo, vhi: (ve[v], 0, 0)
    grid_spec = pltpu.PrefetchScalarGridSpec(
        num_scalar_prefetch=4,
        grid=(n_visits,),
        in_specs=[
            pl.BlockSpec((rb, 1, D_MODEL), rows_map),
            pl.BlockSpec((None, D_MODEL, D_EXPERT), w_map),
            pl.BlockSpec((None, D_MODEL, D_EXPERT), w_map),
            pl.BlockSpec((None, D_EXPERT, D_MODEL), w_map),
        ],
        out_specs=pl.BlockSpec((rb, 1, D_MODEL), rows_map),
        scratch_shapes=[
            pltpu.VMEM((rb, D_MODEL), F32),
            pltpu.VMEM((rb, D_MODEL), BF16),
            pltpu.VMEM((rb, D_MODEL), F32),
        ],
    )
    return pl.pallas_call(
        _expert_kernel,
        grid_spec=grid_spec,
        out_shape=jax.ShapeDtypeStruct((n_rows, 1, D_MODEL), F32),
        compiler_params=pltpu.CompilerParams(
            dimension_semantics=("arbitrary",), vmem_limit_bytes=VMEM_LIMIT),
        name="experts",
    )(*plan, xrows, w_gate, w_up, w_down)


def _final_kernel(x_ref, y0_ref, y1_ref, w_ref, g_ref, b_ref, o_ref, xs_ref, ya_ref, yb_ref):
    tm = o_ref.shape[0]
    xs_ref[...] = x_ref[...].reshape(tm, D_MODEL)
    ya_ref[...] = y0_ref[...].reshape(tm, D_MODEL)
    yb_ref[...] = y1_ref[...].reshape(tm, D_MODEL)
    w = w_ref[...]
    moe = ya_ref[...] * w[:, 0:1] + yb_ref[...] * w[:, 1:2]
    o_ref[...] = _layer_norm_rows(ALPHA * xs_ref[...] + moe, g_ref[...], b_ref[...])


def _final_call(x1rows, y2, w_tok, g, b):
    n = x1rows.shape[0]
    tm = FINAL_TM
    fixed = lambda i: (0, 0)
    return pl.pallas_call(
        _final_kernel,
        grid=(n // tm,),
        in_specs=[
            pl.BlockSpec((tm, 1, D_MODEL), lambda i: (i, 0, 0)),
            pl.BlockSpec((None, tm, 1, D_MODEL), lambda i: (0, i, 0, 0)),
            pl.BlockSpec((None, tm, 1, D_MODEL), lambda i: (1, i, 0, 0)),
            pl.BlockSpec((tm, TOP_K), lambda i: (i, 0)),
            pl.BlockSpec((1, D_MODEL), fixed),
            pl.BlockSpec((1, D_MODEL), fixed),
        ],
        out_specs=pl.BlockSpec((tm, D_MODEL), lambda i: (i, 0)),
        out_shape=jax.ShapeDtypeStruct((n, D_MODEL), F32),
        scratch_shapes=[pltpu.VMEM((tm, D_MODEL), F32)] * 3,
        compiler_params=pltpu.CompilerParams(
            dimension_semantics=("arbitrary",), vmem_limit_bytes=VMEM_LIMIT),
        name="combine_ln",
    )(x1rows, y2, y2, w_tok, g, b)


def _t5_bucket(rel):
    half = N_BUCKETS // 2
    exact = half // 2
    n = jnp.abs(rel)
    nf = jnp.maximum(n, 1).astype(F32)
    large = exact + (jnp.log(nf / exact) / math.log(MAX_DISTANCE / exact) * (half - exact)).astype(jnp.int32)
    large = jnp.minimum(large, half - 1)
    return jnp.where(rel > 0, half, 0) + jnp.where(n < exact, n, large)


def _window_bias(rel_bias):
    rel = jnp.arange(3 * BLOCK)[None, :] - BLOCK - jnp.arange(BLOCK)[:, None]
    onehot = (_t5_bucket(rel)[..., None] == jnp.arange(N_BUCKETS)).astype(F32)
    bias = jnp.einsum("qkb,bh->hqk", onehot, rel_bias.astype(F32), precision=lax.Precision.HIGHEST)
    return jnp.where((jnp.abs(rel) <= WINDOW)[None], bias, NEG_BIG)


def _rope_tables(seq_len):
    rows = seq_len // GRID_W
    row_ids = jnp.repeat(jnp.arange(rows), GRID_W).astype(F32)
    col_ids = jnp.tile(jnp.arange(GRID_W), rows).astype(F32)
    half = HEAD_DIM // 2
    inv = 1.0 / (ROPE_THETA ** (jnp.arange(0, half, 2, dtype=F32) / half))
    ang = jnp.concatenate([row_ids[:, None] * inv, col_ids[:, None] * inv], -1)
    cos, sin = jnp.cos(ang), jnp.sin(ang)
    return jnp.concatenate([cos, cos], -1), jnp.concatenate([-sin, sin], -1)


def _deinterleave_cols(w, n_heads):
    d = w.shape[0]
    return w.reshape(d, n_heads, HEAD_DIM // 2, 2).transpose(0, 1, 3, 2).reshape(d, n_heads * HEAD_DIM)


def _deinterleave_gain(g):
    return g.reshape(HEAD_DIM // 2, 2).T.reshape(1, HEAD_DIM)


def _expert_row_block(n_assign):
    return MOE_RB if n_assign // N_EXPERTS >= 8 * MOE_RB else MOE_RB // 2


def _trunk(x, p):
    bsz, seq_len, d = x.shape
    n = bsz * seq_len
    n_assign = n * TOP_K
    x2 = x.reshape(n, d)
    cos_t, sin_t = _rope_tables(seq_len)
    xn, qa, ka, va, qb, kb, vb = _qkv_call(x2, p["emb_g"], p["emb_b"], p["w_qkv"], cos_t, sin_t,
                                           p["q_gain"], p["k_gain"], seq_len)
    oa = _attn_a_call(p["sink"], qa, ka, va, p["bias_a"], bsz, seq_len)
    ob = _attn_b_call(qb, kb, vb, bsz, seq_len)
    x1rows, lt = _merge_call(xn, oa, ob, p["w_a"], p["w_b"], p["w_g"], p["b_g"], p["w_o"],
                             p["ln1_g"], p["ln1_b"], p["wr_hi"], p["wr_lo"], p["b_r"])
    e_rows, w_rows, rank_rows, cnt = _route_call(lt)
    e_flat = e_rows[:TOP_K].reshape(n_assign)
    counts = cnt[:, 0].astype(jnp.int32)
    ends = jnp.cumsum(counts)
    starts = ends - counts
    start_of = jnp.sum(jnp.where(e_flat[:, None] == jnp.arange(N_EXPERTS), starts[None, :], 0), axis=1)
    dest = (start_of + rank_rows[:TOP_K].reshape(n_assign)).astype(jnp.int32)
    token = jnp.tile(jnp.arange(n, dtype=jnp.int32), TOP_K)
    slot = jnp.arange(n_assign, dtype=jnp.int32)
    rb = _expert_row_block(n_assign)
    plan = _plan_call(jnp.concatenate([starts, ends[-1:]]).astype(jnp.int32), n_assign, rb)
    xrows = _move_rows(x1rows, token, dest, n_assign)
    yrows = _expert_call(plan, xrows, p["w_gate"], p["w_up"], p["w_down"], rb)
    y2 = _move_rows(yrows, dest, slot, n_assign).reshape(TOP_K, n, 1, d)
    out = _final_call(x1rows, y2, w_rows[:TOP_K].T, p["ln2_g"], p["ln2_b"])
    return out.reshape(bsz, seq_len, d)


def kernel(x_prompt, x_sample, emb_ln_g, emb_ln_b, rel_bias, w_in, b_gate, sink_a, q_norm_g, k_norm_g,
           w_branch_a, w_branch_b, w_out, ln1_g, ln1_b, w_coarse, b_coarse, w_fine, b_fine,
           w_gate, w_up, w_down, ln2_g, ln2_b):
    l = 0
    w_in_l = w_in[l]
    qa_end = Q_COLS
    kva_end = qa_end + 2 * KV_COLS
    qb_end = kva_end + Q_COLS
    kb_end = qb_end + KV_COLS
    vb_end = kb_end + KV_COLS
    w_qkv = jnp.concatenate([
        w_in_l[:, :kva_end],
        _deinterleave_cols(w_in_l[:, kva_end:qb_end], N_HEADS),
        _deinterleave_cols(w_in_l[:, qb_end:kb_end], N_KV),
        w_in_l[:, kb_end:vb_end],
    ], axis=1).astype(BF16)
    w_router = jnp.zeros((ROUTER_ROWS, D_MODEL), F32)
    w_router = w_router.at[0:N_GROUPS].set(w_coarse[l].T)
    w_router = w_router.at[FINE_ROW0:FINE_ROW0 + N_EXPERTS].set(w_fine[l].T)
    wr_hi = w_router.astype(BF16)
    wr_lo = (w_router - wr_hi.astype(F32)).astype(BF16)
    b_router = jnp.zeros((ROUTER_ROWS, 1), F32)
    b_router = b_router.at[0:N_GROUPS, 0].set(b_coarse[l].astype(F32))
    b_router = b_router.at[FINE_ROW0:FINE_ROW0 + N_EXPERTS, 0].set(b_fine[l].astype(F32))
    p = {
        "emb_g": emb_ln_g.reshape(1, D_MODEL), "emb_b": emb_ln_b.reshape(1, D_MODEL),
        "w_qkv": w_qkv,
        "q_gain": _deinterleave_gain(q_norm_g[l]), "k_gain": _deinterleave_gain(k_norm_g[l]),
        "sink": sink_a[l].astype(F32), "bias_a": _window_bias(rel_bias),
        "w_a": w_branch_a[l].astype(BF16), "w_b": w_branch_b[l].astype(BF16),
        "w_g": w_in_l[:, vb_end:].astype(BF16), "b_g": b_gate[l].reshape(1, 2 * D_MODEL),
        "w_o": w_out[l].astype(BF16),
        "ln1_g": ln1_g[l].reshape(1, D_MODEL), "ln1_b": ln1_b[l].reshape(1, D_MODEL),
        "wr_hi": wr_hi, "wr_lo": wr_lo, "b_r": b_router,
        "w_gate": w_gate[l].astype(BF16), "w_up": w_up[l].astype(BF16), "w_down": w_down[l].astype(BF16),
        "ln2_g": ln2_g[l].reshape(1, D_MODEL), "ln2_b": ln2_b[l].reshape(1, D_MODEL),
    }
    return _trunk(x_prompt, p), _trunk(x_sample, p)
```

```python
import functools
import math

import numpy as np
import jax
import jax.numpy as jnp
from jax import lax
from jax.experimental import pallas as pl
from jax.experimental.pallas import tpu as pltpu

F32 = jnp.float32
BF16 = jnp.bfloat16

D_MODEL = 2048
HEAD_DIM = 128
N_HEADS = 8
N_KV = 2
GROUP = N_HEADS // N_KV
Q_COLS = N_HEADS * HEAD_DIM
KV_COLS = N_KV * HEAD_DIM
QKV_COLS = 2 * (Q_COLS + 2 * KV_COLS)
WINDOW = 128
BLOCK = 128
GRID_W = 64
ROPE_THETA = 10000.0
N_BUCKETS = 32
MAX_DISTANCE = 128
N_GROUPS = 4
EXPERTS_PER_GROUP = 8
N_EXPERTS = N_GROUPS * EXPERTS_PER_GROUP
TOP_K = 2
D_EXPERT = 512
LN_EPS = 1e-5
RMS_EPS = 1e-6
DEPTH = 1
ALPHA = (2 * DEPTH) ** 0.25
SCALE = HEAD_DIM ** -0.5
NEG_BIG = -1e30

LANES = 128
VMEM_LIMIT = 56 * 1024 * 1024

QKV_TM = 512
ATT_A_TQ = 512
ATT_B_TQ = 256
ATT_B_TK = 1024
MERGE_TM = 512
MERGE_TC = 256
ROUTE_TN = 1024
MOE_RB = 256
FINAL_TM = 512
ROUTER_ROWS = 128
FINE_ROW0 = 8


def _layer_norm_rows(x, g, b):
    mu = jnp.mean(x, axis=-1, keepdims=True)
    xc = x - mu
    var = jnp.mean(xc * xc, axis=-1, keepdims=True)
    return xc * lax.rsqrt(var + LN_EPS) * g + b


def _dot(a, b):
    return jnp.dot(a, b, preferred_element_type=F32)


def _dot_nt(a, b):
    return lax.dot_general(a, b, (((1,), (1,)), ((), ())), preferred_element_type=F32)


def _qkv_kernel(x_ref, g_ref, b_ref, w_ref, cos_ref, sin_ref, qg_ref, kg_ref,
                xn_ref, qa_ref, ka_ref, va_ref, qb_ref, kb_ref, vb_ref):
    xn = _layer_norm_rows(x_ref[...], g_ref[...], b_ref[...])
    xn_ref[...] = xn
    xb = xn.astype(BF16)
    cos = cos_ref[...]
    sin = sin_ref[...]

    def proj(c0, width):
        return _dot(xb, w_ref[:, c0:c0 + width])

    def norm_rope(h, gain):
        y = h * lax.rsqrt(jnp.mean(h * h, axis=-1, keepdims=True) + RMS_EPS) * gain
        return y * cos + pltpu.roll(y, HEAD_DIM // 2, 1) * sin

    c = 0
    qa_ref[...] = (proj(c, Q_COLS) * SCALE).astype(BF16)
    c += Q_COLS
    ka_ref[...] = proj(c, KV_COLS).astype(BF16)
    c += KV_COLS
    va_ref[...] = proj(c, KV_COLS).astype(BF16)
    c += KV_COLS
    qg = qg_ref[...]
    for h in range(N_HEADS):
        hq = proj(c + h * HEAD_DIM, HEAD_DIM)
        qb_ref[:, h * HEAD_DIM:(h + 1) * HEAD_DIM] = (norm_rope(hq, qg) * SCALE).astype(BF16)
    c += Q_COLS
    kg = kg_ref[...]
    for h in range(N_KV):
        hk = proj(c + h * HEAD_DIM, HEAD_DIM)
        kb_ref[:, h * HEAD_DIM:(h + 1) * HEAD_DIM] = norm_rope(hk, kg).astype(BF16)
    c += KV_COLS
    vb_ref[...] = proj(c, KV_COLS).astype(BF16)


def _qkv_call(x2, g, b, w_qkv, cos_t, sin_t, qg, kg, seq_len):
    n = x2.shape[0]
    tm = QKV_TM
    pos_blocks = seq_len // tm
    row = lambda i: (i, 0)
    fixed = lambda i: (0, 0)
    pos = lambda i: (i % pos_blocks, 0)
    out_shapes = (
        jax.ShapeDtypeStruct((n, D_MODEL), F32),
        jax.ShapeDtypeStruct((n, Q_COLS), BF16),
        jax.ShapeDtypeStruct((n, KV_COLS), BF16),
        jax.ShapeDtypeStruct((n, KV_COLS), BF16),
        jax.ShapeDtypeStruct((n, Q_COLS), BF16),
        jax.ShapeDtypeStruct((n, KV_COLS), BF16),
        jax.ShapeDtypeStruct((n, KV_COLS), BF16),
    )
    return pl.pallas_call(
        _qkv_kernel,
        grid=(n // tm,),
        in_specs=[
            pl.BlockSpec((tm, D_MODEL), row),
            pl.BlockSpec((1, D_MODEL), fixed),
            pl.BlockSpec((1, D_MODEL), fixed),
            pl.BlockSpec((D_MODEL, QKV_COLS), fixed, pipeline_mode=pl.Buffered(1)),
            pl.BlockSpec((tm, HEAD_DIM), pos),
            pl.BlockSpec((tm, HEAD_DIM), pos),
            pl.BlockSpec((1, HEAD_DIM), fixed),
            pl.BlockSpec((1, HEAD_DIM), fixed),
        ],
        out_specs=(
            pl.BlockSpec((tm, D_MODEL), row),
            pl.BlockSpec((tm, Q_COLS), row),
            pl.BlockSpec((tm, KV_COLS), row),
            pl.BlockSpec((tm, KV_COLS), row),
            pl.BlockSpec((tm, Q_COLS), row),
            pl.BlockSpec((tm, KV_COLS), row),
            pl.BlockSpec((tm, KV_COLS), row),
        ),
        out_shape=out_shapes,
        compiler_params=pltpu.CompilerParams(
            dimension_semantics=("arbitrary",), vmem_limit_bytes=VMEM_LIMIT),
        name="ln_qkv",
    )(x2, g, b, w_qkv, cos_t, sin_t, qg, kg)


def _attn_a_kernel(sink_ref, q_ref, kc_ref, kp_ref, kn_ref, vc_ref, vp_ref, vn_ref, bias_ref,
                   o_ref, kcat_ref, vcat_ref, *, seq_len):
    tq = ATT_A_TQ
    i = pl.program_id(1)
    kcat_ref[0:BLOCK, :] = kp_ref[...]
    kcat_ref[BLOCK:BLOCK + tq, :] = kc_ref[...]
    kcat_ref[BLOCK + tq:, :] = kn_ref[...]
    vcat_ref[0:BLOCK, :] = vp_ref[...]
    vcat_ref[BLOCK:BLOCK + tq, :] = vc_ref[...]
    vcat_ref[BLOCK + tq:, :] = vn_ref[...]
    col = lax.broadcasted_iota(jnp.int32, (1, 3 * BLOCK), 1)

    def body(j, carry):
        r0 = pl.multiple_of(j * BLOCK, BLOCK)
        kw = kcat_ref[pl.ds(r0, 3 * BLOCK), :]
        vw = vcat_ref[pl.ds(r0, 3 * BLOCK), :]
        qj = q_ref[pl.ds(r0, BLOCK), :]
        key_pos = i * tq + j * BLOCK - BLOCK + col
        inside = (key_pos >= 0) & (key_pos < seq_len)
        for h in range(N_HEADS):
            g = h // GROUP
            s = _dot_nt(qj[:, h * HEAD_DIM:(h + 1) * HEAD_DIM], kw[:, g * HEAD_DIM:(g + 1) * HEAD_DIM])
            s = jnp.where(inside, s + bias_ref[h], NEG_BIG)
            sk = sink_ref[h]
            m = jnp.maximum(jnp.max(s, axis=-1, keepdims=True), sk)
            p = jnp.exp(s - m)
            denom = jnp.sum(p, axis=-1, keepdims=True) + jnp.exp(sk - m)
            pn = (p / denom).astype(BF16)
            o = _dot(pn, vw[:, g * HEAD_DIM:(g + 1) * HEAD_DIM])
            o_ref[pl.ds(r0, BLOCK), h * HEAD_DIM:(h + 1) * HEAD_DIM] = o.astype(BF16)
        return carry

    lax.fori_loop(0, tq // BLOCK, body, 0)


def _attn_a_call(sink, qa, ka, va, bias, bsz, seq_len):
    tq = ATT_A_TQ
    nb = seq_len // BLOCK
    bpt = tq // BLOCK
    q3 = qa.reshape(bsz, seq_len, Q_COLS)
    k3 = ka.reshape(bsz, seq_len, KV_COLS)
    v3 = va.reshape(bsz, seq_len, KV_COLS)
    cur = lambda b, i: (b, i, 0)
    prev = lambda b, i: (b, jnp.maximum(i * bpt - 1, 0), 0)
    nxt = lambda b, i: (b, jnp.minimum((i + 1) * bpt, nb - 1), 0)
    kv_cur = pl.BlockSpec((None, tq, KV_COLS), cur)
    kv_prev = pl.BlockSpec((None, BLOCK, KV_COLS), prev)
    kv_next = pl.BlockSpec((None, BLOCK, KV_COLS), nxt)
    out = pl.pallas_call(
        functools.partial(_attn_a_kernel, seq_len=seq_len),
        grid=(bsz, seq_len // tq),
        in_specs=[
            pl.BlockSpec(memory_space=pltpu.SMEM),
            pl.BlockSpec((None, tq, Q_COLS), cur),
            kv_cur, kv_prev, kv_next,
            kv_cur, kv_prev, kv_next,
            pl.BlockSpec((N_HEADS, BLOCK, 3 * BLOCK), lambda b, i: (0, 0, 0)),
        ],
        out_specs=pl.BlockSpec((None, tq, Q_COLS), cur),
        out_shape=jax.ShapeDtypeStruct((bsz, seq_len, Q_COLS), BF16),
        scratch_shapes=[
            pltpu.VMEM((tq + 2 * BLOCK, KV_COLS), BF16),
            pltpu.VMEM((tq + 2 * BLOCK, KV_COLS), BF16),
        ],
        compiler_params=pltpu.CompilerParams(
            dimension_semantics=("arbitrary", "arbitrary"), vmem_limit_bytes=VMEM_LIMIT),
        name="attn_window",
    )(sink, q3, k3, k3, k3, v3, v3, v3, bias)
    return out.reshape(bsz * seq_len, Q_COLS)


def _attn_b_kernel(q_ref, k_ref, v_ref, o_ref, qs_ref, m_ref, l_ref, acc_ref,
                   s0_ref, s1_ref, p0_ref, p1_ref, a0_ref, a1_ref, *, seq_len):
    tq = ATT_B_TQ
    tk = ATT_B_TK
    n_chunks = seq_len // tk
    for h in range(GROUP):
        qs_ref[h * tq:(h + 1) * tq, :] = q_ref[:, h * HEAD_DIM:(h + 1) * HEAD_DIM]
    m_ref[...] = jnp.full(m_ref.shape, NEG_BIG, F32)
    l_ref[...] = jnp.zeros(l_ref.shape, F32)
    acc_ref[...] = jnp.zeros(acc_ref.shape, F32)

    def scores(c, s_ref):
        k0 = pl.multiple_of(c * tk, tk)
        s_ref[...] = _dot_nt(qs_ref[...], k_ref[pl.ds(k0, tk), :])

    def softmax(s_ref, p_ref, a_ref):
        s = s_ref[...]
        m_prev = m_ref[...]
        m_new = jnp.maximum(m_prev, jnp.max(s, axis=-1, keepdims=True))
        alpha = jnp.exp(m_prev - m_new)
        p = jnp.exp(s - m_new)
        l_ref[...] = alpha * l_ref[...] + jnp.sum(p, axis=-1, keepdims=True)
        m_ref[...] = m_new
        a_ref[...] = alpha
        p_ref[...] = p.astype(BF16)

    def weighted_values(c, p_ref, a_ref):
        k0 = pl.multiple_of(c * tk, tk)
        acc_ref[...] = a_ref[...] * acc_ref[...] + _dot(p_ref[...], v_ref[pl.ds(k0, tk), :])

    scores(0, s0_ref)
    scores(1, s1_ref)
    softmax(s0_ref, p0_ref, a0_ref)

    def body(t, carry):
        c = 2 * t + 1
        scores(c + 1, s0_ref)
        softmax(s1_ref, p1_ref, a1_ref)
        weighted_values(c - 1, p0_ref, a0_ref)
        scores(c + 2, s1_ref)
        softmax(s0_ref, p0_ref, a0_ref)
        weighted_values(c, p1_ref, a1_ref)
        return carry

    lax.fori_loop(0, (n_chunks - 2) // 2, body, 0)
    softmax(s1_ref, p1_ref, a1_ref)
    weighted_values(n_chunks - 2, p0_ref, a0_ref)
    weighted_values(n_chunks - 1, p1_ref, a1_ref)
    o = acc_ref[...] / l_ref[...]
    for h in range(GROUP):
        o_ref[:, h * HEAD_DIM:(h + 1) * HEAD_DIM] = o[h * tq:(h + 1) * tq, :].astype(BF16)


def _attn_b_call(qb, kb, vb, bsz, seq_len):
    tq = ATT_B_TQ
    tk = ATT_B_TK
    assert seq_len % tk == 0 and (seq_len // tk) % 2 == 0, "dense mixer pipeline needs an even number of key chunks"
    q3 = qb.reshape(bsz, seq_len, Q_COLS)
    k3 = kb.reshape(bsz, seq_len, KV_COLS)
    v3 = vb.reshape(bsz, seq_len, KV_COLS)
    gcols = GROUP * HEAD_DIM
    out = pl.pallas_call(
        functools.partial(_attn_b_kernel, seq_len=seq_len),
        grid=(bsz, N_KV, seq_len // tq),
        in_specs=[
            pl.BlockSpec((None, tq, gcols), lambda b, g, i: (b, i, g)),
            pl.BlockSpec((None, seq_len, HEAD_DIM), lambda b, g, i: (b, 0, g)),
            pl.BlockSpec((None, seq_len, HEAD_DIM), lambda b, g, i: (b, 0, g)),
        ],
        out_specs=pl.BlockSpec((None, tq, gcols), lambda b, g, i: (b, i, g)),
        out_shape=jax.ShapeDtypeStruct((bsz, seq_len, Q_COLS), BF16),
        scratch_shapes=[
            pltpu.VMEM((GROUP * tq, HEAD_DIM), BF16),
            pltpu.VMEM((GROUP * tq, 1), F32),
            pltpu.VMEM((GROUP * tq, 1), F32),
            pltpu.VMEM((GROUP * tq, HEAD_DIM), F32),
            pltpu.VMEM((GROUP * tq, tk), F32),
            pltpu.VMEM((GROUP * tq, tk), F32),
            pltpu.VMEM((GROUP * tq, tk), BF16),
            pltpu.VMEM((GROUP * tq, tk), BF16),
            pltpu.VMEM((GROUP * tq, 1), F32),
            pltpu.VMEM((GROUP * tq, 1), F32),
        ],
        compiler_params=pltpu.CompilerParams(
            dimension_semantics=("arbitrary", "arbitrary", "arbitrary"), vmem_limit_bytes=VMEM_LIMIT),
        name="attn_dense",
    )(q3, k3, v3)
    return out.reshape(bsz * seq_len, Q_COLS)


def _merge_kernel(xn_ref, oa_ref, ob_ref, wa_ref, wb_ref, wga_ref, wgb_ref, bga_ref, bgb_ref,
                  wo_ref, g_ref, b_ref, wrh_ref, wrl_ref, br_ref,
                  x1rows_ref, lt_ref, xb_ref, acc_ref):
    c = pl.program_id(1)

    @pl.when(c == 0)
    def _():
        xb_ref[...] = xn_ref[...].astype(BF16)
        acc_ref[...] = jnp.zeros(acc_ref.shape, F32)

    xb = xb_ref[...]
    ya = _dot(oa_ref[...], wa_ref[...])
    yb = _dot(ob_ref[...], wb_ref[...])
    ga = jax.nn.sigmoid(_dot(xb, wga_ref[...]) + bga_ref[...])
    gb = jax.nn.sigmoid(_dot(xb, wgb_ref[...]) + bgb_ref[...])
    mixed = (ga * ya + gb * yb).astype(BF16)
    acc_ref[...] += _dot(mixed, wo_ref[...])

    @pl.when(c == pl.num_programs(1) - 1)
    def _():
        x1 = _layer_norm_rows(ALPHA * xn_ref[...] + acc_ref[...], g_ref[...], b_ref[...])
        x1rows_ref[...] = x1.reshape(x1.shape[0], 1, D_MODEL)
        xh = x1.astype(BF16)
        xl = (x1 - xh.astype(F32)).astype(BF16)
        wrh = wrh_ref[...]
        lt = _dot_nt(wrh, xh) + (_dot_nt(wrh, xl) + _dot_nt(wrl_ref[...], xh))
        lt_ref[...] = lt + br_ref[...]


def _merge_call(xn, oa, ob, wa, wb, wg, bg, wo, g, b, wr_hi, wr_lo, br):
    n = xn.shape[0]
    tm, tc = MERGE_TM, MERGE_TC
    nc = D_MODEL // tc
    row = lambda i, c: (i, 0)
    fixed = lambda i, c: (0, 0)
    colc = lambda i, c: (0, c)
    colc_b = lambda i, c: (0, c + nc)
    return pl.pallas_call(
        _merge_kernel,
        grid=(n // tm, nc),
        in_specs=[
            pl.BlockSpec((tm, D_MODEL), row),
            pl.BlockSpec((tm, Q_COLS), row),
            pl.BlockSpec((tm, Q_COLS), row),
            pl.BlockSpec((Q_COLS, tc), colc),
            pl.BlockSpec((Q_COLS, tc), colc),
            pl.BlockSpec((D_MODEL, tc), colc),
            pl.BlockSpec((D_MODEL, tc), colc_b),
            pl.BlockSpec((1, tc), colc),
            pl.BlockSpec((1, tc), colc_b),
            pl.BlockSpec((tc, D_MODEL), lambda i, c: (c, 0)),
            pl.BlockSpec((1, D_MODEL), fixed),
            pl.BlockSpec((1, D_MODEL), fixed),
            pl.BlockSpec((ROUTER_ROWS, D_MODEL), fixed),
            pl.BlockSpec((ROUTER_ROWS, D_MODEL), fixed),
            pl.BlockSpec((ROUTER_ROWS, 1), fixed),
        ],
        out_specs=(
            pl.BlockSpec((tm, 1, D_MODEL), lambda i, c: (i, 0, 0)),
            pl.BlockSpec((ROUTER_ROWS, tm), lambda i, c: (0, i)),
        ),
        out_shape=(
            jax.ShapeDtypeStruct((n, 1, D_MODEL), F32),
            jax.ShapeDtypeStruct((ROUTER_ROWS, n), F32),
        ),
        scratch_shapes=[
            pltpu.VMEM((tm, D_MODEL), BF16),
            pltpu.VMEM((tm, D_MODEL), F32),
        ],
        compiler_params=pltpu.CompilerParams(
            dimension_semantics=("arbitrary", "arbitrary"), vmem_limit_bytes=VMEM_LIMIT),
        name="merge_ln_router",
    )(xn, oa, ob, wa, wb, wg, wg, bg, bg, wo, g, b, wr_hi, wr_lo, br)


def _route_kernel(lt_ref, e_ref, w_ref, cnt_ref, carry_ref):
    tn = lt_ref.shape[1]

    @pl.when(pl.program_id(0) == 0)
    def _():
        carry_ref[...] = jnp.zeros(carry_ref.shape, F32)

    cl = lt_ref[0:N_GROUPS, :]
    gi = lax.broadcasted_iota(jnp.int32, (N_GROUPS, tn), 0)
    cmax = jnp.max(cl, axis=0, keepdims=True)
    ce = jnp.exp(cl - cmax)
    cp = ce / jnp.sum(ce, axis=0, keepdims=True)
    g_idx = jnp.min(jnp.where(cl == cmax, gi, N_GROUPS), axis=0, keepdims=True)
    g_prob = jnp.sum(jnp.where(gi == g_idx, cp, 0.0), axis=0, keepdims=True)

    fl = jnp.zeros((EXPERTS_PER_GROUP, tn), F32)
    for g in range(N_GROUPS):
        r0 = FINE_ROW0 + g * EXPERTS_PER_GROUP
        fl = jnp.where(g_idx == g, lt_ref[r0:r0 + EXPERTS_PER_GROUP, :], fl)
    fmax = jnp.max(fl, axis=0, keepdims=True)
    fe = jnp.exp(fl - fmax)
    fp = fe / jnp.sum(fe, axis=0, keepdims=True)
    ei = lax.broadcasted_iota(jnp.int32, (EXPERTS_PER_GROUP, tn), 0)
    p1 = jnp.max(fp, axis=0, keepdims=True)
    i1 = jnp.min(jnp.where(fp == p1, ei, EXPERTS_PER_GROUP), axis=0, keepdims=True)
    fp2 = jnp.where(ei == i1, -1.0, fp)
    p2 = jnp.max(fp2, axis=0, keepdims=True)
    i2 = jnp.min(jnp.where(fp2 == p2, ei, EXPERTS_PER_GROUP), axis=0, keepdims=True)
    tot = p1 + p2
    w1 = g_prob * p1 / tot
    w2 = g_prob * p2 / tot
    e1 = g_idx * EXPERTS_PER_GROUP + i1
    e2 = g_idx * EXPERTS_PER_GROUP + i2
    ri = lax.broadcasted_iota(jnp.int32, (8, tn), 0)
    e_ref[...] = jnp.where(ri == 0, e1, jnp.where(ri == 1, e2, 0))
    w_ref[...] = jnp.where(ri == 0, w1, jnp.where(ri == 1, w2, 0.0))

    xi = lax.broadcasted_iota(jnp.int32, (N_EXPERTS, tn), 0)
    hits = jnp.where(xi == e1, 1.0, 0.0) + jnp.where(xi == e2, 1.0, 0.0)
    total = carry_ref[...] + jnp.sum(hits, axis=1, keepdims=True)
    carry_ref[...] = total
    cnt_ref[...] = jnp.broadcast_to(total, cnt_ref.shape)


def _route_call(lt):
    n = lt.shape[1]
    tn = min(ROUTE_TN, n)
    tok = lambda i: (0, i)
    return pl.pallas_call(
        _route_kernel,
        grid=(n // tn,),
        in_specs=[pl.BlockSpec((ROUTER_ROWS, tn), tok)],
        out_specs=(pl.BlockSpec((8, tn), tok), pl.BlockSpec((8, tn), tok),
                   pl.BlockSpec((N_EXPERTS, LANES), lambda i: (0, 0))),
        out_shape=(jax.ShapeDtypeStruct((8, n), jnp.int32), jax.ShapeDtypeStruct((8, n), F32),
                   jax.ShapeDtypeStruct((N_EXPERTS, LANES), F32)),
        scratch_shapes=[pltpu.VMEM((N_EXPERTS, 1), F32)],
        compiler_params=pltpu.CompilerParams(dimension_semantics=("arbitrary",)),
        name="router",
    )(lt)


def _plan_kernel(starts_ref, vb_ref, ve_ref, vlo_ref, vhi_ref, *, rb):
    n_visits = vb_ref.shape[0]
    shift = rb.bit_length() - 1

    def expert_body(e, v):
        end = starts_ref[e + 1]

        def cond(state):
            return state[0] < end

        def body(state):
            r, v = state
            b = lax.shift_right_logical(r, shift)
            hi = jnp.minimum(end, (b + 1) * rb)
            vb_ref[v] = b
            ve_ref[v] = e
            vlo_ref[v] = r - b * rb
            vhi_ref[v] = hi - b * rb
            return hi, v + 1

        return lax.while_loop(cond, body, (starts_ref[e], v))[1]

    used = lax.fori_loop(0, N_EXPERTS, expert_body, 0)
    last_b = vb_ref[used - 1]
    last_e = ve_ref[used - 1]

    def pad(i, carry):
        vb_ref[i] = last_b
        ve_ref[i] = last_e
        vlo_ref[i] = 0
        vhi_ref[i] = 0
        return carry

    lax.fori_loop(used, n_visits, pad, 0)


def _plan_call(starts, n_assign, rb):
    assert rb & (rb - 1) == 0 and n_assign % rb == 0
    n_visits = n_assign // rb + N_EXPERTS - 1
    smem = pl.BlockSpec(memory_space=pltpu.SMEM)
    sds = jax.ShapeDtypeStruct((n_visits,), jnp.int32)
    return pl.pallas_call(
        functools.partial(_plan_kernel, rb=rb),
        in_specs=[smem], out_specs=(smem,) * 4, out_shape=(sds,) * 4, name="visit_plan",
    )(starts)


def _expert_kernel(vb_ref, ve_ref, vlo_ref, vhi_ref,
                   tok0_ref, tokn_ref, slot_ref, x_hbm, wg_ref, wu_ref, wd_ref, y_hbm,
                   xin_ref, xf_ref, xb_ref, ys_ref, yout_ref, pending_ref, gsem, ssem, *, n_blocks):
    rb = xb_ref.shape[0]
    v = pl.program_id(0)
    b = vb_ref[v]
    lo = vlo_ref[v]
    hi = vhi_ref[v]

    def gather_row(idx_ref, r):
        return pltpu.make_async_copy(x_hbm.at[pl.ds(idx_ref[0, 0, r], 1)], xin_ref.at[pl.ds(r, 1)], gsem)

    def scatter_row(r):
        return pltpu.make_async_copy(yout_ref.at[pl.ds(r, 1)], y_hbm.at[pl.ds(slot_ref[0, 0, r], 1)], ssem)

    def start_gather(idx_ref):
        def body(r, carry):
            gather_row(idx_ref, r).start()
            return carry
        lax.fori_loop(0, rb, body, 0, unroll=8)

    def wait_gather():
        def body(r, carry):
            gather_row(tok0_ref, r).wait()
            return carry
        lax.fori_loop(0, rb, body, 0, unroll=8)

    def wait_scatter():
        def body(r, carry):
            scatter_row(r).wait()
            return carry
        lax.fori_loop(0, rb, body, 0, unroll=8)

    def send_block(get_rows):
        @pl.when(pending_ref[0] == 1)
        def _():
            wait_scatter()

        yout_ref[...] = get_rows().reshape(rb, 1, D_MODEL)

        def body(r, carry):
            scatter_row(r).start()
            return carry
        lax.fori_loop(0, rb, body, 0, unroll=8)
        pending_ref[0] = 1

    @pl.when(v == 0)
    def _():
        pending_ref[0] = 0
        start_gather(tok0_ref)

    @pl.when(hi > lo)
    def _():
        @pl.when(lo == 0)
        def _():
            wait_gather()
            xf_ref[...] = xin_ref[...].reshape(rb, D_MODEL)
            xb_ref[...] = xf_ref[...].astype(BF16)

            @pl.when(b + 1 < n_blocks)
            def _():
                start_gather(tokn_ref)

        x = xb_ref[...]
        hidden = jax.nn.silu(_dot(x, wg_ref[...])) * _dot(x, wu_ref[...])
        y = _dot(hidden.astype(BF16), wd_ref[...])
        whole = (lo == 0) & (hi == rb)

        @pl.when(whole)
        def _():
            send_block(lambda: y)

        @pl.when(jnp.logical_not(whole))
        def _():
            rows = lax.broadcasted_iota(jnp.int32, (rb, 1), 0)
            mine = (rows >= lo) & (rows < hi)

            @pl.when(lo == 0)
            def _():
                ys_ref[...] = jnp.where(mine, y, 0.0)

            @pl.when(lo > 0)
            def _():
                ys_ref[...] = jnp.where(mine, y, ys_ref[...])

            @pl.when(hi == rb)
            def _():
                send_block(lambda: ys_ref[...])

    @pl.when(v == pl.num_programs(0) - 1)
    def _():
        @pl.when(pending_ref[0] == 1)
        def _():
            wait_scatter()
            pending_ref[0] = 0


def _expert_call(plan, tok_sorted, slot_sorted, x1rows, w_gate, w_up, w_down, rb):
    n_assign = tok_sorted.shape[0]
    n_blocks = n_assign // rb
    n_visits = plan[0].shape[0]
    tok3 = tok_sorted.reshape(n_blocks, 1, rb)
    slot3 = slot_sorted.reshape(n_blocks, 1, rb)
    w_map = lambda v, vb, ve, vlo, vhi: (ve[v], 0, 0)
    idx_block = (1, 1, rb)
    grid_spec = pltpu.PrefetchScalarGridSpec(
        num_scalar_prefetch=4,
        grid=(n_visits,),
        in_specs=[
            pl.BlockSpec(idx_block, lambda v, vb, ve, vlo, vhi: (0, 0, 0), memory_space=pltpu.SMEM),
            pl.BlockSpec(idx_block, lambda v, vb, ve, vlo, vhi: (jnp.minimum(vb[v] + 1, n_blocks - 1), 0, 0),
                         memory_space=pltpu.SMEM),
            pl.BlockSpec(idx_block, lambda v, vb, ve, vlo, vhi: (vb[v], 0, 0), memory_space=pltpu.SMEM),
            pl.BlockSpec(memory_space=pl.ANY),
            pl.BlockSpec((None, D_MODEL, D_EXPERT), w_map),
            pl.BlockSpec((None, D_MODEL, D_EXPERT), w_map),
            pl.BlockSpec((None, D_EXPERT, D_MODEL), w_map),
        ],
        out_specs=pl.BlockSpec(memory_space=pl.ANY),
        scratch_shapes=[
            pltpu.VMEM((rb, 1, D_MODEL), F32),
            pltpu.VMEM((rb, D_MODEL), F32),
            pltpu.VMEM((rb, D_MODEL), BF16),
            pltpu.VMEM((rb, D_MODEL), F32),
            pltpu.VMEM((rb, 1, D_MODEL), F32),
            pltpu.SMEM((1,), jnp.int32),
            pltpu.SemaphoreType.DMA(()),
            pltpu.SemaphoreType.DMA(()),
        ],
    )
    return pl.pallas_call(
        functools.partial(_expert_kernel, n_blocks=n_blocks),
        grid_spec=grid_spec,
        out_shape=jax.ShapeDtypeStruct((n_assign, 1, D_MODEL), F32),
        compiler_params=pltpu.CompilerParams(
            dimension_semantics=("arbitrary",), vmem_limit_bytes=VMEM_LIMIT),
        name="experts",
    )(*plan, tok3, tok3, slot3, x1rows, w_gate, w_up, w_down)


def _final_kernel(x_ref, y0_ref, y1_ref, w_ref, g_ref, b_ref, o_ref, xs_ref, ya_ref, yb_ref):
    tm = o_ref.shape[0]
    xs_ref[...] = x_ref[...].reshape(tm, D_MODEL)
    ya_ref[...] = y0_ref[...].reshape(tm, D_MODEL)
    yb_ref[...] = y1_ref[...].reshape(tm, D_MODEL)
    w = w_ref[...]
    moe = ya_ref[...] * w[:, 0:1] + yb_ref[...] * w[:, 1:2]
    o_ref[...] = _layer_norm_rows(ALPHA * xs_ref[...] + moe, g_ref[...], b_ref[...])


def _final_call(x1rows, y2, w_tok, g, b):
    n = x1rows.shape[0]
    tm = FINAL_TM
    fixed = lambda i: (0, 0)
    return pl.pallas_call(
        _final_kernel,
        grid=(n // tm,),
        in_specs=[
            pl.BlockSpec((tm, 1, D_MODEL), lambda i: (i, 0, 0)),
            pl.BlockSpec((None, tm, 1, D_MODEL), lambda i: (0, i, 0, 0)),
            pl.BlockSpec((None, tm, 1, D_MODEL), lambda i: (1, i, 0, 0)),
            pl.BlockSpec((tm, TOP_K), lambda i: (i, 0)),
            pl.BlockSpec((1, D_MODEL), fixed),
            pl.BlockSpec((1, D_MODEL), fixed),
        ],
        out_specs=pl.BlockSpec((tm, D_MODEL), lambda i: (i, 0)),
        out_shape=jax.ShapeDtypeStruct((n, D_MODEL), F32),
        scratch_shapes=[pltpu.VMEM((tm, D_MODEL), F32)] * 3,
        compiler_params=pltpu.CompilerParams(
            dimension_semantics=("arbitrary",), vmem_limit_bytes=VMEM_LIMIT),
        name="combine_ln",
    )(x1rows, y2, y2, w_tok, g, b)


def _t5_bucket(rel):
    half = N_BUCKETS // 2
    exact = half // 2
    n = jnp.abs(rel)
    nf = jnp.maximum(n, 1).astype(F32)
    large = exact + (jnp.log(nf / exact) / math.log(MAX_DISTANCE / exact) * (half - exact)).astype(jnp.int32)
    large = jnp.minimum(large, half - 1)
    return jnp.where(rel > 0, half, 0) + jnp.where(n < exact, n, large)


def _window_bias(rel_bias):
    rel = jnp.arange(3 * BLOCK)[None, :] - BLOCK - jnp.arange(BLOCK)[:, None]
    onehot = (_t5_bucket(rel)[..., None] == jnp.arange(N_BUCKETS)).astype(F32)
    bias = jnp.einsum("qkb,bh->hqk", onehot, rel_bias.astype(F32), precision=lax.Precision.HIGHEST)
    return jnp.where((jnp.abs(rel) <= WINDOW)[None], bias, NEG_BIG)


def _rope_tables(seq_len):
    rows = seq_len // GRID_W
    row_ids = jnp.repeat(jnp.arange(rows), GRID_W).astype(F32)
    col_ids = jnp.tile(jnp.arange(GRID_W), rows).astype(F32)
    half = HEAD_DIM // 2
    inv = 1.0 / (ROPE_THETA ** (jnp.arange(0, half, 2, dtype=F32) / half))
    ang = jnp.concatenate([row_ids[:, None] * inv, col_ids[:, None] * inv], -1)
    cos, sin = jnp.cos(ang), jnp.sin(ang)
    return jnp.concatenate([cos, cos], -1), jnp.concatenate([-sin, sin], -1)


def _deinterleave_cols(w, n_heads):
    d = w.shape[0]
    return w.reshape(d, n_heads, HEAD_DIM // 2, 2).transpose(0, 1, 3, 2).reshape(d, n_heads * HEAD_DIM)


def _deinterleave_gain(g):
    return g.reshape(HEAD_DIM // 2, 2).T.reshape(1, HEAD_DIM)


def _expert_row_block(n_assign):
    return MOE_RB if n_assign // N_EXPERTS >= 8 * MOE_RB else MOE_RB // 2


def _trunk(x, p):
    bsz, seq_len, d = x.shape
    n = bsz * seq_len
    n_assign = n * TOP_K
    x2 = x.reshape(n, d)
    cos_t, sin_t = _rope_tables(seq_len)
    xn, qa, ka, va, qb, kb, vb = _qkv_call(x2, p["emb_g"], p["emb_b"], p["w_qkv"], cos_t, sin_t,
                                           p["q_gain"], p["k_gain"], seq_len)
    oa = _attn_a_call(p["sink"], qa, ka, va, p["bias_a"], bsz, seq_len)
    ob = _attn_b_call(qb, kb, vb, bsz, seq_len)
    x1rows, lt = _merge_call(xn, oa, ob, p["w_a"], p["w_b"], p["w_g"], p["b_g"], p["w_o"],
                             p["ln1_g"], p["ln1_b"], p["wr_hi"], p["wr_lo"], p["b_r"])
    e_rows, w_rows, cnt = _route_call(lt)
    e_flat = e_rows[:TOP_K].reshape(n_assign)
    _, slot_sorted = lax.sort((e_flat, jnp.arange(n_assign, dtype=jnp.int32)), num_keys=1)
    tok_sorted = jnp.where(slot_sorted >= n, slot_sorted - n, slot_sorted)
    ends = jnp.cumsum(cnt[:, 0].astype(jnp.int32))
    starts = jnp.concatenate([jnp.zeros((1,), jnp.int32), ends])
    rb = _expert_row_block(n_assign)
    plan = _plan_call(starts, n_assign, rb)
    y2 = _expert_call(plan, tok_sorted, slot_sorted, x1rows, p["w_gate"], p["w_up"], p["w_down"], rb)
    y2 = y2.reshape(TOP_K, n, 1, d)
    out = _final_call(x1rows, y2, w_rows[:TOP_K].T, p["ln2_g"], p["ln2_b"])
    return out.reshape(bsz, seq_len, d)


def kernel(x_prompt, x_sample, emb_ln_g, emb_ln_b, rel_bias, w_in, b_gate, sink_a, q_norm_g, k_norm_g,
           w_branch_a, w_branch_b, w_out, ln1_g, ln1_b, w_coarse, b_coarse, w_fine, b_fine,
           w_gate, w_up, w_down, ln2_g, ln2_b):
    l = 0
    w_in_l = w_in[l]
    qa_end = Q_COLS
    kva_end = qa_end + 2 * KV_COLS
    qb_end = kva_end + Q_COLS
    kb_end = qb_end + KV_COLS
    vb_end = kb_end + KV_COLS
    w_qkv = jnp.concatenate([
        w_in_l[:, :kva_end],
        _deinterleave_cols(w_in_l[:, kva_end:qb_end], N_HEADS),
        _deinterleave_cols(w_in_l[:, qb_end:kb_end], N_KV),
        w_in_l[:, kb_end:vb_end],
    ], axis=1).astype(BF16)
    w_router = jnp.zeros((ROUTER_ROWS, D_MODEL), F32)
    w_router = w_router.at[0:N_GROUPS].set(w_coarse[l].T)
    w_router = w_router.at[FINE_ROW0:FINE_ROW0 + N_EXPERTS].set(w_fine[l].T)
    wr_hi = w_router.astype(BF16)
    wr_lo = (w_router - wr_hi.astype(F32)).astype(BF16)
    b_router = jnp.zeros((ROUTER_ROWS, 1), F32)
    b_router = b_router.at[0:N_GROUPS, 0].set(b_coarse[l].astype(F32))
    b_router = b_router.at[FINE_ROW0:FINE_ROW0 + N_EXPERTS, 0].set(b_fine[l].astype(F32))
    p = {
        "emb_g": emb_ln_g.reshape(1, D_MODEL), "emb_b": emb_ln_b.reshape(1, D_MODEL),
        "w_qkv": w_qkv,
        "q_gain": _deinterleave_gain(q_norm_g[l]), "k_gain": _deinterleave_gain(k_norm_g[l]),
        "sink": sink_a[l].astype(F32), "bias_a": _window_bias(rel_bias),
        "w_a": w_branch_a[l].astype(BF16), "w_b": w_branch_b[l].astype(BF16),
        "w_g": w_in_l[:, vb_end:].astype(BF16), "b_g": b_gate[l].reshape(1, 2 * D_MODEL),
        "w_o": w_out[l].astype(BF16),
        "ln1_g": ln1_g[l].reshape(1, D_MODEL), "ln1_b": ln1_b[l].reshape(1, D_MODEL),
        "wr_hi": wr_hi, "wr_lo": wr_lo, "b_r": b_router,
        "w_gate": w_gate[l].astype(BF16), "w_up": w_up[l].astype(BF16), "w_down": w_down[l].astype(BF16),
        "ln2_g": ln2_g[l].reshape(1, D_MODEL), "ln2_b": ln2_b[l].reshape(1, D_MODEL),
    }
    return _trunk(x_prompt, p), _trunk(x_sample, p)
```

```python
import functools
import math

import numpy as np
import jax
import jax.numpy as jnp
from jax import lax
from jax.experimental import pallas as pl
from jax.experimental.pallas import tpu as pltpu

F32 = jnp.float32
BF16 = jnp.bfloat16

D_MODEL = 2048
HEAD_DIM = 128
N_HEADS = 8
N_KV = 2
GROUP = N_HEADS // N_KV
Q_COLS = N_HEADS * HEAD_DIM
KV_COLS = N_KV * HEAD_DIM
QKV_COLS = 2 * (Q_COLS + 2 * KV_COLS)
WINDOW = 128
BLOCK = 128
GRID_W = 64
ROPE_THETA = 10000.0
N_BUCKETS = 32
MAX_DISTANCE = 128
N_GROUPS = 4
EXPERTS_PER_GROUP = 8
N_EXPERTS = N_GROUPS * EXPERTS_PER_GROUP
TOP_K = 2
D_EXPERT = 512
LN_EPS = 1e-5
RMS_EPS = 1e-6
DEPTH = 1
ALPHA = (2 * DEPTH) ** 0.25
SCALE = HEAD_DIM ** -0.5
LOG2E = math.log2(math.e)
NEG_BIG = -1e30

LANES = 128
VMEM_LIMIT = 56 * 1024 * 1024

QKV_TM = 512
ATT_A_TQ = 512
ATT_B_TQ = 256
ATT_B_TK = 512
MERGE_TM = 512
MERGE_TC = 256
ROUTE_TN = 1024
MOE_RB = 256
FINAL_TM = 512
ROUTER_ROWS = 128
FINE_ROW0 = 8


def _layer_norm_rows(x, g, b):
    mu = jnp.mean(x, axis=-1, keepdims=True)
    xc = x - mu
    var = jnp.mean(xc * xc, axis=-1, keepdims=True)
    return xc * lax.rsqrt(var + LN_EPS) * g + b


def _dot(a, b):
    return jnp.dot(a, b, preferred_element_type=F32)


def _dot_nt(a, b):
    return lax.dot_general(a, b, (((1,), (1,)), ((), ())), preferred_element_type=F32)


def _qkv_kernel(x_ref, g_ref, b_ref, w_ref, cos_ref, sin_ref, qg_ref, kg_ref,
                xn_ref, qa_ref, ka_ref, va_ref, qb_ref, kb_ref, vb_ref):
    xn = _layer_norm_rows(x_ref[...], g_ref[...], b_ref[...])
    xn_ref[...] = xn
    xb = xn.astype(BF16)
    cos = cos_ref[...]
    sin = sin_ref[...]

    def proj(c0, width):
        return _dot(xb, w_ref[:, c0:c0 + width])

    def norm_rope(h, gain):
        y = h * lax.rsqrt(jnp.mean(h * h, axis=-1, keepdims=True) + RMS_EPS) * gain
        return y * cos + pltpu.roll(y, HEAD_DIM // 2, 1) * sin

    c = 0
    qa_ref[...] = (proj(c, Q_COLS) * SCALE).astype(BF16)
    c += Q_COLS
    ka_ref[...] = proj(c, KV_COLS).astype(BF16)
    c += KV_COLS
    va_ref[...] = proj(c, KV_COLS).astype(BF16)
    c += KV_COLS
    qg = qg_ref[...]
    for hp in range(N_HEADS // 2):
        hq2 = proj(c + 2 * hp * HEAD_DIM, 2 * HEAD_DIM)
        for h in (2 * hp, 2 * hp + 1):
            hq = hq2[:, (h - 2 * hp) * HEAD_DIM:(h - 2 * hp + 1) * HEAD_DIM]
            qb_ref[:, h * HEAD_DIM:(h + 1) * HEAD_DIM] = (norm_rope(hq, qg) * (SCALE * LOG2E)).astype(BF16)
    c += Q_COLS
    kg = kg_ref[...]
    hk2 = proj(c, KV_COLS)
    for h in range(N_KV):
        kb_ref[:, h * HEAD_DIM:(h + 1) * HEAD_DIM] = norm_rope(hk2[:, h * HEAD_DIM:(h + 1) * HEAD_DIM], kg).astype(BF16)
    c += KV_COLS
    hv2 = proj(c, KV_COLS)
    ones = jnp.ones((x_ref.shape[0], HEAD_DIM), BF16)
    for h in range(N_KV):
        vb_ref[:, 2 * h * HEAD_DIM:(2 * h + 1) * HEAD_DIM] = hv2[:, h * HEAD_DIM:(h + 1) * HEAD_DIM].astype(BF16)
        vb_ref[:, (2 * h + 1) * HEAD_DIM:(2 * h + 2) * HEAD_DIM] = ones


def _qkv_call(x2, g, b, w_qkv, cos_t, sin_t, qg, kg, seq_len):
    n = x2.shape[0]
    tm = QKV_TM
    pos_blocks = seq_len // tm
    row = lambda i: (i, 0)
    fixed = lambda i: (0, 0)
    pos = lambda i: (i % pos_blocks, 0)
    out_shapes = (
        jax.ShapeDtypeStruct((n, D_MODEL), F32),
        jax.ShapeDtypeStruct((n, Q_COLS), BF16),
        jax.ShapeDtypeStruct((n, KV_COLS), BF16),
        jax.ShapeDtypeStruct((n, KV_COLS), BF16),
        jax.ShapeDtypeStruct((n, Q_COLS), BF16),
        jax.ShapeDtypeStruct((n, KV_COLS), BF16),
        jax.ShapeDtypeStruct((n, 2 * KV_COLS), BF16),
    )
    return pl.pallas_call(
        _qkv_kernel,
        grid=(n // tm,),
        in_specs=[
            pl.BlockSpec((tm, D_MODEL), row),
            pl.BlockSpec((1, D_MODEL), fixed),
            pl.BlockSpec((1, D_MODEL), fixed),
            pl.BlockSpec((D_MODEL, QKV_COLS), fixed, pipeline_mode=pl.Buffered(1)),
            pl.BlockSpec((tm, HEAD_DIM), pos),
            pl.BlockSpec((tm, HEAD_DIM), pos),
            pl.BlockSpec((1, HEAD_DIM), fixed),
            pl.BlockSpec((1, HEAD_DIM), fixed),
        ],
        out_specs=(
            pl.BlockSpec((tm, D_MODEL), row),
            pl.BlockSpec((tm, Q_COLS), row),
            pl.BlockSpec((tm, KV_COLS), row),
            pl.BlockSpec((tm, KV_COLS), row),
            pl.BlockSpec((tm, Q_COLS), row),
            pl.BlockSpec((tm, KV_COLS), row),
            pl.BlockSpec((tm, 2 * KV_COLS), row),
        ),
        out_shape=out_shapes,
        compiler_params=pltpu.CompilerParams(
            dimension_semantics=("arbitrary",), vmem_limit_bytes=VMEM_LIMIT),
        name="ln_qkv",
    )(x2, g, b, w_qkv, cos_t, sin_t, qg, kg)


def _attn_a_kernel(sink_ref, q_ref, kc_ref, kp_ref, kn_ref, vc_ref, vp_ref, vn_ref, bias_ref,
                   o_ref, kcat_ref, vcat_ref, sa_ref, sb_ref, *, seq_len):
    tq = ATT_A_TQ
    i = pl.program_id(1)
    kcat_ref[0:BLOCK, :] = kp_ref[...]
    kcat_ref[BLOCK:BLOCK + tq, :] = kc_ref[...]
    kcat_ref[BLOCK + tq:, :] = kn_ref[...]
    vcat_ref[0:BLOCK, :] = vp_ref[...]
    vcat_ref[BLOCK:BLOCK + tq, :] = vc_ref[...]
    vcat_ref[BLOCK + tq:, :] = vn_ref[...]
    col = lax.broadcasted_iota(jnp.int32, (1, 3 * BLOCK), 1)
    s_refs = (sa_ref, sb_ref)

    def block_scores(j, s_ref):
        qj = q_ref[j * BLOCK:(j + 1) * BLOCK, :]
        for g in range(N_KV):
            qs = jnp.concatenate(
                [qj[:, h * HEAD_DIM:(h + 1) * HEAD_DIM] for h in range(g * GROUP, (g + 1) * GROUP)], axis=0)
            kw = kcat_ref[j * BLOCK:(j + 3) * BLOCK, g * HEAD_DIM:(g + 1) * HEAD_DIM]
            s_ref[g * GROUP * BLOCK:(g + 1) * GROUP * BLOCK, :] = _dot_nt(qs, kw)

    def block_out(j, s_ref):
        key_pos = i * tq + (j - 1) * BLOCK + col
        inside = (key_pos >= 0) & (key_pos < seq_len)
        for g in range(N_KV):
            probs = []
            for h in range(g * GROUP, (g + 1) * GROUP):
                s = s_ref[h * BLOCK:(h + 1) * BLOCK, :]
                s = jnp.where(inside, s + bias_ref[h], NEG_BIG)
                sk = sink_ref[h]
                m = jnp.maximum(jnp.max(s, axis=-1, keepdims=True), sk)
                p = jnp.exp(s - m)
                denom = jnp.sum(p, axis=-1, keepdims=True) + jnp.exp(sk - m)
                probs.append((p / denom).astype(BF16))
            vw = vcat_ref[j * BLOCK:(j + 3) * BLOCK, g * HEAD_DIM:(g + 1) * HEAD_DIM]
            o = _dot(jnp.concatenate(probs, axis=0), vw)
            for hh in range(GROUP):
                h = g * GROUP + hh
                o_ref[j * BLOCK:(j + 1) * BLOCK, h * HEAD_DIM:(h + 1) * HEAD_DIM] = (
                    o[hh * BLOCK:(hh + 1) * BLOCK, :].astype(BF16))

    n_blk = tq // BLOCK
    block_scores(0, s_refs[0])
    for j in range(n_blk):
        if j + 1 < n_blk:
            block_scores(j + 1, s_refs[(j + 1) % 2])
        block_out(j, s_refs[j % 2])


def _attn_a_call(sink, qa, ka, va, bias, bsz, seq_len):
    tq = ATT_A_TQ
    nb = seq_len // BLOCK
    bpt = tq // BLOCK
    q3 = qa.reshape(bsz, seq_len, Q_COLS)
    k3 = ka.reshape(bsz, seq_len, KV_COLS)
    v3 = va.reshape(bsz, seq_len, KV_COLS)
    cur = lambda b, i: (b, i, 0)
    prev = lambda b, i: (b, jnp.maximum(i * bpt - 1, 0), 0)
    nxt = lambda b, i: (b, jnp.minimum((i + 1) * bpt, nb - 1), 0)
    kv_cur = pl.BlockSpec((None, tq, KV_COLS), cur)
    kv_prev = pl.BlockSpec((None, BLOCK, KV_COLS), prev)
    kv_next = pl.BlockSpec((None, BLOCK, KV_COLS), nxt)
    out = pl.pallas_call(
        functools.partial(_attn_a_kernel, seq_len=seq_len),
        grid=(bsz, seq_len // tq),
        in_specs=[
            pl.BlockSpec(memory_space=pltpu.SMEM),
            pl.BlockSpec((None, tq, Q_COLS), cur),
            kv_cur, kv_prev, kv_next,
            kv_cur, kv_prev, kv_next,
            pl.BlockSpec((N_HEADS, BLOCK, 3 * BLOCK), lambda b, i: (0, 0, 0)),
        ],
        out_specs=pl.BlockSpec((None, tq, Q_COLS), cur),
        out_shape=jax.ShapeDtypeStruct((bsz, seq_len, Q_COLS), BF16),
        scratch_shapes=[
            pltpu.VMEM((tq + 2 * BLOCK, KV_COLS), BF16),
            pltpu.VMEM((tq + 2 * BLOCK, KV_COLS), BF16),
            pltpu.VMEM((N_HEADS * BLOCK, 3 * BLOCK), F32),
            pltpu.VMEM((N_HEADS * BLOCK, 3 * BLOCK), F32),
        ],
        compiler_params=pltpu.CompilerParams(
            dimension_semantics=("arbitrary", "arbitrary"), vmem_limit_bytes=VMEM_LIMIT),
        name="attn_window",
    )(sink, q3, k3, k3, k3, v3, v3, v3, bias)
    return out.reshape(bsz * seq_len, Q_COLS)


def _attn_b_kernel(q_ref, qn_ref, k_ref, v_ref, o_ref, qs_ref, qsn_ref, m_ref, acc_ref,
                   s0_ref, s1_ref, p0_ref, p1_ref, a0_ref, a1_ref, *, seq_len):
    tq = ATT_B_TQ
    tk = ATT_B_TK
    n_chunks = seq_len // tk
    rows = GROUP * tq
    for h in range(GROUP):
        qs_ref[h * tq:(h + 1) * tq, :] = q_ref[:, h * HEAD_DIM:(h + 1) * HEAD_DIM]
        qsn_ref[h * tq:(h + 1) * tq, :] = qn_ref[:, h * HEAD_DIM:(h + 1) * HEAD_DIM]

    def scores(q_src, c, s_ref):
        k0 = pl.multiple_of(c * tk, tk)
        s_ref[...] = _dot_nt(q_src[...], k_ref[pl.ds(k0, tk), :])

    def softmax(s_ref, p_ref, a_ref, first=False):
        m_tile = s_ref[:, 0:LANES]
        for j in range(1, tk // LANES):
            m_tile = jnp.maximum(m_tile, s_ref[:, j * LANES:(j + 1) * LANES])
        m_cur = jnp.broadcast_to(jnp.max(m_tile, axis=-1, keepdims=True), (rows, LANES))
        if first:
            m_new = m_cur
            a_ref[...] = jnp.zeros((rows, LANES), F32)
        else:
            m_prev = m_ref[...]
            m_new = jnp.maximum(m_prev, m_cur)
            a_ref[...] = jnp.exp2(m_prev - m_new)
        m_ref[...] = m_new
        for j in range(tk // LANES):
            p_ref[:, j * LANES:(j + 1) * LANES] = jnp.exp2(s_ref[:, j * LANES:(j + 1) * LANES] - m_new).astype(BF16)

    def weighted_values(c, p_ref, a_ref):
        k0 = pl.multiple_of(c * tk, tk)
        pv = _dot(p_ref[...], v_ref[pl.ds(k0, tk), :])
        a = a_ref[...]
        acc_ref[:, 0:HEAD_DIM] = a * acc_ref[:, 0:HEAD_DIM] + pv[:, 0:HEAD_DIM]
        acc_ref[:, HEAD_DIM:] = a * acc_ref[:, HEAD_DIM:] + pv[:, HEAD_DIM:]

    @pl.when(pl.program_id(2) == 0)
    def _():
        scores(qs_ref, 0, s0_ref)
        scores(qs_ref, 1, s1_ref)
        softmax(s0_ref, p0_ref, a0_ref, first=True)

    acc_ref[...] = jnp.zeros(acc_ref.shape, F32)

    def body(t, carry):
        c = 2 * t + 1
        scores(qs_ref, c + 1, s0_ref)
        softmax(s1_ref, p1_ref, a1_ref)
        weighted_values(c - 1, p0_ref, a0_ref)
        scores(qs_ref, c + 2, s1_ref)
        softmax(s0_ref, p0_ref, a0_ref)
        weighted_values(c, p1_ref, a1_ref)
        return carry

    lax.fori_loop(0, (n_chunks - 2) // 2, body, 0)
    scores(qsn_ref, 0, s0_ref)
    softmax(s1_ref, p1_ref, a1_ref)
    weighted_values(n_chunks - 2, p0_ref, a0_ref)
    scores(qsn_ref, 1, s1_ref)
    softmax(s0_ref, p0_ref, a0_ref, first=True)
    weighted_values(n_chunks - 1, p1_ref, a1_ref)
    o = acc_ref[:, 0:HEAD_DIM] / acc_ref[:, HEAD_DIM:]
    for h in range(GROUP):
        o_ref[:, h * HEAD_DIM:(h + 1) * HEAD_DIM] = o[h * tq:(h + 1) * tq, :].astype(BF16)


def _attn_b_call(qb, kb, vb, bsz, seq_len):
    tq = ATT_B_TQ
    tk = ATT_B_TK
    assert seq_len % tk == 0 and (seq_len // tk) % 2 == 0, "dense mixer pipeline needs an even number of key chunks"
    q3 = qb.reshape(bsz, seq_len, Q_COLS)
    k3 = kb.reshape(bsz, seq_len, KV_COLS)
    v3 = vb.reshape(bsz, seq_len, 2 * KV_COLS)
    gcols = GROUP * HEAD_DIM
    n_q = seq_len // tq
    rows = GROUP * tq
    out = pl.pallas_call(
        functools.partial(_attn_b_kernel, seq_len=seq_len),
        grid=(bsz, N_KV, n_q),
        in_specs=[
            pl.BlockSpec((None, tq, gcols), lambda b, g, i: (b, i, g)),
            pl.BlockSpec((None, tq, gcols), lambda b, g, i: (b, jnp.minimum(i + 1, n_q - 1), g)),
            pl.BlockSpec((None, seq_len, HEAD_DIM), lambda b, g, i: (b, 0, g)),
            pl.BlockSpec((None, seq_len, 2 * HEAD_DIM), lambda b, g, i: (b, 0, g)),
        ],
        out_specs=pl.BlockSpec((None, tq, gcols), lambda b, g, i: (b, i, g)),
        out_shape=jax.ShapeDtypeStruct((bsz, seq_len, Q_COLS), BF16),
        scratch_shapes=[
            pltpu.VMEM((rows, HEAD_DIM), BF16),
            pltpu.VMEM((rows, HEAD_DIM), BF16),
            pltpu.VMEM((rows, LANES), F32),
            pltpu.VMEM((rows, 2 * HEAD_DIM), F32),
            pltpu.VMEM((rows, tk), F32),
            pltpu.VMEM((rows, tk), F32),
            pltpu.VMEM((rows, tk), BF16),
            pltpu.VMEM((rows, tk), BF16),
            pltpu.VMEM((rows, LANES), F32),
            pltpu.VMEM((rows, LANES), F32),
        ],
        compiler_params=pltpu.CompilerParams(
            dimension_semantics=("arbitrary", "arbitrary", "arbitrary"), vmem_limit_bytes=VMEM_LIMIT),
        name="attn_dense",
    )(q3, q3, k3, v3)
    return out.reshape(bsz * seq_len, Q_COLS)


def _merge_kernel(xn_ref, oa_ref, ob_ref, wa_ref, wb_ref, wga_ref, wgb_ref, bga_ref, bgb_ref,
                  wo_ref, g_ref, b_ref, wrh_ref, wrl_ref, br_ref,
                  x1rows_ref, lt_ref, xb_ref, acc_ref):
    c = pl.program_id(1)

    @pl.when(c == 0)
    def _():
        xb_ref[...] = xn_ref[...].astype(BF16)
        acc_ref[...] = jnp.zeros(acc_ref.shape, F32)

    xb = xb_ref[...]
    ya = _dot(oa_ref[...], wa_ref[...])
    yb = _dot(ob_ref[...], wb_ref[...])
    ga = jax.nn.sigmoid(_dot(xb, wga_ref[...]) + bga_ref[...])
    gb = jax.nn.sigmoid(_dot(xb, wgb_ref[...]) + bgb_ref[...])
    mixed = (ga * ya + gb * yb).astype(BF16)
    acc_ref[...] += _dot(mixed, wo_ref[...])

    @pl.when(c == pl.num_programs(1) - 1)
    def _():
        x1 = _layer_norm_rows(ALPHA * xn_ref[...] + acc_ref[...], g_ref[...], b_ref[...])
        x1rows_ref[...] = x1.reshape(x1.shape[0], 1, D_MODEL)
        xh = x1.astype(BF16)
        xl = (x1 - xh.astype(F32)).astype(BF16)
        wrh = wrh_ref[...]
        lt = _dot_nt(wrh, xh) + (_dot_nt(wrh, xl) + _dot_nt(wrl_ref[...], xh))
        lt_ref[...] = lt + br_ref[...]


def _merge_call(xn, oa, ob, wa, wb, wg, bg, wo, g, b, wr_hi, wr_lo, br):
    n = xn.shape[0]
    tm, tc = MERGE_TM, MERGE_TC
    nc = D_MODEL // tc
    row = lambda i, c: (i, 0)
    fixed = lambda i, c: (0, 0)
    colc = lambda i, c: (0, c)
    colc_b = lambda i, c: (0, c + nc)
    return pl.pallas_call(
        _merge_kernel,
        grid=(n // tm, nc),
        in_specs=[
            pl.BlockSpec((tm, D_MODEL), row),
            pl.BlockSpec((tm, Q_COLS), row),
            pl.BlockSpec((tm, Q_COLS), row),
            pl.BlockSpec((Q_COLS, tc), colc),
            pl.BlockSpec((Q_COLS, tc), colc),
            pl.BlockSpec((D_MODEL, tc), colc),
            pl.BlockSpec((D_MODEL, tc), colc_b),
            pl.BlockSpec((1, tc), colc),
            pl.BlockSpec((1, tc), colc_b),
            pl.BlockSpec((tc, D_MODEL), lambda i, c: (c, 0)),
            pl.BlockSpec((1, D_MODEL), fixed),
            pl.BlockSpec((1, D_MODEL), fixed),
            pl.BlockSpec((ROUTER_ROWS, D_MODEL), fixed),
            pl.BlockSpec((ROUTER_ROWS, D_MODEL), fixed),
            pl.BlockSpec((ROUTER_ROWS, 1), fixed),
        ],
        out_specs=(
            pl.BlockSpec((tm, 1, D_MODEL), lambda i, c: (i, 0, 0)),
            pl.BlockSpec((ROUTER_ROWS, tm), lambda i, c: (0, i)),
        ),
        out_shape=(
            jax.ShapeDtypeStruct((n, 1, D_MODEL), F32),
            jax.ShapeDtypeStruct((ROUTER_ROWS, n), F32),
        ),
        scratch_shapes=[
            pltpu.VMEM((tm, D_MODEL), BF16),
            pltpu.VMEM((tm, D_MODEL), F32),
        ],
        compiler_params=pltpu.CompilerParams(
            dimension_semantics=("arbitrary", "arbitrary"), vmem_limit_bytes=VMEM_LIMIT),
        name="merge_ln_router",
    )(xn, oa, ob, wa, wb, wg, wg, bg, bg, wo, g, b, wr_hi, wr_lo, br)


def _route_kernel(lt_ref, e_ref, w_ref, cnt_ref, carry_ref):
    tn = lt_ref.shape[1]

    @pl.when(pl.program_id(0) == 0)
    def _():
        carry_ref[...] = jnp.zeros(carry_ref.shape, F32)

    cl = lt_ref[0:N_GROUPS, :]
    gi = lax.broadcasted_iota(jnp.int32, (N_GROUPS, tn), 0)
    cmax = jnp.max(cl, axis=0, keepdims=True)
    ce = jnp.exp(cl - cmax)
    cp = ce / jnp.sum(ce, axis=0, keepdims=True)
    g_idx = jnp.min(jnp.where(cl == cmax, gi, N_GROUPS), axis=0, keepdims=True)
    g_prob = jnp.sum(jnp.where(gi == g_idx, cp, 0.0), axis=0, keepdims=True)

    fl = jnp.zeros((EXPERTS_PER_GROUP, tn), F32)
    for g in range(N_GROUPS):
        r0 = FINE_ROW0 + g * EXPERTS_PER_GROUP
        fl = jnp.where(g_idx == g, lt_ref[r0:r0 + EXPERTS_PER_GROUP, :], fl)
    fmax = jnp.max(fl, axis=0, keepdims=True)
    fe = jnp.exp(fl - fmax)
    fp = fe / jnp.sum(fe, axis=0, keepdims=True)
    ei = lax.broadcasted_iota(jnp.int32, (EXPERTS_PER_GROUP, tn), 0)
    p1 = jnp.max(fp, axis=0, keepdims=True)
    i1 = jnp.min(jnp.where(fp == p1, ei, EXPERTS_PER_GROUP), axis=0, keepdims=True)
    fp2 = jnp.where(ei == i1, -1.0, fp)
    p2 = jnp.max(fp2, axis=0, keepdims=True)
    i2 = jnp.min(jnp.where(fp2 == p2, ei, EXPERTS_PER_GROUP), axis=0, keepdims=True)
    tot = p1 + p2
    w1 = g_prob * p1 / tot
    w2 = g_prob * p2 / tot
    e1 = g_idx * EXPERTS_PER_GROUP + i1
    e2 = g_idx * EXPERTS_PER_GROUP + i2
    ri = lax.broadcasted_iota(jnp.int32, (8, tn), 0)
    e_ref[...] = jnp.where(ri == 0, e1, jnp.where(ri == 1, e2, 0))
    w_ref[...] = jnp.where(ri == 0, w1, jnp.where(ri == 1, w2, 0.0))

    xi = lax.broadcasted_iota(jnp.int32, (N_EXPERTS, tn), 0)
    hits = jnp.where(xi == e1, 1.0, 0.0) + jnp.where(xi == e2, 1.0, 0.0)
    total = carry_ref[...] + jnp.sum(hits, axis=1, keepdims=True)
    carry_ref[...] = total
    cnt_ref[...] = jnp.broadcast_to(total, cnt_ref.shape)


def _route_call(lt):
    n = lt.shape[1]
    tn = min(ROUTE_TN, n)
    tok = lambda i: (0, i)
    return pl.pallas_call(
        _route_kernel,
        grid=(n // tn,),
        in_specs=[pl.BlockSpec((ROUTER_ROWS, tn), tok)],
        out_specs=(pl.BlockSpec((8, tn), tok), pl.BlockSpec((8, tn), tok),
                   pl.BlockSpec((N_EXPERTS, LANES), lambda i: (0, 0))),
        out_shape=(jax.ShapeDtypeStruct((8, n), jnp.int32), jax.ShapeDtypeStruct((8, n), F32),
                   jax.ShapeDtypeStruct((N_EXPERTS, LANES), F32)),
        scratch_shapes=[pltpu.VMEM((N_EXPERTS, 1), F32)],
        compiler_params=pltpu.CompilerParams(dimension_semantics=("arbitrary",)),
        name="router",
    )(lt)


def _plan_kernel(starts_ref, vb_ref, ve_ref, vlo_ref, vhi_ref, *, rb):
    n_visits = vb_ref.shape[0]
    shift = rb.bit_length() - 1

    def expert_body(e, v):
        end = starts_ref[e + 1]

        def cond(state):
            return state[0] < end

        def body(state):
            r, v = state
            b = lax.shift_right_logical(r, shift)
            hi = jnp.minimum(end, (b + 1) * rb)
            vb_ref[v] = b
            ve_ref[v] = e
            vlo_ref[v] = r - b * rb
            vhi_ref[v] = hi - b * rb
            return hi, v + 1

        return lax.while_loop(cond, body, (starts_ref[e], v))[1]

    used = lax.fori_loop(0, N_EXPERTS, expert_body, 0)
    last_b = vb_ref[used - 1]
    last_e = ve_ref[used - 1]

    def pad(i, carry):
        vb_ref[i] = last_b
        ve_ref[i] = last_e
        vlo_ref[i] = 0
        vhi_ref[i] = 0
        return carry

    lax.fori_loop(used, n_visits, pad, 0)


def _plan_call(starts, n_assign, rb):
    assert rb & (rb - 1) == 0 and n_assign % rb == 0
    n_visits = n_assign // rb + N_EXPERTS - 1
    smem = pl.BlockSpec(memory_space=pltpu.SMEM)
    sds = jax.ShapeDtypeStruct((n_visits,), jnp.int32)
    return pl.pallas_call(
        functools.partial(_plan_kernel, rb=rb),
        in_specs=[smem], out_specs=(smem,) * 4, out_shape=(sds,) * 4, name="visit_plan",
    )(starts)


def _expert_kernel(vb_ref, ve_ref, vlo_ref, vhi_ref,
                   tok0_ref, tokn_ref, slot_ref, x_hbm, wg_ref, wu_ref, wd_ref, y_hbm,
                   xin_ref, xf_ref, xb_ref, ys_ref, yout_ref, wgb_ref, wub_ref, wdb_ref,
                   pending_ref, gsem, ssem, *, n_blocks):
    rb = xb_ref.shape[0]
    v = pl.program_id(0)
    b = vb_ref[v]
    lo = vlo_ref[v]
    hi = vhi_ref[v]

    def gather_row(idx_ref, r):
        return pltpu.make_async_copy(x_hbm.at[pl.ds(idx_ref[0, 0, r], 1)], xin_ref.at[pl.ds(r, 1)], gsem)

    def scatter_row(r):
        return pltpu.make_async_copy(yout_ref.at[pl.ds(r, 1)], y_hbm.at[pl.ds(slot_ref[0, 0, r], 1)], ssem)

    def start_gather(idx_ref):
        def body(r, carry):
            gather_row(idx_ref, r).start()
            return carry
        lax.fori_loop(0, rb, body, 0, unroll=8)

    def wait_gather():
        def body(r, carry):
            gather_row(tok0_ref, r).wait()
            return carry
        lax.fori_loop(0, rb, body, 0, unroll=8)

    def wait_scatter():
        def body(r, carry):
            scatter_row(r).wait()
            return carry
        lax.fori_loop(0, rb, body, 0, unroll=8)

    def send_block(get_rows):
        @pl.when(pending_ref[0] == 1)
        def _():
            wait_scatter()

        yout_ref[...] = get_rows().reshape(rb, 1, D_MODEL)

        def body(r, carry):
            scatter_row(r).start()
            return carry
        lax.fori_loop(0, rb, body, 0, unroll=8)
        pending_ref[0] = 1

    @pl.when(v == 0)
    def _():
        pending_ref[0] = 0
        start_gather(tok0_ref)

    @pl.when(hi > lo)
    def _():
        @pl.when(lo == 0)
        def _():
            wait_gather()
            xf_ref[...] = xin_ref[...].reshape(rb, D_MODEL)
            xb_ref[...] = xf_ref[...].astype(BF16)

            @pl.when(b + 1 < n_blocks)
            def _():
                start_gather(tokn_ref)

        @pl.when((v == 0) | (ve_ref[v] != ve_ref[jnp.maximum(v - 1, 0)]))
        def _():
            wgb_ref[...] = wg_ref[...].astype(BF16)
            wub_ref[...] = wu_ref[...].astype(BF16)
            wdb_ref[...] = wd_ref[...].astype(BF16)

        x = xb_ref[...]
        hidden = jax.nn.silu(_dot(x, wgb_ref[...])) * _dot(x, wub_ref[...])
        y = _dot(hidden.astype(BF16), wdb_ref[...])
        whole = (lo == 0) & (hi == rb)

        @pl.when(whole)
        def _():
            send_block(lambda: y)

        @pl.when(jnp.logical_not(whole))
        def _():
            rows = lax.broadcasted_iota(jnp.int32, (rb, 1), 0)
            mine = (rows >= lo) & (rows < hi)

            @pl.when(lo == 0)
            def _():
                ys_ref[...] = jnp.where(mine, y, 0.0)

            @pl.when(lo > 0)
            def _():
                ys_ref[...] = jnp.where(mine, y, ys_ref[...])

            @pl.when(hi == rb)
            def _():
                send_block(lambda: ys_ref[...])

    @pl.when(v == pl.num_programs(0) - 1)
    def _():
        @pl.when(pending_ref[0] == 1)
        def _():
            wait_scatter()
            pending_ref[0] = 0


def _expert_call(plan, tok_sorted, slot_sorted, x1rows, w_gate, w_up, w_down, rb):
    n_assign = tok_sorted.shape[0]
    n_blocks = n_assign // rb
    n_visits = plan[0].shape[0]
    tok3 = tok_sorted.reshape(n_blocks, 1, rb)
    slot3 = slot_sorted.reshape(n_blocks, 1, rb)
    w_map = lambda v, vb, ve, vlo, vhi: (ve[v], 0, 0)
    idx_block = (1, 1, rb)
    grid_spec = pltpu.PrefetchScalarGridSpec(
        num_scalar_prefetch=4,
        grid=(n_visits,),
        in_specs=[
            pl.BlockSpec(idx_block, lambda v, vb, ve, vlo, vhi: (0, 0, 0), memory_space=pltpu.SMEM),
            pl.BlockSpec(idx_block, lambda v, vb, ve, vlo, vhi: (jnp.minimum(vb[v] + 1, n_blocks - 1), 0, 0),
                         memory_space=pltpu.SMEM),
            pl.BlockSpec(idx_block, lambda v, vb, ve, vlo, vhi: (vb[v], 0, 0), memory_space=pltpu.SMEM),
            pl.BlockSpec(memory_space=pl.ANY),
            pl.BlockSpec((None, D_MODEL, D_EXPERT), w_map),
            pl.BlockSpec((None, D_MODEL, D_EXPERT), w_map),
            pl.BlockSpec((None, D_EXPERT, D_MODEL), w_map),
        ],
        out_specs=pl.BlockSpec(memory_space=pl.ANY),
        scratch_shapes=[
            pltpu.VMEM((rb, 1, D_MODEL), F32),
            pltpu.VMEM((rb, D_MODEL), F32),
            pltpu.VMEM((rb, D_MODEL), BF16),
            pltpu.VMEM((rb, D_MODEL), F32),
            pltpu.VMEM((rb, 1, D_MODEL), F32),
            pltpu.VMEM((D_MODEL, D_EXPERT), BF16),
            pltpu.VMEM((D_MODEL, D_EXPERT), BF16),
            pltpu.VMEM((D_EXPERT, D_MODEL), BF16),
            pltpu.SMEM((1,), jnp.int32),
            pltpu.SemaphoreType.DMA(()),
            pltpu.SemaphoreType.DMA(()),
        ],
    )
    return pl.pallas_call(
        functools.partial(_expert_kernel, n_blocks=n_blocks),
        grid_spec=grid_spec,
        out_shape=jax.ShapeDtypeStruct((n_assign, 1, D_MODEL), F32),
        compiler_params=pltpu.CompilerParams(
            dimension_semantics=("arbitrary",), vmem_limit_bytes=VMEM_LIMIT),
        name="experts",
    )(*plan, tok3, tok3, slot3, x1rows, w_gate, w_up, w_down)


def _final_kernel(x_ref, y0_ref, y1_ref, w_ref, g_ref, b_ref, o_ref, xs_ref, ya_ref, yb_ref):
    tm = o_ref.shape[0]
    xs_ref[...] = x_ref[...].reshape(tm, D_MODEL)
    ya_ref[...] = y0_ref[...].reshape(tm, D_MODEL)
    yb_ref[...] = y1_ref[...].reshape(tm, D_MODEL)
    w = w_ref[...]
    moe = ya_ref[...] * w[:, 0:1] + yb_ref[...] * w[:, 1:2]
    o_ref[...] = _layer_norm_rows(ALPHA * xs_ref[...] + moe, g_ref[...], b_ref[...])


def _final_call(x1rows, y2, w_tok, g, b):
    n = x1rows.shape[0]
    tm = FINAL_TM
    fixed = lambda i: (0, 0)
    return pl.pallas_call(
        _final_kernel,
        grid=(n // tm,),
        in_specs=[
            pl.BlockSpec((tm, 1, D_MODEL), lambda i: (i, 0, 0)),
            pl.BlockSpec((None, tm, 1, D_MODEL), lambda i: (0, i, 0, 0)),
            pl.BlockSpec((None, tm, 1, D_MODEL), lambda i: (1, i, 0, 0)),
            pl.BlockSpec((tm, TOP_K), lambda i: (i, 0)),
            pl.BlockSpec((1, D_MODEL), fixed),
            pl.BlockSpec((1, D_MODEL), fixed),
        ],
        out_specs=pl.BlockSpec((tm, D_MODEL), lambda i: (i, 0)),
        out_shape=jax.ShapeDtypeStruct((n, D_MODEL), F32),
        scratch_shapes=[pltpu.VMEM((tm, D_MODEL), F32)] * 3,
        compiler_params=pltpu.CompilerParams(
            dimension_semantics=("arbitrary",), vmem_limit_bytes=VMEM_LIMIT),
        name="combine_ln",
    )(x1rows, y2, y2, w_tok, g, b)


def _t5_bucket(rel):
    half = N_BUCKETS // 2
    exact = half // 2
    n = jnp.abs(rel)
    nf = jnp.maximum(n, 1).astype(F32)
    large = exact + (jnp.log(nf / exact) / math.log(MAX_DISTANCE / exact) * (half - exact)).astype(jnp.int32)
    large = jnp.minimum(large, half - 1)
    return jnp.where(rel > 0, half, 0) + jnp.where(n < exact, n, large)


def _window_bias(rel_bias):
    rel = jnp.arange(3 * BLOCK)[None, :] - BLOCK - jnp.arange(BLOCK)[:, None]
    onehot = (_t5_bucket(rel)[..., None] == jnp.arange(N_BUCKETS)).astype(F32)
    bias = jnp.einsum("qkb,bh->hqk", onehot, rel_bias.astype(F32), precision=lax.Precision.HIGHEST)
    return jnp.where((jnp.abs(rel) <= WINDOW)[None], bias, NEG_BIG)


def _rope_tables(seq_len):
    rows = seq_len // GRID_W
    row_ids = jnp.repeat(jnp.arange(rows), GRID_W).astype(F32)
    col_ids = jnp.tile(jnp.arange(GRID_W), rows).astype(F32)
    half = HEAD_DIM // 2
    inv = 1.0 / (ROPE_THETA ** (jnp.arange(0, half, 2, dtype=F32) / half))
    ang = jnp.concatenate([row_ids[:, None] * inv, col_ids[:, None] * inv], -1)
    cos, sin = jnp.cos(ang), jnp.sin(ang)
    return jnp.concatenate([cos, cos], -1), jnp.concatenate([-sin, sin], -1)


def _deinterleave_cols(w, n_heads):
    d = w.shape[0]
    return w.reshape(d, n_heads, HEAD_DIM // 2, 2).transpose(0, 1, 3, 2).reshape(d, n_heads * HEAD_DIM)


def _deinterleave_gain(g):
    return g.reshape(HEAD_DIM // 2, 2).T.reshape(1, HEAD_DIM)


def _expert_row_block(n_assign):
    return MOE_RB if n_assign // N_EXPERTS >= 8 * MOE_RB else MOE_RB // 2


def _trunk(x, p):
    bsz, seq_len, d = x.shape
    n = bsz * seq_len
    n_assign = n * TOP_K
    x2 = x.reshape(n, d)
    cos_t, sin_t = _rope_tables(seq_len)
    xn, qa, ka, va, qb, kb, vb = _qkv_call(x2, p["emb_g"], p["emb_b"], p["w_qkv"], cos_t, sin_t,
                                           p["q_gain"], p["k_gain"], seq_len)
    oa = _attn_a_call(p["sink"], qa, ka, va, p["bias_a"], bsz, seq_len)
    ob = _attn_b_call(qb, kb, vb, bsz, seq_len)
    x1rows, lt = _merge_call(xn, oa, ob, p["w_a"], p["w_b"], p["w_g"], p["b_g"], p["w_o"],
                             p["ln1_g"], p["ln1_b"], p["wr_hi"], p["wr_lo"], p["b_r"])
    e_rows, w_rows, cnt = _route_call(lt)
    e_flat = e_rows[:TOP_K].reshape(n_assign)
    _, slot_sorted = lax.sort((e_flat, jnp.arange(n_assign, dtype=jnp.int32)), num_keys=1)
    tok_sorted = jnp.where(slot_sorted >= n, slot_sorted - n, slot_sorted)
    ends = jnp.cumsum(cnt[:, 0].astype(jnp.int32))
    starts = jnp.concatenate([jnp.zeros((1,), jnp.int32), ends])
    rb = _expert_row_block(n_assign)
    plan = _plan_call(starts, n_assign, rb)
    y2 = _expert_call(plan, tok_sorted, slot_sorted, x1rows, p["w_gate"], p["w_up"], p["w_down"], rb)
    y2 = y2.reshape(TOP_K, n, 1, d)
    out = _final_call(x1rows, y2, w_rows[:TOP_K].T, p["ln2_g"], p["ln2_b"])
    return out.reshape(bsz, seq_len, d)


def kernel(x_prompt, x_sample, emb_ln_g, emb_ln_b, rel_bias, w_in, b_gate, sink_a, q_norm_g, k_norm_g,
           w_branch_a, w_branch_b, w_out, ln1_g, ln1_b, w_coarse, b_coarse, w_fine, b_fine,
           w_gate, w_up, w_down, ln2_g, ln2_b):
    l = 0
    w_in_l = w_in[l]
    qa_end = Q_COLS
    kva_end = qa_end + 2 * KV_COLS
    qb_end = kva_end + Q_COLS
    kb_end = qb_end + KV_COLS
    vb_end = kb_end + KV_COLS
    w_qkv = jnp.concatenate([
        w_in_l[:, :kva_end],
        _deinterleave_cols(w_in_l[:, kva_end:qb_end], N_HEADS),
        _deinterleave_cols(w_in_l[:, qb_end:kb_end], N_KV),
        w_in_l[:, kb_end:vb_end],
    ], axis=1).astype(BF16)
    w_router = jnp.zeros((ROUTER_ROWS, D_MODEL), F32)
    w_router = w_router.at[0:N_GROUPS].set(w_coarse[l].T)
    w_router = w_router.at[FINE_ROW0:FINE_ROW0 + N_EXPERTS].set(w_fine[l].T)
    wr_hi = w_router.astype(BF16)
    wr_lo = (w_router - wr_hi.astype(F32)).astype(BF16)
    b_router = jnp.zeros((ROUTER_ROWS, 1), F32)
    b_router = b_router.at[0:N_GROUPS, 0].set(b_coarse[l].astype(F32))
    b_router = b_router.at[FINE_ROW0:FINE_ROW0 + N_EXPERTS, 0].set(b_fine[l].astype(F32))
    p = {
        "emb_g": emb_ln_g.reshape(1, D_MODEL), "emb_b": emb_ln_b.reshape(1, D_MODEL),
        "w_qkv": w_qkv,
        "q_gain": _deinterleave_gain(q_norm_g[l]), "k_gain": _deinterleave_gain(k_norm_g[l]),
        "sink": sink_a[l].astype(F32), "bias_a": _window_bias(rel_bias),
        "w_a": w_branch_a[l].astype(BF16), "w_b": w_branch_b[l].astype(BF16),
        "w_g": w_in_l[:, vb_end:].astype(BF16), "b_g": b_gate[l].reshape(1, 2 * D_MODEL),
        "w_o": w_out[l].astype(BF16),
        "ln1_g": ln1_g[l].reshape(1, D_MODEL), "ln1_b": ln1_b[l].reshape(1, D_MODEL),
        "wr_hi": wr_hi, "wr_lo": wr_lo, "b_r": b_router,
        "w_gate": w_gate[l], "w_up": w_up[l], "w_down": w_down[l],
        "ln2_g": ln2_g[l].reshape(1, D_MODEL), "ln2_b": ln2_b[l].reshape(1, D_MODEL),
    }
    return _trunk(x_prompt, p), _trunk(x_sample, p)
```

```python
import functools
import math

import numpy as np
import jax
import jax.numpy as jnp
from jax import lax
from jax.experimental import pallas as pl
from jax.experimental.pallas import tpu as pltpu

F32 = jnp.float32
BF16 = jnp.bfloat16

D_MODEL = 2048
HEAD_DIM = 128
N_HEADS = 8
N_KV = 2
GROUP = N_HEADS // N_KV
Q_COLS = N_HEADS * HEAD_DIM
KV_COLS = N_KV * HEAD_DIM
QKV_COLS = 2 * (Q_COLS + 2 * KV_COLS)
WINDOW = 128
BLOCK = 128
GRID_W = 64
ROPE_THETA = 10000.0
N_BUCKETS = 32
MAX_DISTANCE = 128
N_GROUPS = 4
EXPERTS_PER_GROUP = 8
N_EXPERTS = N_GROUPS * EXPERTS_PER_GROUP
TOP_K = 2
D_EXPERT = 512
LN_EPS = 1e-5
RMS_EPS = 1e-6
DEPTH = 1
ALPHA = (2 * DEPTH) ** 0.25
SCALE = HEAD_DIM ** -0.5
LOG2E = math.log2(math.e)
NEG_BIG = -1e30

LANES = 128
VMEM_LIMIT = 56 * 1024 * 1024

QKV_TM = 512
ATT_A_TQ = 512
ATT_B_TQ = 256
ATT_B_TK = 512
MERGE_TM = 256
MERGE_TC = 1024
ROUTE_TN = 1024
MOE_RB = 256
FINAL_TM = 512
ROUTER_ROWS = 128
FINE_ROW0 = 8


def _layer_norm_rows(x, g, b):
    mu = jnp.mean(x, axis=-1, keepdims=True)
    xc = x - mu
    var = jnp.mean(xc * xc, axis=-1, keepdims=True)
    return xc * lax.rsqrt(var + LN_EPS) * g + b


def _dot(a, b):
    return jnp.dot(a, b, preferred_element_type=F32)


def _dot_nt(a, b):
    return lax.dot_general(a, b, (((1,), (1,)), ((), ())), preferred_element_type=F32)


def _qkv_kernel(x_ref, g_ref, b_ref, w_ref, cos_ref, sin_ref, qg_ref, kg_ref,
                xn_ref, qa_ref, ka_ref, va_ref, qb_ref, kb_ref, vb_ref):
    xn = _layer_norm_rows(x_ref[...], g_ref[...], b_ref[...])
    xn_ref[...] = xn
    xb = xn.astype(BF16)
    cos = cos_ref[...]
    sin = sin_ref[...]

    def proj(c0, width):
        return _dot(xb, w_ref[:, c0:c0 + width])

    def norm_rope(h, gain):
        y = h * lax.rsqrt(jnp.mean(h * h, axis=-1, keepdims=True) + RMS_EPS) * gain
        return y * cos + pltpu.roll(y, HEAD_DIM // 2, 1) * sin

    c = 0
    qa_ref[...] = (proj(c, Q_COLS) * SCALE).astype(BF16)
    c += Q_COLS
    ka_ref[...] = proj(c, KV_COLS).astype(BF16)
    c += KV_COLS
    va_ref[...] = proj(c, KV_COLS).astype(BF16)
    c += KV_COLS
    qg = qg_ref[...]
    for hp in range(N_HEADS // 2):
        hq2 = proj(c + 2 * hp * HEAD_DIM, 2 * HEAD_DIM)
        for h in (2 * hp, 2 * hp + 1):
            hq = hq2[:, (h - 2 * hp) * HEAD_DIM:(h - 2 * hp + 1) * HEAD_DIM]
            qb_ref[:, h * HEAD_DIM:(h + 1) * HEAD_DIM] = (norm_rope(hq, qg) * (SCALE * LOG2E)).astype(BF16)
    c += Q_COLS
    kg = kg_ref[...]
    hk2 = proj(c, KV_COLS)
    for h in range(N_KV):
        kb_ref[:, h * HEAD_DIM:(h + 1) * HEAD_DIM] = norm_rope(hk2[:, h * HEAD_DIM:(h + 1) * HEAD_DIM], kg).astype(BF16)
    c += KV_COLS
    hv2 = proj(c, KV_COLS)
    ones = jnp.ones((x_ref.shape[0], HEAD_DIM), BF16)
    for h in range(N_KV):
        vb_ref[:, 2 * h * HEAD_DIM:(2 * h + 1) * HEAD_DIM] = hv2[:, h * HEAD_DIM:(h + 1) * HEAD_DIM].astype(BF16)
        vb_ref[:, (2 * h + 1) * HEAD_DIM:(2 * h + 2) * HEAD_DIM] = ones


def _qkv_call(x2, g, b, w_qkv, cos_t, sin_t, qg, kg, seq_len):
    n = x2.shape[0]
    tm = QKV_TM
    pos_blocks = seq_len // tm
    row = lambda i: (i, 0)
    fixed = lambda i: (0, 0)
    pos = lambda i: (i % pos_blocks, 0)
    out_shapes = (
        jax.ShapeDtypeStruct((n, D_MODEL), F32),
        jax.ShapeDtypeStruct((n, Q_COLS), BF16),
        jax.ShapeDtypeStruct((n, KV_COLS), BF16),
        jax.ShapeDtypeStruct((n, KV_COLS), BF16),
        jax.ShapeDtypeStruct((n, Q_COLS), BF16),
        jax.ShapeDtypeStruct((n, KV_COLS), BF16),
        jax.ShapeDtypeStruct((n, 2 * KV_COLS), BF16),
    )
    return pl.pallas_call(
        _qkv_kernel,
        grid=(n // tm,),
        in_specs=[
            pl.BlockSpec((tm, D_MODEL), row),
            pl.BlockSpec((1, D_MODEL), fixed),
            pl.BlockSpec((1, D_MODEL), fixed),
            pl.BlockSpec((D_MODEL, QKV_COLS), fixed, pipeline_mode=pl.Buffered(1)),
            pl.BlockSpec((tm, HEAD_DIM), pos),
            pl.BlockSpec((tm, HEAD_DIM), pos),
            pl.BlockSpec((1, HEAD_DIM), fixed),
            pl.BlockSpec((1, HEAD_DIM), fixed),
        ],
        out_specs=(
            pl.BlockSpec((tm, D_MODEL), row),
            pl.BlockSpec((tm, Q_COLS), row),
            pl.BlockSpec((tm, KV_COLS), row),
            pl.BlockSpec((tm, KV_COLS), row),
            pl.BlockSpec((tm, Q_COLS), row),
            pl.BlockSpec((tm, KV_COLS), row),
            pl.BlockSpec((tm, 2 * KV_COLS), row),
        ),
        out_shape=out_shapes,
        compiler_params=pltpu.CompilerParams(
            dimension_semantics=("arbitrary",), vmem_limit_bytes=VMEM_LIMIT),
        name="ln_qkv",
    )(x2, g, b, w_qkv, cos_t, sin_t, qg, kg)


def _attn_a_kernel(sink_ref, q_ref, kc_ref, kp_ref, kn_ref, vc_ref, vp_ref, vn_ref, bias_ref,
                   o_ref, kcat_ref, vcat_ref, sa_ref, sb_ref, *, seq_len):
    tq = ATT_A_TQ
    i = pl.program_id(1)
    kcat_ref[0:BLOCK, :] = kp_ref[...]
    kcat_ref[BLOCK:BLOCK + tq, :] = kc_ref[...]
    kcat_ref[BLOCK + tq:, :] = kn_ref[...]
    vcat_ref[0:BLOCK, :] = vp_ref[...]
    vcat_ref[BLOCK:BLOCK + tq, :] = vc_ref[...]
    vcat_ref[BLOCK + tq:, :] = vn_ref[...]
    col = lax.broadcasted_iota(jnp.int32, (1, 3 * BLOCK), 1)
    s_refs = (sa_ref, sb_ref)

    def block_scores(j, s_ref):
        qj = q_ref[j * BLOCK:(j + 1) * BLOCK, :]
        for g in range(N_KV):
            qs = jnp.concatenate(
                [qj[:, h * HEAD_DIM:(h + 1) * HEAD_DIM] for h in range(g * GROUP, (g + 1) * GROUP)], axis=0)
            kw = kcat_ref[j * BLOCK:(j + 3) * BLOCK, g * HEAD_DIM:(g + 1) * HEAD_DIM]
            s_ref[g * GROUP * BLOCK:(g + 1) * GROUP * BLOCK, :] = _dot_nt(qs, kw)

    def block_out(j, s_ref):
        key_pos = i * tq + (j - 1) * BLOCK + col
        inside = (key_pos >= 0) & (key_pos < seq_len)
        for g in range(N_KV):
            probs = []
            for h in range(g * GROUP, (g + 1) * GROUP):
                s = s_ref[h * BLOCK:(h + 1) * BLOCK, :]
                s = jnp.where(inside, s + bias_ref[h], NEG_BIG)
                sk = sink_ref[h]
                m = jnp.maximum(jnp.max(s, axis=-1, keepdims=True), sk)
                p = jnp.exp(s - m)
                denom = jnp.sum(p, axis=-1, keepdims=True) + jnp.exp(sk - m)
                probs.append((p / denom).astype(BF16))
            vw = vcat_ref[j * BLOCK:(j + 3) * BLOCK, g * HEAD_DIM:(g + 1) * HEAD_DIM]
            o = _dot(jnp.concatenate(probs, axis=0), vw)
            for hh in range(GROUP):
                h = g * GROUP + hh
                o_ref[j * BLOCK:(j + 1) * BLOCK, h * HEAD_DIM:(h + 1) * HEAD_DIM] = (
                    o[hh * BLOCK:(hh + 1) * BLOCK, :].astype(BF16))

    n_blk = tq // BLOCK
    block_scores(0, s_refs[0])
    for j in range(n_blk):
        if j + 1 < n_blk:
            block_scores(j + 1, s_refs[(j + 1) % 2])
        block_out(j, s_refs[j % 2])


def _attn_a_call(sink, qa, ka, va, bias, bsz, seq_len):
    tq = ATT_A_TQ
    nb = seq_len // BLOCK
    bpt = tq // BLOCK
    q3 = qa.reshape(bsz, seq_len, Q_COLS)
    k3 = ka.reshape(bsz, seq_len, KV_COLS)
    v3 = va.reshape(bsz, seq_len, KV_COLS)
    cur = lambda b, i: (b, i, 0)
    prev = lambda b, i: (b, jnp.maximum(i * bpt - 1, 0), 0)
    nxt = lambda b, i: (b, jnp.minimum((i + 1) * bpt, nb - 1), 0)
    kv_cur = pl.BlockSpec((None, tq, KV_COLS), cur)
    kv_prev = pl.BlockSpec((None, BLOCK, KV_COLS), prev)
    kv_next = pl.BlockSpec((None, BLOCK, KV_COLS), nxt)
    out = pl.pallas_call(
        functools.partial(_attn_a_kernel, seq_len=seq_len),
        grid=(bsz, seq_len // tq),
        in_specs=[
            pl.BlockSpec(memory_space=pltpu.SMEM),
            pl.BlockSpec((None, tq, Q_COLS), cur),
            kv_cur, kv_prev, kv_next,
            kv_cur, kv_prev, kv_next,
            pl.BlockSpec((N_HEADS, BLOCK, 3 * BLOCK), lambda b, i: (0, 0, 0)),
        ],
        out_specs=pl.BlockSpec((None, tq, Q_COLS), cur),
        out_shape=jax.ShapeDtypeStruct((bsz, seq_len, Q_COLS), BF16),
        scratch_shapes=[
            pltpu.VMEM((tq + 2 * BLOCK, KV_COLS), BF16),
            pltpu.VMEM((tq + 2 * BLOCK, KV_COLS), BF16),
            pltpu.VMEM((N_HEADS * BLOCK, 3 * BLOCK), F32),
            pltpu.VMEM((N_HEADS * BLOCK, 3 * BLOCK), F32),
        ],
        compiler_params=pltpu.CompilerParams(
            dimension_semantics=("arbitrary", "arbitrary"), vmem_limit_bytes=VMEM_LIMIT),
        name="attn_window",
    )(sink, q3, k3, k3, k3, v3, v3, v3, bias)
    return out.reshape(bsz * seq_len, Q_COLS)


def _attn_b_kernel(q_ref, qn_ref, k_ref, v_ref, o_ref, qs_ref, qsn_ref, m_ref, acc_ref,
                   s0_ref, s1_ref, p0_ref, p1_ref, a0_ref, a1_ref, *, seq_len):
    tq = ATT_B_TQ
    tk = ATT_B_TK
    n_chunks = seq_len // tk
    rows = GROUP * tq
    for h in range(GROUP):
        qs_ref[h * tq:(h + 1) * tq, :] = q_ref[:, h * HEAD_DIM:(h + 1) * HEAD_DIM]
        qsn_ref[h * tq:(h + 1) * tq, :] = qn_ref[:, h * HEAD_DIM:(h + 1) * HEAD_DIM]

    def scores(q_src, c, s_ref):
        k0 = pl.multiple_of(c * tk, tk)
        s_ref[...] = _dot_nt(q_src[...], k_ref[pl.ds(k0, tk), :])

    def softmax(s_ref, p_ref, a_ref, first=False):
        m_tile = s_ref[:, 0:LANES]
        for j in range(1, tk // LANES):
            m_tile = jnp.maximum(m_tile, s_ref[:, j * LANES:(j + 1) * LANES])
        m_cur = jnp.broadcast_to(jnp.max(m_tile, axis=-1, keepdims=True), (rows, LANES))
        if first:
            m_new = m_cur
            a_ref[...] = jnp.zeros((rows, LANES), F32)
        else:
            m_prev = m_ref[...]
            m_new = jnp.maximum(m_prev, m_cur)
            a_ref[...] = jnp.exp2(m_prev - m_new)
        m_ref[...] = m_new
        for j in range(tk // LANES):
            p_ref[:, j * LANES:(j + 1) * LANES] = jnp.exp2(s_ref[:, j * LANES:(j + 1) * LANES] - m_new).astype(BF16)

    def weighted_values(c, p_ref, a_ref):
        k0 = pl.multiple_of(c * tk, tk)
        pv = _dot(p_ref[...], v_ref[pl.ds(k0, tk), :])
        a = a_ref[...]
        acc_ref[:, 0:HEAD_DIM] = a * acc_ref[:, 0:HEAD_DIM] + pv[:, 0:HEAD_DIM]
        acc_ref[:, HEAD_DIM:] = a * acc_ref[:, HEAD_DIM:] + pv[:, HEAD_DIM:]

    @pl.when(pl.program_id(2) == 0)
    def _():
        scores(qs_ref, 0, s0_ref)
        scores(qs_ref, 1, s1_ref)
        softmax(s0_ref, p0_ref, a0_ref, first=True)

    acc_ref[...] = jnp.zeros(acc_ref.shape, F32)

    def body(t, carry):
        c = 2 * t + 1
        scores(qs_ref, c + 1, s0_ref)
        softmax(s1_ref, p1_ref, a1_ref)
        weighted_values(c - 1, p0_ref, a0_ref)
        scores(qs_ref, c + 2, s1_ref)
        softmax(s0_ref, p0_ref, a0_ref)
        weighted_values(c, p1_ref, a1_ref)
        return carry

    lax.fori_loop(0, (n_chunks - 2) // 2, body, 0, unroll=True)
    scores(qsn_ref, 0, s0_ref)
    softmax(s1_ref, p1_ref, a1_ref)
    weighted_values(n_chunks - 2, p0_ref, a0_ref)
    scores(qsn_ref, 1, s1_ref)
    softmax(s0_ref, p0_ref, a0_ref, first=True)
    weighted_values(n_chunks - 1, p1_ref, a1_ref)
    o = acc_ref[:, 0:HEAD_DIM] / acc_ref[:, HEAD_DIM:]
    for h in range(GROUP):
        o_ref[:, h * HEAD_DIM:(h + 1) * HEAD_DIM] = o[h * tq:(h + 1) * tq, :].astype(BF16)


def _attn_b_call(qb, kb, vb, bsz, seq_len):
    tq = ATT_B_TQ
    tk = ATT_B_TK
    assert seq_len % tk == 0 and (seq_len // tk) % 2 == 0, "dense mixer pipeline needs an even number of key chunks"
    q3 = qb.reshape(bsz, seq_len, Q_COLS)
    k3 = kb.reshape(bsz, seq_len, KV_COLS)
    v3 = vb.reshape(bsz, seq_len, 2 * KV_COLS)
    gcols = GROUP * HEAD_DIM
    n_q = seq_len // tq
    rows = GROUP * tq
    out = pl.pallas_call(
        functools.partial(_attn_b_kernel, seq_len=seq_len),
        grid=(bsz, N_KV, n_q),
        in_specs=[
            pl.BlockSpec((None, tq, gcols), lambda b, g, i: (b, i, g)),
            pl.BlockSpec((None, tq, gcols), lambda b, g, i: (b, jnp.minimum(i + 1, n_q - 1), g)),
            pl.BlockSpec((None, seq_len, HEAD_DIM), lambda b, g, i: (b, 0, g)),
            pl.BlockSpec((None, seq_len, 2 * HEAD_DIM), lambda b, g, i: (b, 0, g)),
        ],
        out_specs=pl.BlockSpec((None, tq, gcols), lambda b, g, i: (b, i, g)),
        out_shape=jax.ShapeDtypeStruct((bsz, seq_len, Q_COLS), BF16),
        scratch_shapes=[
            pltpu.VMEM((rows, HEAD_DIM), BF16),
            pltpu.VMEM((rows, HEAD_DIM), BF16),
            pltpu.VMEM((rows, LANES), F32),
            pltpu.VMEM((rows, 2 * HEAD_DIM), F32),
            pltpu.VMEM((rows, tk), F32),
            pltpu.VMEM((rows, tk), F32),
            pltpu.VMEM((rows, tk), BF16),
            pltpu.VMEM((rows, tk), BF16),
            pltpu.VMEM((rows, LANES), F32),
            pltpu.VMEM((rows, LANES), F32),
        ],
        compiler_params=pltpu.CompilerParams(
            dimension_semantics=("arbitrary", "arbitrary", "arbitrary"), vmem_limit_bytes=VMEM_LIMIT),
        name="attn_dense",
    )(q3, q3, k3, v3)
    return out.reshape(bsz * seq_len, Q_COLS)


def _merge_kernel(xn_ref, oa_ref, ob_ref, wa_ref, wb_ref, wg_ref, bg_ref, wo_ref,
                  g_ref, b_ref, wrh_ref, wrl_ref, br_ref, x1rows_ref, lt_ref, acc_ref):
    tc = MERGE_TC
    xb = xn_ref[...].astype(BF16)
    oa = oa_ref[...]
    ob = ob_ref[...]
    for c in range(D_MODEL // tc):
        lo, hi = c * tc, (c + 1) * tc
        ga = jax.nn.sigmoid(_dot(xb, wg_ref[:, lo:hi]) + bg_ref[:, lo:hi])
        gb = jax.nn.sigmoid(_dot(xb, wg_ref[:, D_MODEL + lo:D_MODEL + hi]) + bg_ref[:, D_MODEL + lo:D_MODEL + hi])
        mixed = (ga * _dot(oa, wa_ref[:, lo:hi]) + gb * _dot(ob, wb_ref[:, lo:hi])).astype(BF16)
        part = _dot(mixed, wo_ref[lo:hi, :])
        if c == 0:
            acc_ref[...] = part
        else:
            acc_ref[...] += part
    x1 = _layer_norm_rows(ALPHA * xn_ref[...] + acc_ref[...], g_ref[...], b_ref[...])
    x1rows_ref[...] = x1.reshape(x1.shape[0], 1, D_MODEL)
    xh = x1.astype(BF16)
    xl = (x1 - xh.astype(F32)).astype(BF16)
    wrh = wrh_ref[...]
    lt = _dot_nt(wrh, xh) + (_dot_nt(wrh, xl) + _dot_nt(wrl_ref[...], xh))
    lt_ref[...] = lt + br_ref[...]


def _merge_call(xn, oa, ob, wa, wb, wg, bg, wo, g, b, wr_hi, wr_lo, br):
    n = xn.shape[0]
    tm = MERGE_TM
    row = lambda i: (i, 0)
    fixed = lambda i: (0, 0)

    def resident(shape):
        return pl.BlockSpec(shape, fixed, pipeline_mode=pl.Buffered(1))

    return pl.pallas_call(
        _merge_kernel,
        grid=(n // tm,),
        in_specs=[
            pl.BlockSpec((tm, D_MODEL), row),
            pl.BlockSpec((tm, Q_COLS), row),
            pl.BlockSpec((tm, Q_COLS), row),
            resident((Q_COLS, D_MODEL)),
            resident((Q_COLS, D_MODEL)),
            resident((D_MODEL, 2 * D_MODEL)),
            resident((1, 2 * D_MODEL)),
            resident((D_MODEL, D_MODEL)),
            resident((1, D_MODEL)),
            resident((1, D_MODEL)),
            resident((ROUTER_ROWS, D_MODEL)),
            resident((ROUTER_ROWS, D_MODEL)),
            resident((ROUTER_ROWS, 1)),
        ],
        out_specs=(
            pl.BlockSpec((tm, 1, D_MODEL), lambda i: (i, 0, 0)),
            pl.BlockSpec((ROUTER_ROWS, tm), lambda i: (0, i)),
        ),
        out_shape=(
            jax.ShapeDtypeStruct((n, 1, D_MODEL), F32),
            jax.ShapeDtypeStruct((ROUTER_ROWS, n), F32),
        ),
        scratch_shapes=[pltpu.VMEM((tm, D_MODEL), F32)],
        compiler_params=pltpu.CompilerParams(
            dimension_semantics=("arbitrary",), vmem_limit_bytes=VMEM_LIMIT),
        name="merge_ln_router",
    )(xn, oa, ob, wa, wb, wg, bg, wo, g, b, wr_hi, wr_lo, br)


def _route_kernel(lt_ref, e_ref, w_ref, cnt_ref, carry_ref):
    tn = lt_ref.shape[1]

    @pl.when(pl.program_id(0) == 0)
    def _():
        carry_ref[...] = jnp.zeros(carry_ref.shape, F32)

    cl = lt_ref[0:N_GROUPS, :]
    gi = lax.broadcasted_iota(jnp.int32, (N_GROUPS, tn), 0)
    cmax = jnp.max(cl, axis=0, keepdims=True)
    ce = jnp.exp(cl - cmax)
    cp = ce / jnp.sum(ce, axis=0, keepdims=True)
    g_idx = jnp.min(jnp.where(cl == cmax, gi, N_GROUPS), axis=0, keepdims=True)
    g_prob = jnp.sum(jnp.where(gi == g_idx, cp, 0.0), axis=0, keepdims=True)

    fl = jnp.zeros((EXPERTS_PER_GROUP, tn), F32)
    for g in range(N_GROUPS):
        r0 = FINE_ROW0 + g * EXPERTS_PER_GROUP
        fl = jnp.where(g_idx == g, lt_ref[r0:r0 + EXPERTS_PER_GROUP, :], fl)
    fmax = jnp.max(fl, axis=0, keepdims=True)
    fe = jnp.exp(fl - fmax)
    fp = fe / jnp.sum(fe, axis=0, keepdims=True)
    ei = lax.broadcasted_iota(jnp.int32, (EXPERTS_PER_GROUP, tn), 0)
    p1 = jnp.max(fp, axis=0, keepdims=True)
    i1 = jnp.min(jnp.where(fp == p1, ei, EXPERTS_PER_GROUP), axis=0, keepdims=True)
    fp2 = jnp.where(ei == i1, -1.0, fp)
    p2 = jnp.max(fp2, axis=0, keepdims=True)
    i2 = jnp.min(jnp.where(fp2 == p2, ei, EXPERTS_PER_GROUP), axis=0, keepdims=True)
    tot = p1 + p2
    w1 = g_prob * p1 / tot
    w2 = g_prob * p2 / tot
    e1 = g_idx * EXPERTS_PER_GROUP + i1
    e2 = g_idx * EXPERTS_PER_GROUP + i2
    ri = lax.broadcasted_iota(jnp.int32, (8, tn), 0)
    e_ref[...] = jnp.where(ri == 0, e1, jnp.where(ri == 1, e2, 0))
    w_ref[...] = jnp.where(ri == 0, w1, jnp.where(ri == 1, w2, 0.0))

    xi = lax.broadcasted_iota(jnp.int32, (N_EXPERTS, tn), 0)
    hits = jnp.where(xi == e1, 1.0, 0.0) + jnp.where(xi == e2, 1.0, 0.0)
    total = carry_ref[...] + jnp.sum(hits, axis=1, keepdims=True)
    carry_ref[...] = total
    cnt_ref[...] = jnp.broadcast_to(total, cnt_ref.shape)


def _route_call(lt):
    n = lt.shape[1]
    tn = min(ROUTE_TN, n)
    tok = lambda i: (0, i)
    return pl.pallas_call(
        _route_kernel,
        grid=(n // tn,),
        in_specs=[pl.BlockSpec((ROUTER_ROWS, tn), tok)],
        out_specs=(pl.BlockSpec((8, tn), tok), pl.BlockSpec((8, tn), tok),
                   pl.BlockSpec((N_EXPERTS, LANES), lambda i: (0, 0))),
        out_shape=(jax.ShapeDtypeStruct((8, n), jnp.int32), jax.ShapeDtypeStruct((8, n), F32),
                   jax.ShapeDtypeStruct((N_EXPERTS, LANES), F32)),
        scratch_shapes=[pltpu.VMEM((N_EXPERTS, 1), F32)],
        compiler_params=pltpu.CompilerParams(dimension_semantics=("arbitrary",)),
        name="router",
    )(lt)


def _plan_kernel(starts_ref, vb_ref, ve_ref, vlo_ref, vhi_ref, *, rb):
    n_visits = vb_ref.shape[0]
    shift = rb.bit_length() - 1

    def expert_body(e, v):
        end = starts_ref[e + 1]

        def cond(state):
            return state[0] < end

        def body(state):
            r, v = state
            b = lax.shift_right_logical(r, shift)
            hi = jnp.minimum(end, (b + 1) * rb)
            vb_ref[v] = b
            ve_ref[v] = e
            vlo_ref[v] = r - b * rb
            vhi_ref[v] = hi - b * rb
            return hi, v + 1

        return lax.while_loop(cond, body, (starts_ref[e], v))[1]

    used = lax.fori_loop(0, N_EXPERTS, expert_body, 0)
    last_b = vb_ref[used - 1]
    last_e = ve_ref[used - 1]

    def pad(i, carry):
        vb_ref[i] = last_b
        ve_ref[i] = last_e
        vlo_ref[i] = 0
        vhi_ref[i] = 0
        return carry

    lax.fori_loop(used, n_visits, pad, 0)


def _plan_call(starts, n_assign, rb):
    assert rb & (rb - 1) == 0 and n_assign % rb == 0
    n_visits = n_assign // rb + N_EXPERTS - 1
    smem = pl.BlockSpec(memory_space=pltpu.SMEM)
    sds = jax.ShapeDtypeStruct((n_visits,), jnp.int32)
    return pl.pallas_call(
        functools.partial(_plan_kernel, rb=rb),
        in_specs=[smem], out_specs=(smem,) * 4, out_shape=(sds,) * 4, name="visit_plan",
    )(starts)


def _expert_kernel(vb_ref, ve_ref, vlo_ref, vhi_ref,
                   tok0_ref, tokn_ref, slot_ref, x_hbm, wg_ref, wu_ref, wd_ref, y_hbm,
                   xin_ref, xf_ref, xb_ref, ys_ref, yout_ref, wgb_ref, wub_ref, wdb_ref,
                   pending_ref, gsem, ssem, *, n_blocks):
    rb = xb_ref.shape[0]
    v = pl.program_id(0)
    b = vb_ref[v]
    lo = vlo_ref[v]
    hi = vhi_ref[v]

    def gather_row(idx_ref, r):
        return pltpu.make_async_copy(x_hbm.at[pl.ds(idx_ref[0, 0, r], 1)], xin_ref.at[pl.ds(r, 1)], gsem)

    def scatter_row(r):
        return pltpu.make_async_copy(yout_ref.at[pl.ds(r, 1)], y_hbm.at[pl.ds(slot_ref[0, 0, r], 1)], ssem)

    def start_gather(idx_ref):
        def body(r, carry):
            gather_row(idx_ref, r).start()
            return carry
        lax.fori_loop(0, rb, body, 0, unroll=8)

    def wait_gather():
        def body(r, carry):
            gather_row(tok0_ref, r).wait()
            return carry
        lax.fori_loop(0, rb, body, 0, unroll=8)

    def wait_scatter():
        def body(r, carry):
            scatter_row(r).wait()
            return carry
        lax.fori_loop(0, rb, body, 0, unroll=8)

    def send_block(get_rows):
        @pl.when(pending_ref[0] == 1)
        def _():
            wait_scatter()

        yout_ref[...] = get_rows().reshape(rb, 1, D_MODEL)

        def body(r, carry):
            scatter_row(r).start()
            return carry
        lax.fori_loop(0, rb, body, 0, unroll=8)
        pending_ref[0] = 1

    @pl.when(v == 0)
    def _():
        pending_ref[0] = 0
        start_gather(tok0_ref)

    @pl.when(hi > lo)
    def _():
        @pl.when(lo == 0)
        def _():
            wait_gather()
            xf_ref[...] = xin_ref[...].reshape(rb, D_MODEL)
            xb_ref[...] = xf_ref[...].astype(BF16)

            @pl.when(b + 1 < n_blocks)
            def _():
                start_gather(tokn_ref)

        @pl.when((v == 0) | (ve_ref[v] != ve_ref[jnp.maximum(v - 1, 0)]))
        def _():
            wgb_ref[...] = wg_ref[...].astype(BF16)
            wub_ref[...] = wu_ref[...].astype(BF16)
            wdb_ref[...] = wd_ref[...].astype(BF16)

        x = xb_ref[...]
        hidden = jax.nn.silu(_dot(x, wgb_ref[...])) * _dot(x, wub_ref[...])
        y = _dot(hidden.astype(BF16), wdb_ref[...])
        whole = (lo == 0) & (hi == rb)

        @pl.when(whole)
        def _():
            send_block(lambda: y)

        @pl.when(jnp.logical_not(whole))
        def _():
            rows = lax.broadcasted_iota(jnp.int32, (rb, 1), 0)
            mine = (rows >= lo) & (rows < hi)

            @pl.when(lo == 0)
            def _():
                ys_ref[...] = jnp.where(mine, y, 0.0)

            @pl.when(lo > 0)
            def _():
                ys_ref[...] = jnp.where(mine, y, ys_ref[...])

            @pl.when(hi == rb)
            def _():
                send_block(lambda: ys_ref[...])

    @pl.when(v == pl.num_programs(0) - 1)
    def _():
        @pl.when(pending_ref[0] == 1)
        def _():
            wait_scatter()
            pending_ref[0] = 0


def _expert_call(plan, tok_sorted, slot_sorted, x1rows, w_gate, w_up, w_down, rb):
    n_assign = tok_sorted.shape[0]
    n_blocks = n_assign // rb
    n_visits = plan[0].shape[0]
    tok3 = tok_sorted.reshape(n_blocks, 1, rb)
    slot3 = slot_sorted.reshape(n_blocks, 1, rb)
    w_map = lambda v, vb, ve, vlo, vhi: (ve[v], 0, 0)
    idx_block = (1, 1, rb)
    grid_spec = pltpu.PrefetchScalarGridSpec(
        num_scalar_prefetch=4,
        grid=(n_visits,),
        in_specs=[
            pl.BlockSpec(idx_block, lambda v, vb, ve, vlo, vhi: (0, 0, 0), memory_space=pltpu.SMEM),
            pl.BlockSpec(idx_block, lambda v, vb, ve, vlo, vhi: (jnp.minimum(vb[v] + 1, n_blocks - 1), 0, 0),
                         memory_space=pltpu.SMEM),
            pl.BlockSpec(idx_block, lambda v, vb, ve, vlo, vhi: (vb[v], 0, 0), memory_space=pltpu.SMEM),
            pl.BlockSpec(memory_space=pl.ANY),
            pl.BlockSpec((None, D_MODEL, D_EXPERT), w_map),
            pl.BlockSpec((None, D_MODEL, D_EXPERT), w_map),
            pl.BlockSpec((None, D_EXPERT, D_MODEL), w_map),
        ],
        out_specs=pl.BlockSpec(memory_space=pl.ANY),
        scratch_shapes=[
            pltpu.VMEM((rb, 1, D_MODEL), F32),
            pltpu.VMEM((rb, D_MODEL), F32),
            pltpu.VMEM((rb, D_MODEL), BF16),
            pltpu.VMEM((rb, D_MODEL), F32),
            pltpu.VMEM((rb, 1, D_MODEL), F32),
            pltpu.VMEM((D_MODEL, D_EXPERT), BF16),
            pltpu.VMEM((D_MODEL, D_EXPERT), BF16),
            pltpu.VMEM((D_EXPERT, D_MODEL), BF16),
            pltpu.SMEM((1,), jnp.int32),
            pltpu.SemaphoreType.DMA(()),
            pltpu.SemaphoreType.DMA(()),
        ],
    )
    return pl.pallas_call(
        functools.partial(_expert_kernel, n_blocks=n_blocks),
        grid_spec=grid_spec,
        out_shape=jax.ShapeDtypeStruct((n_assign, 1, D_MODEL), F32),
        compiler_params=pltpu.CompilerParams(
            dimension_semantics=("arbitrary",), vmem_limit_bytes=VMEM_LIMIT),
        name="experts",
    )(*plan, tok3, tok3, slot3, x1rows, w_gate, w_up, w_down)


def _final_kernel(x_ref, y0_ref, y1_ref, w_ref, g_ref, b_ref, o_ref, xs_ref, ya_ref, yb_ref):
    tm = o_ref.shape[0]
    xs_ref[...] = x_ref[...].reshape(tm, D_MODEL)
    ya_ref[...] = y0_ref[...].reshape(tm, D_MODEL)
    yb_ref[...] = y1_ref[...].reshape(tm, D_MODEL)
    w = w_ref[...]
    moe = ya_ref[...] * w[:, 0:1] + yb_ref[...] * w[:, 1:2]
    o_ref[...] = _layer_norm_rows(ALPHA * xs_ref[...] + moe, g_ref[...], b_ref[...])


def _final_call(x1rows, y2, w_tok, g, b):
    n = x1rows.shape[0]
    tm = FINAL_TM
    fixed = lambda i: (0, 0)
    return pl.pallas_call(
        _final_kernel,
        grid=(n // tm,),
        in_specs=[
            pl.BlockSpec((tm, 1, D_MODEL), lambda i: (i, 0, 0)),
            pl.BlockSpec((None, tm, 1, D_MODEL), lambda i: (0, i, 0, 0)),
            pl.BlockSpec((None, tm, 1, D_MODEL), lambda i: (1, i, 0, 0)),
            pl.BlockSpec((tm, TOP_K), lambda i: (i, 0)),
            pl.BlockSpec((1, D_MODEL), fixed),
            pl.BlockSpec((1, D_MODEL), fixed),
        ],
        out_specs=pl.BlockSpec((tm, D_MODEL), lambda i: (i, 0)),
        out_shape=jax.ShapeDtypeStruct((n, D_MODEL), F32),
        scratch_shapes=[pltpu.VMEM((tm, D_MODEL), F32)] * 3,
        compiler_params=pltpu.CompilerParams(
            dimension_semantics=("arbitrary",), vmem_limit_bytes=VMEM_LIMIT),
        name="combine_ln",
    )(x1rows, y2, y2, w_tok, g, b)


def _t5_bucket(rel):
    half = N_BUCKETS // 2
    exact = half // 2
    n = jnp.abs(rel)
    nf = jnp.maximum(n, 1).astype(F32)
    large = exact + (jnp.log(nf / exact) / math.log(MAX_DISTANCE / exact) * (half - exact)).astype(jnp.int32)
    large = jnp.minimum(large, half - 1)
    return jnp.where(rel > 0, half, 0) + jnp.where(n < exact, n, large)


def _window_bias(rel_bias):
    rel = jnp.arange(3 * BLOCK)[None, :] - BLOCK - jnp.arange(BLOCK)[:, None]
    onehot = (_t5_bucket(rel)[..., None] == jnp.arange(N_BUCKETS)).astype(F32)
    bias = jnp.einsum("qkb,bh->hqk", onehot, rel_bias.astype(F32), precision=lax.Precision.HIGHEST)
    return jnp.where((jnp.abs(rel) <= WINDOW)[None], bias, NEG_BIG)


def _rope_tables(seq_len):
    rows = seq_len // GRID_W
    row_ids = jnp.repeat(jnp.arange(rows), GRID_W).astype(F32)
    col_ids = jnp.tile(jnp.arange(GRID_W), rows).astype(F32)
    half = HEAD_DIM // 2
    inv = 1.0 / (ROPE_THETA ** (jnp.arange(0, half, 2, dtype=F32) / half))
    ang = jnp.concatenate([row_ids[:, None] * inv, col_ids[:, None] * inv], -1)
    cos, sin = jnp.cos(ang), jnp.sin(ang)
    return jnp.concatenate([cos, cos], -1), jnp.concatenate([-sin, sin], -1)


def _deinterleave_cols(w, n_heads):
    d = w.shape[0]
    return w.reshape(d, n_heads, HEAD_DIM // 2, 2).transpose(0, 1, 3, 2).reshape(d, n_heads * HEAD_DIM)


def _deinterleave_gain(g):
    return g.reshape(HEAD_DIM // 2, 2).T.reshape(1, HEAD_DIM)


def _expert_row_block(n_assign):
    return MOE_RB if n_assign // N_EXPERTS >= 8 * MOE_RB else MOE_RB // 2


def _trunk(x, p):
    bsz, seq_len, d = x.shape
    n = bsz * seq_len
    n_assign = n * TOP_K
    x2 = x.reshape(n, d)
    cos_t, sin_t = _rope_tables(seq_len)
    xn, qa, ka, va, qb, kb, vb = _qkv_call(x2, p["emb_g"], p["emb_b"], p["w_qkv"], cos_t, sin_t,
                                           p["q_gain"], p["k_gain"], seq_len)
    oa = _attn_a_call(p["sink"], qa, ka, va, p["bias_a"], bsz, seq_len)
    ob = _attn_b_call(qb, kb, vb, bsz, seq_len)
    x1rows, lt = _merge_call(xn, oa, ob, p["w_a"], p["w_b"], p["w_g"], p["b_g"], p["w_o"],
                             p["ln1_g"], p["ln1_b"], p["wr_hi"], p["wr_lo"], p["b_r"])
    e_rows, w_rows, cnt = _route_call(lt)
    e_flat = e_rows[:TOP_K].reshape(n_assign)
    _, slot_sorted = lax.sort((e_flat, jnp.arange(n_assign, dtype=jnp.int32)), num_keys=1)
    tok_sorted = jnp.where(slot_sorted >= n, slot_sorted - n, slot_sorted)
    ends = jnp.cumsum(cnt[:, 0].astype(jnp.int32))
    starts = jnp.concatenate([jnp.zeros((1,), jnp.int32), ends])
    rb = _expert_row_block(n_assign)
    plan = _plan_call(starts, n_assign, rb)
    y2 = _expert_call(plan, tok_sorted, slot_sorted, x1rows, p["w_gate"], p["w_up"], p["w_down"], rb)
    y2 = y2.reshape(TOP_K, n, 1, d)
    out = _final_call(x1rows, y2, w_rows[:TOP_K].T, p["ln2_g"], p["ln2_b"])
    return out.reshape(bsz, seq_len, d)


def kernel(x_prompt, x_sample, emb_ln_g, emb_ln_b, rel_bias, w_in, b_gate, sink_a, q_norm_g, k_norm_g,
           w_branch_a, w_branch_b, w_out, ln1_g, ln1_b, w_coarse, b_coarse, w_fine, b_fine,
           w_gate, w_up, w_down, ln2_g, ln2_b):
    l = 0
    w_in_l = w_in[l]
    qa_end = Q_COLS
    kva_end = qa_end + 2 * KV_COLS
    qb_end = kva_end + Q_COLS
    kb_end = qb_end + KV_COLS
    vb_end = kb_end + KV_COLS
    w_qkv = jnp.concatenate([
        w_in_l[:, :kva_end],
        _deinterleave_cols(w_in_l[:, kva_end:qb_end], N_HEADS),
        _deinterleave_cols(w_in_l[:, qb_end:kb_end], N_KV),
        w_in_l[:, kb_end:vb_end],
    ], axis=1).astype(BF16)
    w_router = jnp.zeros((ROUTER_ROWS, D_MODEL), F32)
    w_router = w_router.at[0:N_GROUPS].set(w_coarse[l].T)
    w_router = w_router.at[FINE_ROW0:FINE_ROW0 + N_EXPERTS].set(w_fine[l].T)
    wr_hi = w_router.astype(BF16)
    wr_lo = (w_router - wr_hi.astype(F32)).astype(BF16)
    b_router = jnp.zeros((ROUTER_ROWS, 1), F32)
    b_router = b_router.at[0:N_GROUPS, 0].set(b_coarse[l].astype(F32))
    b_router = b_router.at[FINE_ROW0:FINE_ROW0 + N_EXPERTS, 0].set(b_fine[l].astype(F32))
    p = {
        "emb_g": emb_ln_g.reshape(1, D_MODEL), "emb_b": emb_ln_b.reshape(1, D_MODEL),
        "w_qkv": w_qkv,
        "q_gain": _deinterleave_gain(q_norm_g[l]), "k_gain": _deinterleave_gain(k_norm_g[l]),
        "sink": sink_a[l].astype(F32), "bias_a": _window_bias(rel_bias),
        "w_a": w_branch_a[l].astype(BF16), "w_b": w_branch_b[l].astype(BF16),
        "w_g": w_in_l[:, vb_end:].astype(BF16), "b_g": b_gate[l].reshape(1, 2 * D_MODEL),
        "w_o": w_out[l].astype(BF16),
        "ln1_g": ln1_g[l].reshape(1, D_MODEL), "ln1_b": ln1_b[l].reshape(1, D_MODEL),
        "wr_hi": wr_hi, "wr_lo": wr_lo, "b_r": b_router,
        "w_gate": w_gate[l], "w_up": w_up[l], "w_down": w_down[l],
        "ln2_g": ln2_g[l].reshape(1, D_MODEL), "ln2_b": ln2_b[l].reshape(1, D_MODEL),
    }
    return _trunk(x_prompt, p), _trunk(x_sample, p)
```

```python
import functools
import math

import numpy as np
import jax
import jax.numpy as jnp
from jax import lax
from jax.experimental import pallas as pl
from jax.experimental.pallas import tpu as pltpu

F32 = jnp.float32
BF16 = jnp.bfloat16

D_MODEL = 2048
HEAD_DIM = 128
N_HEADS = 8
N_KV = 2
GROUP = N_HEADS // N_KV
Q_COLS = N_HEADS * HEAD_DIM
KV_COLS = N_KV * HEAD_DIM
QKV_COLS = 2 * (Q_COLS + 2 * KV_COLS)
WINDOW = 128
BLOCK = 128
GRID_W = 64
ROPE_THETA = 10000.0
N_BUCKETS = 32
MAX_DISTANCE = 128
N_GROUPS = 4
EXPERTS_PER_GROUP = 8
N_EXPERTS = N_GROUPS * EXPERTS_PER_GROUP
TOP_K = 2
D_EXPERT = 512
LN_EPS = 1e-5
RMS_EPS = 1e-6
DEPTH = 1
ALPHA = (2 * DEPTH) ** 0.25
SCALE = HEAD_DIM ** -0.5
LOG2E = math.log2(math.e)
NEG_BIG = -1e30

LANES = 128
VMEM_LIMIT = 56 * 1024 * 1024

QKV_TM = 512
ATT_A_TQ = 512
ATT_B_TQ = 256
ATT_B_TK = 512
MERGE_TM = 256
MERGE_TC = 1024
ROUTE_TN = 1024
MOE_RB = 256
FINAL_TM = 512
ROUTER_ROWS = 128
FINE_ROW0 = 8


def _layer_norm_rows(x, g, b):
    mu = jnp.mean(x, axis=-1, keepdims=True)
    xc = x - mu
    var = jnp.mean(xc * xc, axis=-1, keepdims=True)
    return xc * lax.rsqrt(var + LN_EPS) * g + b


def _dot(a, b):
    return jnp.dot(a, b, preferred_element_type=F32)


def _dot_nt(a, b):
    return lax.dot_general(a, b, (((1,), (1,)), ((), ())), preferred_element_type=F32)


def _qkv_kernel(x_ref, g_ref, b_ref, w_ref, cos_ref, sin_ref, qg_ref, kg_ref,
                xn_ref, qa_ref, ka_ref, va_ref, qb_ref, kb_ref, vb_ref):
    xn = _layer_norm_rows(x_ref[...], g_ref[...], b_ref[...])
    xn_ref[...] = xn
    xb = xn.astype(BF16)
    cos = cos_ref[...]
    sin = sin_ref[...]

    def proj(c0, width):
        return _dot(xb, w_ref[:, c0:c0 + width])

    def norm_rope(h, gain):
        y = h * lax.rsqrt(jnp.mean(h * h, axis=-1, keepdims=True) + RMS_EPS) * gain
        return y * cos + pltpu.roll(y, HEAD_DIM // 2, 1) * sin

    c = Q_COLS + 2 * KV_COLS
    qg = qg_ref[...]
    for hp in range(N_HEADS // 2):
        hq2 = proj(c + 2 * hp * HEAD_DIM, 2 * HEAD_DIM)
        for h in (2 * hp, 2 * hp + 1):
            hq = hq2[:, (h - 2 * hp) * HEAD_DIM:(h - 2 * hp + 1) * HEAD_DIM]
            qb_ref[:, h * HEAD_DIM:(h + 1) * HEAD_DIM] = (norm_rope(hq, qg) * (SCALE * LOG2E)).astype(BF16)
    c += Q_COLS
    kg = kg_ref[...]
    hk2 = proj(c, KV_COLS)
    for h in range(N_KV):
        kb_ref[:, h * HEAD_DIM:(h + 1) * HEAD_DIM] = norm_rope(hk2[:, h * HEAD_DIM:(h + 1) * HEAD_DIM], kg).astype(BF16)
    c += KV_COLS
    hv2 = proj(c, KV_COLS)
    ones = jnp.ones((x_ref.shape[0], HEAD_DIM), BF16)
    for h in range(N_KV):
        vb_ref[:, 2 * h * HEAD_DIM:(2 * h + 1) * HEAD_DIM] = hv2[:, h * HEAD_DIM:(h + 1) * HEAD_DIM].astype(BF16)
        vb_ref[:, (2 * h + 1) * HEAD_DIM:(2 * h + 2) * HEAD_DIM] = ones
    c = 0
    qa_ref[...] = (proj(c, Q_COLS) * SCALE).astype(BF16)
    c += Q_COLS
    ka_ref[...] = proj(c, KV_COLS).astype(BF16)
    c += KV_COLS
    va_ref[...] = proj(c, KV_COLS).astype(BF16)


def _qkv_call(x2, g, b, w_qkv, cos_t, sin_t, qg, kg, seq_len):
    n = x2.shape[0]
    tm = QKV_TM
    pos_blocks = seq_len // tm
    row = lambda i: (i, 0)
    fixed = lambda i: (0, 0)
    pos = lambda i: (i % pos_blocks, 0)
    out_shapes = (
        jax.ShapeDtypeStruct((n, D_MODEL), F32),
        jax.ShapeDtypeStruct((n, Q_COLS), BF16),
        jax.ShapeDtypeStruct((n, KV_COLS), BF16),
        jax.ShapeDtypeStruct((n, KV_COLS), BF16),
        jax.ShapeDtypeStruct((n, Q_COLS), BF16),
        jax.ShapeDtypeStruct((n, KV_COLS), BF16),
        jax.ShapeDtypeStruct((n, 2 * KV_COLS), BF16),
    )
    return pl.pallas_call(
        _qkv_kernel,
        grid=(n // tm,),
        in_specs=[
            pl.BlockSpec((tm, D_MODEL), row),
            pl.BlockSpec((1, D_MODEL), fixed),
            pl.BlockSpec((1, D_MODEL), fixed),
            pl.BlockSpec((D_MODEL, QKV_COLS), fixed, pipeline_mode=pl.Buffered(1)),
            pl.BlockSpec((tm, HEAD_DIM), pos),
            pl.BlockSpec((tm, HEAD_DIM), pos),
            pl.BlockSpec((1, HEAD_DIM), fixed),
            pl.BlockSpec((1, HEAD_DIM), fixed),
        ],
        out_specs=(
            pl.BlockSpec((tm, D_MODEL), row),
            pl.BlockSpec((tm, Q_COLS), row),
            pl.BlockSpec((tm, KV_COLS), row),
            pl.BlockSpec((tm, KV_COLS), row),
            pl.BlockSpec((tm, Q_COLS), row),
            pl.BlockSpec((tm, KV_COLS), row),
            pl.BlockSpec((tm, 2 * KV_COLS), row),
        ),
        out_shape=out_shapes,
        compiler_params=pltpu.CompilerParams(
            dimension_semantics=("arbitrary",), vmem_limit_bytes=VMEM_LIMIT),
        name="ln_qkv",
    )(x2, g, b, w_qkv, cos_t, sin_t, qg, kg)


def _attn_a_kernel(sink_ref, q_ref, kc_ref, kp_ref, kn_ref, vc_ref, vp_ref, vn_ref, bias_ref,
                   o_ref, kcat_ref, vcat_ref, sa_ref, sb_ref, *, seq_len):
    tq = ATT_A_TQ
    i = pl.program_id(1)
    kcat_ref[0:BLOCK, :] = kp_ref[...]
    kcat_ref[BLOCK:BLOCK + tq, :] = kc_ref[...]
    kcat_ref[BLOCK + tq:, :] = kn_ref[...]
    vcat_ref[0:BLOCK, :] = vp_ref[...]
    vcat_ref[BLOCK:BLOCK + tq, :] = vc_ref[...]
    vcat_ref[BLOCK + tq:, :] = vn_ref[...]
    col = lax.broadcasted_iota(jnp.int32, (1, 3 * BLOCK), 1)
    s_refs = (sa_ref, sb_ref)

    def block_scores(j, s_ref):
        qj = q_ref[j * BLOCK:(j + 1) * BLOCK, :]
        for g in range(N_KV):
            qs = jnp.concatenate(
                [qj[:, h * HEAD_DIM:(h + 1) * HEAD_DIM] for h in range(g * GROUP, (g + 1) * GROUP)], axis=0)
            kw = kcat_ref[j * BLOCK:(j + 3) * BLOCK, g * HEAD_DIM:(g + 1) * HEAD_DIM]
            s_ref[g * GROUP * BLOCK:(g + 1) * GROUP * BLOCK, :] = _dot_nt(qs, kw)

    def block_out(j, s_ref):
        key_pos = i * tq + (j - 1) * BLOCK + col
        inside = (key_pos >= 0) & (key_pos < seq_len)
        for g in range(N_KV):
            probs = []
            for h in range(g * GROUP, (g + 1) * GROUP):
                s = s_ref[h * BLOCK:(h + 1) * BLOCK, :]
                s = jnp.where(inside, s + bias_ref[h], NEG_BIG)
                sk = sink_ref[h]
                m = jnp.maximum(jnp.max(s, axis=-1, keepdims=True), sk)
                p = jnp.exp(s - m)
                denom = jnp.sum(p, axis=-1, keepdims=True) + jnp.exp(sk - m)
                probs.append((p / denom).astype(BF16))
            vw = vcat_ref[j * BLOCK:(j + 3) * BLOCK, g * HEAD_DIM:(g + 1) * HEAD_DIM]
            o = _dot(jnp.concatenate(probs, axis=0), vw)
            for hh in range(GROUP):
                h = g * GROUP + hh
                o_ref[j * BLOCK:(j + 1) * BLOCK, h * HEAD_DIM:(h + 1) * HEAD_DIM] = (
                    o[hh * BLOCK:(hh + 1) * BLOCK, :].astype(BF16))

    n_blk = tq // BLOCK
    block_scores(0, s_refs[0])
    for j in range(n_blk):
        if j + 1 < n_blk:
            block_scores(j + 1, s_refs[(j + 1) % 2])
        block_out(j, s_refs[j % 2])


def _attn_a_call(sink, qa, ka, va, bias, bsz, seq_len):
    tq = ATT_A_TQ
    nb = seq_len // BLOCK
    bpt = tq // BLOCK
    q3 = qa.reshape(bsz, seq_len, Q_COLS)
    k3 = ka.reshape(bsz, seq_len, KV_COLS)
    v3 = va.reshape(bsz, seq_len, KV_COLS)
    cur = lambda b, i: (b, i, 0)
    prev = lambda b, i: (b, jnp.maximum(i * bpt - 1, 0), 0)
    nxt = lambda b, i: (b, jnp.minimum((i + 1) * bpt, nb - 1), 0)
    kv_cur = pl.BlockSpec((None, tq, KV_COLS), cur)
    kv_prev = pl.BlockSpec((None, BLOCK, KV_COLS), prev)
    kv_next = pl.BlockSpec((None, BLOCK, KV_COLS), nxt)
    out = pl.pallas_call(
        functools.partial(_attn_a_kernel, seq_len=seq_len),
        grid=(bsz, seq_len // tq),
        in_specs=[
            pl.BlockSpec(memory_space=pltpu.SMEM),
            pl.BlockSpec((None, tq, Q_COLS), cur),
            kv_cur, kv_prev, kv_next,
            kv_cur, kv_prev, kv_next,
            pl.BlockSpec((N_HEADS, BLOCK, 3 * BLOCK), lambda b, i: (0, 0, 0)),
        ],
        out_specs=pl.BlockSpec((None, tq, Q_COLS), cur),
        out_shape=jax.ShapeDtypeStruct((bsz, seq_len, Q_COLS), BF16),
        scratch_shapes=[
            pltpu.VMEM((tq + 2 * BLOCK, KV_COLS), BF16),
            pltpu.VMEM((tq + 2 * BLOCK, KV_COLS), BF16),
            pltpu.VMEM((N_HEADS * BLOCK, 3 * BLOCK), F32),
            pltpu.VMEM((N_HEADS * BLOCK, 3 * BLOCK), F32),
        ],
        compiler_params=pltpu.CompilerParams(
            dimension_semantics=("arbitrary", "arbitrary"), vmem_limit_bytes=VMEM_LIMIT),
        name="attn_window",
    )(sink, q3, k3, k3, k3, v3, v3, v3, bias)
    return out.reshape(bsz * seq_len, Q_COLS)


def _attn_b_kernel(q_ref, qn_ref, k_ref, v_ref, o_ref, qs_ref, qsn_ref, m_ref, acc_ref,
                   s0_ref, s1_ref, p0_ref, p1_ref, a0_ref, a1_ref, *, seq_len):
    tq = ATT_B_TQ
    tk = ATT_B_TK
    n_chunks = seq_len // tk
    rows = GROUP * tq
    for h in range(GROUP):
        qs_ref[h * tq:(h + 1) * tq, :] = q_ref[:, h * HEAD_DIM:(h + 1) * HEAD_DIM]
        qsn_ref[h * tq:(h + 1) * tq, :] = qn_ref[:, h * HEAD_DIM:(h + 1) * HEAD_DIM]

    def scores(q_src, c, s_ref):
        k0 = pl.multiple_of(c * tk, tk)
        s_ref[...] = _dot_nt(q_src[...], k_ref[pl.ds(k0, tk), :])

    def softmax(s_ref, p_ref, a_ref, first=False):
        m_tile = s_ref[:, 0:LANES]
        for j in range(1, tk // LANES):
            m_tile = jnp.maximum(m_tile, s_ref[:, j * LANES:(j + 1) * LANES])
        m_cur = jnp.broadcast_to(jnp.max(m_tile, axis=-1, keepdims=True), (rows, LANES))
        if first:
            m_new = m_cur
            a_ref[...] = jnp.zeros((rows, LANES), F32)
        else:
            m_prev = m_ref[...]
            m_new = jnp.maximum(m_prev, m_cur)
            a_ref[...] = jnp.exp2(m_prev - m_new)
        m_ref[...] = m_new
        for j in range(tk // LANES):
            p_ref[:, j * LANES:(j + 1) * LANES] = jnp.exp2(s_ref[:, j * LANES:(j + 1) * LANES] - m_new).astype(BF16)

    def weighted_values(c, p_ref, a_ref):
        k0 = pl.multiple_of(c * tk, tk)
        pv = _dot(p_ref[...], v_ref[pl.ds(k0, tk), :])
        a = a_ref[...]
        acc_ref[:, 0:HEAD_DIM] = a * acc_ref[:, 0:HEAD_DIM] + pv[:, 0:HEAD_DIM]
        acc_ref[:, HEAD_DIM:] = a * acc_ref[:, HEAD_DIM:] + pv[:, HEAD_DIM:]

    @pl.when(pl.program_id(2) == 0)
    def _():
        scores(qs_ref, 0, s0_ref)
        scores(qs_ref, 1, s1_ref)
        softmax(s0_ref, p0_ref, a0_ref, first=True)

    acc_ref[...] = jnp.zeros(acc_ref.shape, F32)

    def body(t, carry):
        c = 2 * t + 1
        scores(qs_ref, c + 1, s0_ref)
        softmax(s1_ref, p1_ref, a1_ref)
        weighted_values(c - 1, p0_ref, a0_ref)
        scores(qs_ref, c + 2, s1_ref)
        softmax(s0_ref, p0_ref, a0_ref)
        weighted_values(c, p1_ref, a1_ref)
        return carry

    lax.fori_loop(0, (n_chunks - 2) // 2, body, 0, unroll=True)
    scores(qsn_ref, 0, s0_ref)
    softmax(s1_ref, p1_ref, a1_ref)
    weighted_values(n_chunks - 2, p0_ref, a0_ref)
    scores(qsn_ref, 1, s1_ref)
    softmax(s0_ref, p0_ref, a0_ref, first=True)
    weighted_values(n_chunks - 1, p1_ref, a1_ref)
    o = acc_ref[:, 0:HEAD_DIM] / acc_ref[:, HEAD_DIM:]
    for h in range(GROUP):
        o_ref[:, h * HEAD_DIM:(h + 1) * HEAD_DIM] = o[h * tq:(h + 1) * tq, :].astype(BF16)


def _attn_b_call(qb, kb, vb, bsz, seq_len):
    tq = ATT_B_TQ
    tk = ATT_B_TK
    assert seq_len % tk == 0 and (seq_len // tk) % 2 == 0, "dense mixer pipeline needs an even number of key chunks"
    q3 = qb.reshape(bsz, seq_len, Q_COLS)
    k3 = kb.reshape(bsz, seq_len, KV_COLS)
    v3 = vb.reshape(bsz, seq_len, 2 * KV_COLS)
    gcols = GROUP * HEAD_DIM
    n_q = seq_len // tq
    rows = GROUP * tq
    out = pl.pallas_call(
        functools.partial(_attn_b_kernel, seq_len=seq_len),
        grid=(bsz, N_KV, n_q),
        in_specs=[
            pl.BlockSpec((None, tq, gcols), lambda b, g, i: (b, i, g)),
            pl.BlockSpec((None, tq, gcols), lambda b, g, i: (b, jnp.minimum(i + 1, n_q - 1), g)),
            pl.BlockSpec((None, seq_len, HEAD_DIM), lambda b, g, i: (b, 0, g)),
            pl.BlockSpec((None, seq_len, 2 * HEAD_DIM), lambda b, g, i: (b, 0, g)),
        ],
        out_specs=pl.BlockSpec((None, tq, gcols), lambda b, g, i: (b, i, g)),
        out_shape=jax.ShapeDtypeStruct((bsz, seq_len, Q_COLS), BF16),
        scratch_shapes=[
            pltpu.VMEM((rows, HEAD_DIM), BF16),
            pltpu.VMEM((rows, HEAD_DIM), BF16),
            pltpu.VMEM((rows, LANES), F32),
            pltpu.VMEM((rows, 2 * HEAD_DIM), F32),
            pltpu.VMEM((rows, tk), F32),
            pltpu.VMEM((rows, tk), F32),
            pltpu.VMEM((rows, tk), BF16),
            pltpu.VMEM((rows, tk), BF16),
            pltpu.VMEM((rows, LANES), F32),
            pltpu.VMEM((rows, LANES), F32),
        ],
        compiler_params=pltpu.CompilerParams(
            dimension_semantics=("arbitrary", "arbitrary", "arbitrary"), vmem_limit_bytes=VMEM_LIMIT),
        name="attn_dense",
    )(q3, q3, k3, v3)
    return out.reshape(bsz * seq_len, Q_COLS)


def _merge_kernel(xn_ref, oa_ref, ob_ref, wa_ref, wb_ref, wg_ref, bg_ref, wo_ref,
                  g_ref, b_ref, wrh_ref, wrl_ref, br_ref, x1rows_ref, lt_ref, acc_ref):
    tc = MERGE_TC
    xb = xn_ref[...].astype(BF16)
    oa = oa_ref[...]
    ob = ob_ref[...]
    for c in range(D_MODEL // tc):
        lo, hi = c * tc, (c + 1) * tc
        ga = jax.nn.sigmoid(_dot(xb, wg_ref[:, lo:hi]) + bg_ref[:, lo:hi])
        gb = jax.nn.sigmoid(_dot(xb, wg_ref[:, D_MODEL + lo:D_MODEL + hi]) + bg_ref[:, D_MODEL + lo:D_MODEL + hi])
        mixed = (ga * _dot(oa, wa_ref[:, lo:hi]) + gb * _dot(ob, wb_ref[:, lo:hi])).astype(BF16)
        part = _dot(mixed, wo_ref[lo:hi, :])
        if c == 0:
            acc_ref[...] = part
        else:
            acc_ref[...] += part
    x1 = _layer_norm_rows(ALPHA * xn_ref[...] + acc_ref[...], g_ref[...], b_ref[...])
    x1rows_ref[...] = x1.reshape(x1.shape[0], 1, D_MODEL)
    xh = x1.astype(BF16)
    xl = (x1 - xh.astype(F32)).astype(BF16)
    wrh = wrh_ref[...]
    lt = _dot_nt(wrh, xh) + (_dot_nt(wrh, xl) + _dot_nt(wrl_ref[...], xh))
    lt_ref[...] = lt + br_ref[...]


def _merge_call(xn, oa, ob, wa, wb, wg, bg, wo, g, b, wr_hi, wr_lo, br):
    n = xn.shape[0]
    tm = MERGE_TM
    row = lambda i: (i, 0)
    fixed = lambda i: (0, 0)

    def resident(shape):
        return pl.BlockSpec(shape, fixed, pipeline_mode=pl.Buffered(1))

    return pl.pallas_call(
        _merge_kernel,
        grid=(n // tm,),
        in_specs=[
            pl.BlockSpec((tm, D_MODEL), row),
            pl.BlockSpec((tm, Q_COLS), row),
            pl.BlockSpec((tm, Q_COLS), row),
            resident((Q_COLS, D_MODEL)),
            resident((Q_COLS, D_MODEL)),
            resident((D_MODEL, 2 * D_MODEL)),
            resident((1, 2 * D_MODEL)),
            resident((D_MODEL, D_MODEL)),
            resident((1, D_MODEL)),
            resident((1, D_MODEL)),
            resident((ROUTER_ROWS, D_MODEL)),
            resident((ROUTER_ROWS, D_MODEL)),
            resident((ROUTER_ROWS, 1)),
        ],
        out_specs=(
            pl.BlockSpec((tm, 1, D_MODEL), lambda i: (i, 0, 0)),
            pl.BlockSpec((ROUTER_ROWS, tm), lambda i: (0, i)),
        ),
        out_shape=(
            jax.ShapeDtypeStruct((n, 1, D_MODEL), F32),
            jax.ShapeDtypeStruct((ROUTER_ROWS, n), F32),
        ),
        scratch_shapes=[pltpu.VMEM((tm, D_MODEL), F32)],
        compiler_params=pltpu.CompilerParams(
            dimension_semantics=("arbitrary",), vmem_limit_bytes=VMEM_LIMIT),
        name="merge_ln_router",
    )(xn, oa, ob, wa, wb, wg, bg, wo, g, b, wr_hi, wr_lo, br)


def _route_kernel(lt_ref, e_ref, w_ref, cnt_ref, carry_ref):
    tn = lt_ref.shape[1]

    @pl.when(pl.program_id(0) == 0)
    def _():
        carry_ref[...] = jnp.zeros(carry_ref.shape, F32)

    cl = lt_ref[0:N_GROUPS, :]
    gi = lax.broadcasted_iota(jnp.int32, (N_GROUPS, tn), 0)
    cmax = jnp.max(cl, axis=0, keepdims=True)
    ce = jnp.exp(cl - cmax)
    cp = ce / jnp.sum(ce, axis=0, keepdims=True)
    g_idx = jnp.min(jnp.where(cl == cmax, gi, N_GROUPS), axis=0, keepdims=True)
    g_prob = jnp.sum(jnp.where(gi == g_idx, cp, 0.0), axis=0, keepdims=True)

    fl = jnp.zeros((EXPERTS_PER_GROUP, tn), F32)
    for g in range(N_GROUPS):
        r0 = FINE_ROW0 + g * EXPERTS_PER_GROUP
        fl = jnp.where(g_idx == g, lt_ref[r0:r0 + EXPERTS_PER_GROUP, :], fl)
    fmax = jnp.max(fl, axis=0, keepdims=True)
    fe = jnp.exp(fl - fmax)
    fp = fe / jnp.sum(fe, axis=0, keepdims=True)
    ei = lax.broadcasted_iota(jnp.int32, (EXPERTS_PER_GROUP, tn), 0)
    p1 = jnp.max(fp, axis=0, keepdims=True)
    i1 = jnp.min(jnp.where(fp == p1, ei, EXPERTS_PER_GROUP), axis=0, keepdims=True)
    fp2 = jnp.where(ei == i1, -1.0, fp)
    p2 = jnp.max(fp2, axis=0, keepdims=True)
    i2 = jnp.min(jnp.where(fp2 == p2, ei, EXPERTS_PER_GROUP), axis=0, keepdims=True)
    tot = p1 + p2
    w1 = g_prob * p1 / tot
    w2 = g_prob * p2 / tot
    e1 = g_idx * EXPERTS_PER_GROUP + i1
    e2 = g_idx * EXPERTS_PER_GROUP + i2
    ri = lax.broadcasted_iota(jnp.int32, (8, tn), 0)
    e_ref[...] = jnp.where(ri == 0, e1, jnp.where(ri == 1, e2, 0))
    w_ref[...] = jnp.where(ri == 0, w1, jnp.where(ri == 1, w2, 0.0))

    xi = lax.broadcasted_iota(jnp.int32, (N_EXPERTS, tn), 0)
    hits = jnp.where(xi == e1, 1.0, 0.0) + jnp.where(xi == e2, 1.0, 0.0)
    total = carry_ref[...] + jnp.sum(hits, axis=1, keepdims=True)
    carry_ref[...] = total
    cnt_ref[...] = jnp.broadcast_to(total, cnt_ref.shape)


def _route_call(lt):
    n = lt.shape[1]
    tn = min(ROUTE_TN, n)
    tok = lambda i: (0, i)
    return pl.pallas_call(
        _route_kernel,
        grid=(n // tn,),
        in_specs=[pl.BlockSpec((ROUTER_ROWS, tn), tok)],
        out_specs=(pl.BlockSpec((8, tn), tok), pl.BlockSpec((8, tn), tok),
                   pl.BlockSpec((N_EXPERTS, LANES), lambda i: (0, 0))),
        out_shape=(jax.ShapeDtypeStruct((8, n), jnp.int32), jax.ShapeDtypeStruct((8, n), F32),
                   jax.ShapeDtypeStruct((N_EXPERTS, LANES), F32)),
        scratch_shapes=[pltpu.VMEM((N_EXPERTS, 1), F32)],
        compiler_params=pltpu.CompilerParams(dimension_semantics=("arbitrary",)),
        name="router",
    )(lt)


def _plan_kernel(starts_ref, vb_ref, ve_ref, vlo_ref, vhi_ref, *, rb):
    n_visits = vb_ref.shape[0]
    shift = rb.bit_length() - 1

    def expert_body(e, v):
        end = starts_ref[e + 1]

        def cond(state):
            return state[0] < end

        def body(state):
            r, v = state
            b = lax.shift_right_logical(r, shift)
            hi = jnp.minimum(end, (b + 1) * rb)
            vb_ref[v] = b
            ve_ref[v] = e
            vlo_ref[v] = r - b * rb
            vhi_ref[v] = hi - b * rb
            return hi, v + 1

        return lax.while_loop(cond, body, (starts_ref[e], v))[1]

    used = lax.fori_loop(0, N_EXPERTS, expert_body, 0)
    last_b = vb_ref[used - 1]
    last_e = ve_ref[used - 1]

    def pad(i, carry):
        vb_ref[i] = last_b
        ve_ref[i] = last_e
        vlo_ref[i] = 0
        vhi_ref[i] = 0
        return carry

    lax.fori_loop(used, n_visits, pad, 0)


def _plan_call(starts, n_assign, rb):
    assert rb & (rb - 1) == 0 and n_assign % rb == 0
    n_visits = n_assign // rb + N_EXPERTS - 1
    smem = pl.BlockSpec(memory_space=pltpu.SMEM)
    sds = jax.ShapeDtypeStruct((n_visits,), jnp.int32)
    return pl.pallas_call(
        functools.partial(_plan_kernel, rb=rb),
        in_specs=[smem], out_specs=(smem,) * 4, out_shape=(sds,) * 4, name="visit_plan",
    )(starts)


def _expert_kernel(vb_ref, ve_ref, vlo_ref, vhi_ref,
                   tok0_ref, tokn_ref, slot_ref, slotp_ref, x_hbm, wg_ref, wu_ref, wd_ref, y_hbm,
                   xin_ref, xf_ref, xb_ref, ys_ref, yout_ref, wgb_ref, wub_ref, wdb_ref,
                   state_ref, gsem, ssem):
    rb = xb_ref.shape[0]
    v = pl.program_id(0)
    b = vb_ref[v]
    lo = vlo_ref[v]
    hi = vhi_ref[v]
    unsent, in_flight = 0, 1

    def gather_row(idx_ref, r):
        return pltpu.make_async_copy(x_hbm.at[pl.ds(idx_ref[0, 0, r], 1)], xin_ref.at[pl.ds(r, 1)], gsem)

    def scatter_row(buf, idx_ref, r):
        return pltpu.make_async_copy(yout_ref.at[buf, pl.ds(r, 1)], y_hbm.at[pl.ds(idx_ref[0, 0, r], 1)], ssem.at[buf])

    def start_gather(idx_ref):
        def body(r, carry):
            gather_row(idx_ref, r).start()
            return carry
        lax.fori_loop(0, rb, body, 0, unroll=8)

    def wait_gather():
        def body(r, carry):
            gather_row(tok0_ref, r).wait()
            return carry
        lax.fori_loop(0, rb, body, 0, unroll=8)

    def wait_scatter(buf):
        def body(r, carry):
            scatter_row(buf, slot_ref, r).wait()
            return carry
        lax.fori_loop(0, rb, body, 0, unroll=8)
        state_ref[in_flight + buf] = 0

    def park_block(get_rows):
        buf = b & 1

        @pl.when(state_ref[in_flight + buf] == 1)
        def _():
            wait_scatter(buf)

        yout_ref[buf] = get_rows().reshape(rb, 1, D_MODEL)
        state_ref[unsent] = 1

    def mark_sent(buf):
        state_ref[unsent] = 0
        state_ref[in_flight + buf] = 1

    @pl.when(v == 0)
    def _():
        state_ref[unsent] = 0
        state_ref[in_flight] = 0
        state_ref[in_flight + 1] = 0
        start_gather(tok0_ref)

    @pl.when(hi > lo)
    def _():
        @pl.when(lo == 0)
        def _():
            wait_gather()
            xf_ref[...] = xin_ref[...].reshape(rb, D_MODEL)
            xb_ref[...] = xf_ref[...].astype(BF16)

        @pl.when((v == 0) | (ve_ref[v] != ve_ref[jnp.maximum(v - 1, 0)]))
        def _():
            wgb_ref[...] = wg_ref[...].astype(BF16)
            wub_ref[...] = wu_ref[...].astype(BF16)
            wdb_ref[...] = wd_ref[...].astype(BF16)

        def expert_rows():
            x = xb_ref[...]
            hidden = jax.nn.silu(_dot(x, wgb_ref[...])) * _dot(x, wub_ref[...])
            return _dot(hidden.astype(BF16), wdb_ref[...])

        whole = (lo == 0) & (hi == rb)
        prev_buf = (b + 1) & 1

        @pl.when(whole & (b > 0))
        def _():
            for r in range(rb):
                gather_row(tokn_ref, r).start()
                scatter_row(prev_buf, slotp_ref, r).start()
            mark_sent(prev_buf)
            y = expert_rows()
            park_block(lambda: y)

        @pl.when(whole & (b == 0))
        def _():
            for r in range(rb):
                gather_row(tokn_ref, r).start()
            y = expert_rows()
            park_block(lambda: y)

        @pl.when(jnp.logical_not(whole))
        def _():
            @pl.when(lo == 0)
            def _():
                start_gather(tokn_ref)

                @pl.when(state_ref[unsent] == 1)
                def _():
                    def body(r, carry):
                        scatter_row(prev_buf, slotp_ref, r).start()
                        return carry
                    lax.fori_loop(0, rb, body, 0, unroll=8)
                    mark_sent(prev_buf)

            y = expert_rows()
            rows = lax.broadcasted_iota(jnp.int32, (rb, 1), 0)
            mine = (rows >= lo) & (rows < hi)

            @pl.when(lo == 0)
            def _():
                ys_ref[...] = jnp.where(mine, y, 0.0)

            @pl.when(lo > 0)
            def _():
                ys_ref[...] = jnp.where(mine, y, ys_ref[...])

            @pl.when(hi == rb)
            def _():
                park_block(lambda: ys_ref[...])

    @pl.when(v == pl.num_programs(0) - 1)
    def _():
        last_buf = b & 1

        @pl.when(state_ref[unsent] == 1)
        def _():
            def body(r, carry):
                scatter_row(last_buf, slot_ref, r).start()
                return carry
            lax.fori_loop(0, rb, body, 0, unroll=8)
            mark_sent(last_buf)

        for buf in range(2):
            @pl.when(state_ref[in_flight + buf] == 1)
            def _():
                wait_scatter(buf)

        wait_gather()


def _expert_call(plan, tok_sorted, slot_sorted, x1rows, w_gate, w_up, w_down, rb):
    n_assign = tok_sorted.shape[0]
    n_blocks = n_assign // rb
    n_visits = plan[0].shape[0]
    tok3 = tok_sorted.reshape(n_blocks, 1, rb)
    slot3 = slot_sorted.reshape(n_blocks, 1, rb)
    w_map = lambda v, vb, ve, vlo, vhi: (ve[v], 0, 0)
    idx_block = (1, 1, rb)
    grid_spec = pltpu.PrefetchScalarGridSpec(
        num_scalar_prefetch=4,
        grid=(n_visits,),
        in_specs=[
            pl.BlockSpec(idx_block, lambda v, vb, ve, vlo, vhi: (0, 0, 0), memory_space=pltpu.SMEM),
            pl.BlockSpec(idx_block, lambda v, vb, ve, vlo, vhi: (jnp.minimum(vb[v] + 1, n_blocks - 1), 0, 0),
                         memory_space=pltpu.SMEM),
            pl.BlockSpec(idx_block, lambda v, vb, ve, vlo, vhi: (vb[v], 0, 0), memory_space=pltpu.SMEM),
            pl.BlockSpec(idx_block, lambda v, vb, ve, vlo, vhi: (jnp.maximum(vb[v] - 1, 0), 0, 0),
                         memory_space=pltpu.SMEM),
            pl.BlockSpec(memory_space=pl.ANY),
            pl.BlockSpec((None, D_MODEL, D_EXPERT), w_map),
            pl.BlockSpec((None, D_MODEL, D_EXPERT), w_map),
            pl.BlockSpec((None, D_EXPERT, D_MODEL), w_map),
        ],
        out_specs=pl.BlockSpec(memory_space=pl.ANY),
        scratch_shapes=[
            pltpu.VMEM((rb, 1, D_MODEL), F32),
            pltpu.VMEM((rb, D_MODEL), F32),
            pltpu.VMEM((rb, D_MODEL), BF16),
            pltpu.VMEM((rb, D_MODEL), F32),
            pltpu.VMEM((2, rb, 1, D_MODEL), F32),
            pltpu.VMEM((D_MODEL, D_EXPERT), BF16),
            pltpu.VMEM((D_MODEL, D_EXPERT), BF16),
            pltpu.VMEM((D_EXPERT, D_MODEL), BF16),
            pltpu.SMEM((3,), jnp.int32),
            pltpu.SemaphoreType.DMA(()),
            pltpu.SemaphoreType.DMA((2,)),
        ],
    )
    return pl.pallas_call(
        _expert_kernel,
        grid_spec=grid_spec,
        out_shape=jax.ShapeDtypeStruct((n_assign, 1, D_MODEL), F32),
        compiler_params=pltpu.CompilerParams(
            dimension_semantics=("arbitrary",), vmem_limit_bytes=VMEM_LIMIT),
        name="experts",
    )(*plan, tok3, tok3, slot3, slot3, x1rows, w_gate, w_up, w_down)


def _final_kernel(x_ref, y0_ref, y1_ref, w_ref, g_ref, b_ref, o_ref, xs_ref, ya_ref, yb_ref):
    tm = o_ref.shape[0]
    xs_ref[...] = x_ref[...].reshape(tm, D_MODEL)
    ya_ref[...] = y0_ref[...].reshape(tm, D_MODEL)
    yb_ref[...] = y1_ref[...].reshape(tm, D_MODEL)
    w = w_ref[...]
    moe = ya_ref[...] * w[:, 0:1] + yb_ref[...] * w[:, 1:2]
    o_ref[...] = _layer_norm_rows(ALPHA * xs_ref[...] + moe, g_ref[...], b_ref[...])


def _final_call(x1rows, y2, w_tok, g, b):
    n = x1rows.shape[0]
    tm = FINAL_TM
    fixed = lambda i: (0, 0)
    return pl.pallas_call(
        _final_kernel,
        grid=(n // tm,),
        in_specs=[
            pl.BlockSpec((tm, 1, D_MODEL), lambda i: (i, 0, 0)),
            pl.BlockSpec((None, tm, 1, D_MODEL), lambda i: (0, i, 0, 0)),
            pl.BlockSpec((None, tm, 1, D_MODEL), lambda i: (1, i, 0, 0)),
            pl.BlockSpec((tm, TOP_K), lambda i: (i, 0)),
            pl.BlockSpec((1, D_MODEL), fixed),
            pl.BlockSpec((1, D_MODEL), fixed),
        ],
        out_specs=pl.BlockSpec((tm, D_MODEL), lambda i: (i, 0)),
        out_shape=jax.ShapeDtypeStruct((n, D_MODEL), F32),
        scratch_shapes=[pltpu.VMEM((tm, D_MODEL), F32)] * 3,
        compiler_params=pltpu.CompilerParams(
            dimension_semantics=("arbitrary",), vmem_limit_bytes=VMEM_LIMIT),
        name="combine_ln",
    )(x1rows, y2, y2, w_tok, g, b)


def _t5_bucket(rel):
    half = N_BUCKETS // 2
    exact = half // 2
    n = jnp.abs(rel)
    nf = jnp.maximum(n, 1).astype(F32)
    large = exact + (jnp.log(nf / exact) / math.log(MAX_DISTANCE / exact) * (half - exact)).astype(jnp.int32)
    large = jnp.minimum(large, half - 1)
    return jnp.where(rel > 0, half, 0) + jnp.where(n < exact, n, large)


def _window_bias(rel_bias):
    rel = jnp.arange(3 * BLOCK)[None, :] - BLOCK - jnp.arange(BLOCK)[:, None]
    onehot = (_t5_bucket(rel)[..., None] == jnp.arange(N_BUCKETS)).astype(F32)
    bias = jnp.einsum("qkb,bh->hqk", onehot, rel_bias.astype(F32), precision=lax.Precision.HIGHEST)
    return jnp.where((jnp.abs(rel) <= WINDOW)[None], bias, NEG_BIG)


def _rope_tables(seq_len):
    rows = seq_len // GRID_W
    row_ids = jnp.repeat(jnp.arange(rows), GRID_W).astype(F32)
    col_ids = jnp.tile(jnp.arange(GRID_W), rows).astype(F32)
    half = HEAD_DIM // 2
    inv = 1.0 / (ROPE_THETA ** (jnp.arange(0, half, 2, dtype=F32) / half))
    ang = jnp.concatenate([row_ids[:, None] * inv, col_ids[:, None] * inv], -1)
    cos, sin = jnp.cos(ang), jnp.sin(ang)
    return jnp.concatenate([cos, cos], -1), jnp.concatenate([-sin, sin], -1)


def _deinterleave_cols(w, n_heads):
    d = w.shape[0]
    return w.reshape(d, n_heads, HEAD_DIM // 2, 2).transpose(0, 1, 3, 2).reshape(d, n_heads * HEAD_DIM)


def _deinterleave_gain(g):
    return g.reshape(HEAD_DIM // 2, 2).T.reshape(1, HEAD_DIM)


def _expert_row_block(n_assign):
    return MOE_RB if n_assign // N_EXPERTS >= 8 * MOE_RB else MOE_RB // 2


def _trunk(x, p):
    bsz, seq_len, d = x.shape
    n = bsz * seq_len
    n_assign = n * TOP_K
    x2 = x.reshape(n, d)
    cos_t, sin_t = _rope_tables(seq_len)
    xn, qa, ka, va, qb, kb, vb = _qkv_call(x2, p["emb_g"], p["emb_b"], p["w_qkv"], cos_t, sin_t,
                                           p["q_gain"], p["k_gain"], seq_len)
    oa = _attn_a_call(p["sink"], qa, ka, va, p["bias_a"], bsz, seq_len)
    ob = _attn_b_call(qb, kb, vb, bsz, seq_len)
    x1rows, lt = _merge_call(xn, oa, ob, p["w_a"], p["w_b"], p["w_g"], p["b_g"], p["w_o"],
                             p["ln1_g"], p["ln1_b"], p["wr_hi"], p["wr_lo"], p["b_r"])
    e_rows, w_rows, cnt = _route_call(lt)
    e_flat = e_rows[:TOP_K].reshape(n_assign)
    _, slot_sorted = lax.sort((e_flat, jnp.arange(n_assign, dtype=jnp.int32)), num_keys=1)
    tok_sorted = jnp.where(slot_sorted >= n, slot_sorted - n, slot_sorted)
    ends = jnp.cumsum(cnt[:, 0].astype(jnp.int32))
    starts = jnp.concatenate([jnp.zeros((1,), jnp.int32), ends])
    rb = _expert_row_block(n_assign)
    plan = _plan_call(starts, n_assign, rb)
    y2 = _expert_call(plan, tok_sorted, slot_sorted, x1rows, p["w_gate"], p["w_up"], p["w_down"], rb)
    y2 = y2.reshape(TOP_K, n, 1, d)
    out = _final_call(x1rows, y2, w_rows[:TOP_K].T, p["ln2_g"], p["ln2_b"])
    return out.reshape(bsz, seq_len, d)


def kernel(x_prompt, x_sample, emb_ln_g, emb_ln_b, rel_bias, w_in, b_gate, sink_a, q_norm_g, k_norm_g,
           w_branch_a, w_branch_b, w_out, ln1_g, ln1_b, w_coarse, b_coarse, w_fine, b_fine,
           w_gate, w_up, w_down, ln2_g, ln2_b):
    l = 0
    w_in_l = w_in[l]
    qa_end = Q_COLS
    kva_end = qa_end + 2 * KV_COLS
    qb_end = kva_end + Q_COLS
    kb_end = qb_end + KV_COLS
    vb_end = kb_end + KV_COLS
    w_qkv = jnp.concatenate([
        w_in_l[:, :kva_end],
        _deinterleave_cols(w_in_l[:, kva_end:qb_end], N_HEADS),
        _deinterleave_cols(w_in_l[:, qb_end:kb_end], N_KV),
        w_in_l[:, kb_end:vb_end],
    ], axis=1).astype(BF16)
    w_router = jnp.zeros((ROUTER_ROWS, D_MODEL), F32)
    w_router = w_router.at[0:N_GROUPS].set(w_coarse[l].T)
    w_router = w_router.at[FINE_ROW0:FINE_ROW0 + N_EXPERTS].set(w_fine[l].T)
    wr_hi = w_router.astype(BF16)
    wr_lo = (w_router - wr_hi.astype(F32)).astype(BF16)
    b_router = jnp.zeros((ROUTER_ROWS, 1), F32)
    b_router = b_router.at[0:N_GROUPS, 0].set(b_coarse[l].astype(F32))
    b_router = b_router.at[FINE_ROW0:FINE_ROW0 + N_EXPERTS, 0].set(b_fine[l].astype(F32))
    p = {
        "emb_g": emb_ln_g.reshape(1, D_MODEL), "emb_b": emb_ln_b.reshape(1, D_MODEL),
        "w_qkv": w_qkv,
        "q_gain": _deinterleave_gain(q_norm_g[l]), "k_gain": _deinterleave_gain(k_norm_g[l]),
        "sink": sink_a[l].astype(F32), "bias_a": _window_bias(rel_bias),
        "w_a": w_branch_a[l].astype(BF16), "w_b": w_branch_b[l].astype(BF16),
        "w_g": w_in_l[:, vb_end:].astype(BF16), "b_g": b_gate[l].reshape(1, 2 * D_MODEL),
        "w_o": w_out[l].astype(BF16),
        "ln1_g": ln1_g[l].reshape(1, D_MODEL), "ln1_b": ln1_b[l].reshape(1, D_MODEL),
        "wr_hi": wr_hi, "wr_lo": wr_lo, "b_r": b_router,
        "w_gate": w_gate[l], "w_up": w_up[l], "w_down": w_down[l],
        "ln2_g": ln2_g[l].reshape(1, D_MODEL), "ln2_b": ln2_b[l].reshape(1, D_MODEL),
    }
    return _trunk(x_prompt, p), _trunk(x_sample, p)
```

```python
import functools
import math

import numpy as np
import jax
import jax.numpy as jnp
from jax import lax
from jax.experimental import pallas as pl
from jax.experimental.pallas import tpu as pltpu

F32 = jnp.float32
BF16 = jnp.bfloat16

D_MODEL = 2048
HEAD_DIM = 128
N_HEADS = 8
N_KV = 2
GROUP = N_HEADS // N_KV
Q_COLS = N_HEADS * HEAD_DIM
KV_COLS = N_KV * HEAD_DIM
QKV_COLS = 2 * (Q_COLS + 2 * KV_COLS)
WINDOW = 128
BLOCK = 128
GRID_W = 64
ROPE_THETA = 10000.0
N_BUCKETS = 32
MAX_DISTANCE = 128
N_GROUPS = 4
EXPERTS_PER_GROUP = 8
N_EXPERTS = N_GROUPS * EXPERTS_PER_GROUP
TOP_K = 2
D_EXPERT = 512
LN_EPS = 1e-5
RMS_EPS = 1e-6
DEPTH = 1
ALPHA = (2 * DEPTH) ** 0.25
SCALE = HEAD_DIM ** -0.5
LOG2E = math.log2(math.e)
NEG_BIG = -1e30

LANES = 128
VMEM_LIMIT = 56 * 1024 * 1024

QKV_TM = 512
ATT_A_TQ = 512
ATT_B_TQ = 256
ATT_B_TK = 512
MERGE_TM = 256
MERGE_TC = 1024
ROUTE_TN = 1024
MOE_RB = 256
FINAL_TM = 512
ROUTER_ROWS = 128
FINE_ROW0 = 8


def _layer_norm_rows(x, g, b):
    mu = jnp.mean(x, axis=-1, keepdims=True)
    xc = x - mu
    var = jnp.mean(xc * xc, axis=-1, keepdims=True)
    return xc * lax.rsqrt(var + LN_EPS) * g + b


def _dot(a, b):
    return jnp.dot(a, b, preferred_element_type=F32)


def _dot_nt(a, b):
    return lax.dot_general(a, b, (((1,), (1,)), ((), ())), preferred_element_type=F32)


def _qkv_kernel(x_ref, g_ref, b_ref, w_ref, cos_ref, sin_ref, qg_ref, kg_ref,
                xn_ref, qa_ref, ka_ref, va_ref, qb_ref, kb_ref, vb_ref):
    xn = _layer_norm_rows(x_ref[...], g_ref[...], b_ref[...])
    xn_ref[...] = xn
    xb = xn.astype(BF16)
    cos = cos_ref[...]
    sin = sin_ref[...]

    def proj(c0, width):
        return _dot(xb, w_ref[:, c0:c0 + width])

    def norm_rope(h, gain):
        y = h * lax.rsqrt(jnp.mean(h * h, axis=-1, keepdims=True) + RMS_EPS) * gain
        return y * cos + pltpu.roll(y, HEAD_DIM // 2, 1) * sin

    c = Q_COLS + 2 * KV_COLS
    qg = qg_ref[...]
    for hp in range(N_HEADS // 2):
        hq2 = proj(c + 2 * hp * HEAD_DIM, 2 * HEAD_DIM)
        for h in (2 * hp, 2 * hp + 1):
            hq = hq2[:, (h - 2 * hp) * HEAD_DIM:(h - 2 * hp + 1) * HEAD_DIM]
            qb_ref[:, h * HEAD_DIM:(h + 1) * HEAD_DIM] = (norm_rope(hq, qg) * (SCALE * LOG2E)).astype(BF16)
    c += Q_COLS
    kg = kg_ref[...]
    hk2 = proj(c, KV_COLS)
    for h in range(N_KV):
        kb_ref[:, h * HEAD_DIM:(h + 1) * HEAD_DIM] = norm_rope(hk2[:, h * HEAD_DIM:(h + 1) * HEAD_DIM], kg).astype(BF16)
    c += KV_COLS
    hv2 = proj(c, KV_COLS)
    ones = jnp.ones((x_ref.shape[0], HEAD_DIM), BF16)
    for h in range(N_KV):
        vb_ref[:, 2 * h * HEAD_DIM:(2 * h + 1) * HEAD_DIM] = hv2[:, h * HEAD_DIM:(h + 1) * HEAD_DIM].astype(BF16)
        vb_ref[:, (2 * h + 1) * HEAD_DIM:(2 * h + 2) * HEAD_DIM] = ones
    c = 0
    qa_ref[...] = (proj(c, Q_COLS) * SCALE).astype(BF16)
    c += Q_COLS
    ka_ref[...] = proj(c, KV_COLS).astype(BF16)
    c += KV_COLS
    va_ref[...] = proj(c, KV_COLS).astype(BF16)


def _qkv_call(x2, g, b, w_qkv, cos_t, sin_t, qg, kg, seq_len):
    n = x2.shape[0]
    tm = QKV_TM
    pos_blocks = seq_len // tm
    row = lambda i: (i, 0)
    fixed = lambda i: (0, 0)
    pos = lambda i: (i % pos_blocks, 0)
    out_shapes = (
        jax.ShapeDtypeStruct((n, D_MODEL), F32),
        jax.ShapeDtypeStruct((n, Q_COLS), BF16),
        jax.ShapeDtypeStruct((n, KV_COLS), BF16),
        jax.ShapeDtypeStruct((n, KV_COLS), BF16),
        jax.ShapeDtypeStruct((n, Q_COLS), BF16),
        jax.ShapeDtypeStruct((n, KV_COLS), BF16),
        jax.ShapeDtypeStruct((n, 2 * KV_COLS), BF16),
    )
    return pl.pallas_call(
        _qkv_kernel,
        grid=(n // tm,),
        in_specs=[
            pl.BlockSpec((tm, D_MODEL), row),
            pl.BlockSpec((1, D_MODEL), fixed),
            pl.BlockSpec((1, D_MODEL), fixed),
            pl.BlockSpec((D_MODEL, QKV_COLS), fixed, pipeline_mode=pl.Buffered(1)),
            pl.BlockSpec((tm, HEAD_DIM), pos),
            pl.BlockSpec((tm, HEAD_DIM), pos),
            pl.BlockSpec((1, HEAD_DIM), fixed),
            pl.BlockSpec((1, HEAD_DIM), fixed),
        ],
        out_specs=(
            pl.BlockSpec((tm, D_MODEL), row),
            pl.BlockSpec((tm, Q_COLS), row),
            pl.BlockSpec((tm, KV_COLS), row),
            pl.BlockSpec((tm, KV_COLS), row),
            pl.BlockSpec((tm, Q_COLS), row),
            pl.BlockSpec((tm, KV_COLS), row),
            pl.BlockSpec((tm, 2 * KV_COLS), row),
        ),
        out_shape=out_shapes,
        compiler_params=pltpu.CompilerParams(
            dimension_semantics=("arbitrary",), vmem_limit_bytes=VMEM_LIMIT),
        name="ln_qkv",
    )(x2, g, b, w_qkv, cos_t, sin_t, qg, kg)


def _attn_a_kernel(sink_ref, q_ref, kc_ref, kp_ref, kn_ref, vc_ref, vp_ref, vn_ref, bias_ref,
                   o_ref, kcat_ref, vcat_ref, sa_ref, sb_ref, *, seq_len):
    tq = ATT_A_TQ
    i = pl.program_id(1)
    kcat_ref[0:BLOCK, :] = kp_ref[...]
    kcat_ref[BLOCK:BLOCK + tq, :] = kc_ref[...]
    kcat_ref[BLOCK + tq:, :] = kn_ref[...]
    vcat_ref[0:BLOCK, :] = vp_ref[...]
    vcat_ref[BLOCK:BLOCK + tq, :] = vc_ref[...]
    vcat_ref[BLOCK + tq:, :] = vn_ref[...]
    col = lax.broadcasted_iota(jnp.int32, (1, 3 * BLOCK), 1)
    s_refs = (sa_ref, sb_ref)

    def block_scores(j, s_ref):
        qj = q_ref[j * BLOCK:(j + 1) * BLOCK, :]
        for g in range(N_KV):
            qs = jnp.concatenate(
                [qj[:, h * HEAD_DIM:(h + 1) * HEAD_DIM] for h in range(g * GROUP, (g + 1) * GROUP)], axis=0)
            kw = kcat_ref[j * BLOCK:(j + 3) * BLOCK, g * HEAD_DIM:(g + 1) * HEAD_DIM]
            s_ref[g * GROUP * BLOCK:(g + 1) * GROUP * BLOCK, :] = _dot_nt(qs, kw)

    def block_out(j, s_ref):
        key_pos = i * tq + (j - 1) * BLOCK + col
        inside = (key_pos >= 0) & (key_pos < seq_len)
        for g in range(N_KV):
            probs = []
            for h in range(g * GROUP, (g + 1) * GROUP):
                s = s_ref[h * BLOCK:(h + 1) * BLOCK, :]
                s = jnp.where(inside, s + bias_ref[h], NEG_BIG)
                sk = sink_ref[h]
                m = jnp.maximum(jnp.max(s, axis=-1, keepdims=True), sk)
                p = jnp.exp(s - m)
                denom = jnp.sum(p, axis=-1, keepdims=True) + jnp.exp(sk - m)
                probs.append((p / denom).astype(BF16))
            vw = vcat_ref[j * BLOCK:(j + 3) * BLOCK, g * HEAD_DIM:(g + 1) * HEAD_DIM]
            o = _dot(jnp.concatenate(probs, axis=0), vw)
            for hh in range(GROUP):
                h = g * GROUP + hh
                o_ref[j * BLOCK:(j + 1) * BLOCK, h * HEAD_DIM:(h + 1) * HEAD_DIM] = (
                    o[hh * BLOCK:(hh + 1) * BLOCK, :].astype(BF16))

    n_blk = tq // BLOCK
    block_scores(0, s_refs[0])
    for j in range(n_blk):
        if j + 1 < n_blk:
            block_scores(j + 1, s_refs[(j + 1) % 2])
        block_out(j, s_refs[j % 2])


def _attn_a_call(sink, qa, ka, va, bias, bsz, seq_len):
    tq = ATT_A_TQ
    nb = seq_len // BLOCK
    bpt = tq // BLOCK
    q3 = qa.reshape(bsz, seq_len, Q_COLS)
    k3 = ka.reshape(bsz, seq_len, KV_COLS)
    v3 = va.reshape(bsz, seq_len, KV_COLS)
    cur = lambda b, i: (b, i, 0)
    prev = lambda b, i: (b, jnp.maximum(i * bpt - 1, 0), 0)
    nxt = lambda b, i: (b, jnp.minimum((i + 1) * bpt, nb - 1), 0)
    kv_cur = pl.BlockSpec((None, tq, KV_COLS), cur)
    kv_prev = pl.BlockSpec((None, BLOCK, KV_COLS), prev)
    kv_next = pl.BlockSpec((None, BLOCK, KV_COLS), nxt)
    out = pl.pallas_call(
        functools.partial(_attn_a_kernel, seq_len=seq_len),
        grid=(bsz, seq_len // tq),
        in_specs=[
            pl.BlockSpec(memory_space=pltpu.SMEM),
            pl.BlockSpec((None, tq, Q_COLS), cur),
            kv_cur, kv_prev, kv_next,
            kv_cur, kv_prev, kv_next,
            pl.BlockSpec((N_HEADS, BLOCK, 3 * BLOCK), lambda b, i: (0, 0, 0)),
        ],
        out_specs=pl.BlockSpec((None, tq, Q_COLS), cur),
        out_shape=jax.ShapeDtypeStruct((bsz, seq_len, Q_COLS), BF16),
        scratch_shapes=[
            pltpu.VMEM((tq + 2 * BLOCK, KV_COLS), BF16),
            pltpu.VMEM((tq + 2 * BLOCK, KV_COLS), BF16),
            pltpu.VMEM((N_HEADS * BLOCK, 3 * BLOCK), F32),
            pltpu.VMEM((N_HEADS * BLOCK, 3 * BLOCK), F32),
        ],
        compiler_params=pltpu.CompilerParams(
            dimension_semantics=("arbitrary", "arbitrary"), vmem_limit_bytes=VMEM_LIMIT),
        name="attn_window",
    )(sink, q3, k3, k3, k3, v3, v3, v3, bias)
    return out.reshape(bsz * seq_len, Q_COLS)


def _attn_b_kernel(q_ref, qn_ref, k_ref, v_ref, o_ref, qs_ref, qsn_ref, m_ref, acc_ref,
                   s0_ref, s1_ref, p0_ref, p1_ref, a0_ref, a1_ref, *, seq_len):
    tq = ATT_B_TQ
    tk = ATT_B_TK
    n_chunks = seq_len // tk
    rows = GROUP * tq
    for h in range(GROUP):
        qs_ref[h * tq:(h + 1) * tq, :] = q_ref[:, h * HEAD_DIM:(h + 1) * HEAD_DIM]
        qsn_ref[h * tq:(h + 1) * tq, :] = qn_ref[:, h * HEAD_DIM:(h + 1) * HEAD_DIM]

    def scores(q_src, c, s_ref):
        k0 = pl.multiple_of(c * tk, tk)
        s_ref[...] = _dot_nt(q_src[...], k_ref[pl.ds(k0, tk), :])

    def softmax(s_ref, p_ref, a_ref, first=False):
        m_tile = s_ref[:, 0:LANES]
        for j in range(1, tk // LANES):
            m_tile = jnp.maximum(m_tile, s_ref[:, j * LANES:(j + 1) * LANES])
        m_cur = jnp.broadcast_to(jnp.max(m_tile, axis=-1, keepdims=True), (rows, LANES))
        if first:
            m_new = m_cur
            a_ref[...] = jnp.zeros((rows, LANES), F32)
        else:
            m_prev = m_ref[...]
            m_new = jnp.maximum(m_prev, m_cur)
            a_ref[...] = jnp.exp2(m_prev - m_new)
        m_ref[...] = m_new
        for j in range(tk // LANES):
            p_ref[:, j * LANES:(j + 1) * LANES] = jnp.exp2(s_ref[:, j * LANES:(j + 1) * LANES] - m_new).astype(BF16)

    def weighted_values(c, p_ref, a_ref):
        k0 = pl.multiple_of(c * tk, tk)
        pv = _dot(p_ref[...], v_ref[pl.ds(k0, tk), :])
        a = a_ref[...]
        acc_ref[:, 0:HEAD_DIM] = a * acc_ref[:, 0:HEAD_DIM] + pv[:, 0:HEAD_DIM]
        acc_ref[:, HEAD_DIM:] = a * acc_ref[:, HEAD_DIM:] + pv[:, HEAD_DIM:]

    @pl.when(pl.program_id(2) == 0)
    def _():
        scores(qs_ref, 0, s0_ref)
        scores(qs_ref, 1, s1_ref)
        softmax(s0_ref, p0_ref, a0_ref, first=True)

    acc_ref[...] = jnp.zeros(acc_ref.shape, F32)

    def body(t, carry):
        c = 2 * t + 1
        scores(qs_ref, c + 1, s0_ref)
        softmax(s1_ref, p1_ref, a1_ref)
        weighted_values(c - 1, p0_ref, a0_ref)
        scores(qs_ref, c + 2, s1_ref)
        softmax(s0_ref, p0_ref, a0_ref)
        weighted_values(c, p1_ref, a1_ref)
        return carry

    lax.fori_loop(0, (n_chunks - 2) // 2, body, 0, unroll=True)
    scores(qsn_ref, 0, s0_ref)
    softmax(s1_ref, p1_ref, a1_ref)
    weighted_values(n_chunks - 2, p0_ref, a0_ref)
    scores(qsn_ref, 1, s1_ref)
    softmax(s0_ref, p0_ref, a0_ref, first=True)
    weighted_values(n_chunks - 1, p1_ref, a1_ref)
    o = acc_ref[:, 0:HEAD_DIM] / acc_ref[:, HEAD_DIM:]
    for h in range(GROUP):
        o_ref[:, h * HEAD_DIM:(h + 1) * HEAD_DIM] = o[h * tq:(h + 1) * tq, :].astype(BF16)


def _attn_b_call(qb, kb, vb, bsz, seq_len):
    tq = ATT_B_TQ
    tk = ATT_B_TK
    assert seq_len % tk == 0 and (seq_len // tk) % 2 == 0, "dense mixer pipeline needs an even number of key chunks"
    q3 = qb.reshape(bsz, seq_len, Q_COLS)
    k3 = kb.reshape(bsz, seq_len, KV_COLS)
    v3 = vb.reshape(bsz, seq_len, 2 * KV_COLS)
    gcols = GROUP * HEAD_DIM
    n_q = seq_len // tq
    rows = GROUP * tq
    out = pl.pallas_call(
        functools.partial(_attn_b_kernel, seq_len=seq_len),
        grid=(bsz, N_KV, n_q),
        in_specs=[
            pl.BlockSpec((None, tq, gcols), lambda b, g, i: (b, i, g)),
            pl.BlockSpec((None, tq, gcols), lambda b, g, i: (b, jnp.minimum(i + 1, n_q - 1), g)),
            pl.BlockSpec((None, seq_len, HEAD_DIM), lambda b, g, i: (b, 0, g)),
            pl.BlockSpec((None, seq_len, 2 * HEAD_DIM), lambda b, g, i: (b, 0, g)),
        ],
        out_specs=pl.BlockSpec((None, tq, gcols), lambda b, g, i: (b, i, g)),
        out_shape=jax.ShapeDtypeStruct((bsz, seq_len, Q_COLS), BF16),
        scratch_shapes=[
            pltpu.VMEM((rows, HEAD_DIM), BF16),
            pltpu.VMEM((rows, HEAD_DIM), BF16),
            pltpu.VMEM((rows, LANES), F32),
            pltpu.VMEM((rows, 2 * HEAD_DIM), F32),
            pltpu.VMEM((rows, tk), F32),
            pltpu.VMEM((rows, tk), F32),
            pltpu.VMEM((rows, tk), BF16),
            pltpu.VMEM((rows, tk), BF16),
            pltpu.VMEM((rows, LANES), F32),
            pltpu.VMEM((rows, LANES), F32),
        ],
        compiler_params=pltpu.CompilerParams(
            dimension_semantics=("arbitrary", "arbitrary", "arbitrary"), vmem_limit_bytes=VMEM_LIMIT),
        name="attn_dense",
    )(q3, q3, k3, v3)
    return out.reshape(bsz * seq_len, Q_COLS)


def _merge_kernel(xn0_ref, oa0_ref, ob0_ref, xn1_ref, oa1_ref, ob1_ref, *rest, tiles0):
    i = pl.program_id(0)

    @pl.when(i < tiles0)
    def _():
        _merge_tile(xn0_ref, oa0_ref, ob0_ref, *rest)

    @pl.when(i >= tiles0)
    def _():
        _merge_tile(xn1_ref, oa1_ref, ob1_ref, *rest)


def _merge_tile(xn_ref, oa_ref, ob_ref, wa_ref, wb_ref, wg_ref, bg_ref, wo_ref,
                g_ref, b_ref, wrh_ref, wrl_ref, br_ref, x1rows_ref, lt_ref, acc_ref):
    tc = MERGE_TC
    xb = xn_ref[...].astype(BF16)
    oa = oa_ref[...]
    ob = ob_ref[...]
    for c in range(D_MODEL // tc):
        lo, hi = c * tc, (c + 1) * tc
        ga = jax.nn.sigmoid(_dot(xb, wg_ref[:, lo:hi]) + bg_ref[:, lo:hi])
        gb = jax.nn.sigmoid(_dot(xb, wg_ref[:, D_MODEL + lo:D_MODEL + hi]) + bg_ref[:, D_MODEL + lo:D_MODEL + hi])
        mixed = (ga * _dot(oa, wa_ref[:, lo:hi]) + gb * _dot(ob, wb_ref[:, lo:hi])).astype(BF16)
        part = _dot(mixed, wo_ref[lo:hi, :])
        if c == 0:
            acc_ref[...] = part
        else:
            acc_ref[...] += part
    x1 = _layer_norm_rows(ALPHA * xn_ref[...] + acc_ref[...], g_ref[...], b_ref[...])
    x1rows_ref[...] = x1.reshape(x1.shape[0], 1, D_MODEL)
    xh = x1.astype(BF16)
    xl = (x1 - xh.astype(F32)).astype(BF16)
    wrh = wrh_ref[...]
    logits = _dot(xh, wrh) + (_dot(xl, wrh) + _dot(xh, wrl_ref[...]))
    lt_ref[...] = logits.T + br_ref[...]


def _merge_call(batch0, batch1, wa, wb, wg, bg, wo, g, b, wr_hi, wr_lo, br):
    tm = MERGE_TM
    tiles0 = batch0[0].shape[0] // tm
    n = batch0[0].shape[0] + batch1[0].shape[0]
    rows0 = lambda i: (jnp.minimum(i, tiles0 - 1), 0)
    rows1 = lambda i: (jnp.maximum(i - tiles0, 0), 0)
    fixed = lambda i: (0, 0)

    def resident(shape):
        return pl.BlockSpec(shape, fixed, pipeline_mode=pl.Buffered(1))

    return pl.pallas_call(
        functools.partial(_merge_kernel, tiles0=tiles0),
        grid=(n // tm,),
        in_specs=[
            pl.BlockSpec((tm, D_MODEL), rows0),
            pl.BlockSpec((tm, Q_COLS), rows0),
            pl.BlockSpec((tm, Q_COLS), rows0),
            pl.BlockSpec((tm, D_MODEL), rows1),
            pl.BlockSpec((tm, Q_COLS), rows1),
            pl.BlockSpec((tm, Q_COLS), rows1),
            resident((Q_COLS, D_MODEL)),
            resident((Q_COLS, D_MODEL)),
            resident((D_MODEL, 2 * D_MODEL)),
            resident((1, 2 * D_MODEL)),
            resident((D_MODEL, D_MODEL)),
            resident((1, D_MODEL)),
            resident((1, D_MODEL)),
            resident((D_MODEL, ROUTER_ROWS)),
            resident((D_MODEL, ROUTER_ROWS)),
            resident((ROUTER_ROWS, 1)),
        ],
        out_specs=(
            pl.BlockSpec((tm, 1, D_MODEL), lambda i: (i, 0, 0)),
            pl.BlockSpec((ROUTER_ROWS, tm), lambda i: (0, i)),
        ),
        out_shape=(
            jax.ShapeDtypeStruct((n, 1, D_MODEL), F32),
            jax.ShapeDtypeStruct((ROUTER_ROWS, n), F32),
        ),
        scratch_shapes=[pltpu.VMEM((tm, D_MODEL), F32)],
        compiler_params=pltpu.CompilerParams(
            dimension_semantics=("arbitrary",), vmem_limit_bytes=VMEM_LIMIT),
        name="merge_ln_router",
    )(*batch0, *batch1, wa, wb, wg, bg, wo, g, b, wr_hi, wr_lo, br)


def _route_kernel(lt_ref, e_ref, w_ref, cnt_ref, carry_ref):
    tn = lt_ref.shape[1]

    @pl.when(pl.program_id(0) == 0)
    def _():
        carry_ref[...] = jnp.zeros(carry_ref.shape, F32)

    cl = lt_ref[0:N_GROUPS, :]
    gi = lax.broadcasted_iota(jnp.int32, (N_GROUPS, tn), 0)
    cmax = jnp.max(cl, axis=0, keepdims=True)
    ce = jnp.exp(cl - cmax)
    cp = ce / jnp.sum(ce, axis=0, keepdims=True)
    g_idx = jnp.min(jnp.where(cl == cmax, gi, N_GROUPS), axis=0, keepdims=True)
    g_prob = jnp.sum(jnp.where(gi == g_idx, cp, 0.0), axis=0, keepdims=True)

    fl = jnp.zeros((EXPERTS_PER_GROUP, tn), F32)
    for g in range(N_GROUPS):
        r0 = FINE_ROW0 + g * EXPERTS_PER_GROUP
        fl = jnp.where(g_idx == g, lt_ref[r0:r0 + EXPERTS_PER_GROUP, :], fl)
    fmax = jnp.max(fl, axis=0, keepdims=True)
    fe = jnp.exp(fl - fmax)
    fp = fe / jnp.sum(fe, axis=0, keepdims=True)
    ei = lax.broadcasted_iota(jnp.int32, (EXPERTS_PER_GROUP, tn), 0)
    p1 = jnp.max(fp, axis=0, keepdims=True)
    i1 = jnp.min(jnp.where(fp == p1, ei, EXPERTS_PER_GROUP), axis=0, keepdims=True)
    fp2 = jnp.where(ei == i1, -1.0, fp)
    p2 = jnp.max(fp2, axis=0, keepdims=True)
    i2 = jnp.min(jnp.where(fp2 == p2, ei, EXPERTS_PER_GROUP), axis=0, keepdims=True)
    tot = p1 + p2
    w1 = g_prob * p1 / tot
    w2 = g_prob * p2 / tot
    e1 = g_idx * EXPERTS_PER_GROUP + i1
    e2 = g_idx * EXPERTS_PER_GROUP + i2
    ri = lax.broadcasted_iota(jnp.int32, (8, tn), 0)
    e_ref[...] = jnp.where(ri == 0, e1, jnp.where(ri == 1, e2, 0))
    w_ref[...] = jnp.where(ri == 0, w1, jnp.where(ri == 1, w2, 0.0))

    xi = lax.broadcasted_iota(jnp.int32, (N_EXPERTS, tn), 0)
    hits = jnp.where(xi == e1, 1.0, 0.0) + jnp.where(xi == e2, 1.0, 0.0)
    total = carry_ref[...] + jnp.sum(hits, axis=1, keepdims=True)
    carry_ref[...] = total
    cnt_ref[...] = jnp.broadcast_to(total, cnt_ref.shape)


def _route_call(lt):
    n = lt.shape[1]
    tn = min(ROUTE_TN, n)
    tok = lambda i: (0, i)
    return pl.pallas_call(
        _route_kernel,
        grid=(n // tn,),
        in_specs=[pl.BlockSpec((ROUTER_ROWS, tn), tok)],
        out_specs=(pl.BlockSpec((8, tn), tok), pl.BlockSpec((8, tn), tok),
                   pl.BlockSpec((N_EXPERTS, LANES), lambda i: (0, 0))),
        out_shape=(jax.ShapeDtypeStruct((8, n), jnp.int32), jax.ShapeDtypeStruct((8, n), F32),
                   jax.ShapeDtypeStruct((N_EXPERTS, LANES), F32)),
        scratch_shapes=[pltpu.VMEM((N_EXPERTS, 1), F32)],
        compiler_params=pltpu.CompilerParams(dimension_semantics=("arbitrary",)),
        name="router",
    )(lt)


def _plan_kernel(starts_ref, vb_ref, ve_ref, vlo_ref, vhi_ref, *, rb):
    n_visits = vb_ref.shape[0]
    shift = rb.bit_length() - 1

    def expert_body(e, v):
        end = starts_ref[e + 1]

        def cond(state):
            return state[0] < end

        def body(state):
            r, v = state
            b = lax.shift_right_logical(r, shift)
            hi = jnp.minimum(end, (b + 1) * rb)
            vb_ref[v] = b
            ve_ref[v] = e
            vlo_ref[v] = r - b * rb
            vhi_ref[v] = hi - b * rb
            return hi, v + 1

        return lax.while_loop(cond, body, (starts_ref[e], v))[1]

    used = lax.fori_loop(0, N_EXPERTS, expert_body, 0)
    last_b = vb_ref[used - 1]
    last_e = ve_ref[used - 1]

    def pad(i, carry):
        vb_ref[i] = last_b
        ve_ref[i] = last_e
        vlo_ref[i] = 0
        vhi_ref[i] = 0
        return carry

    lax.fori_loop(used, n_visits, pad, 0)


def _plan_call(starts, n_assign, rb):
    assert rb & (rb - 1) == 0 and n_assign % rb == 0
    n_visits = n_assign // rb + N_EXPERTS - 1
    smem = pl.BlockSpec(memory_space=pltpu.SMEM)
    sds = jax.ShapeDtypeStruct((n_visits,), jnp.int32)
    return pl.pallas_call(
        functools.partial(_plan_kernel, rb=rb),
        in_specs=[smem], out_specs=(smem,) * 4, out_shape=(sds,) * 4, name="visit_plan",
    )(starts)


def _expert_kernel(vb_ref, ve_ref, vlo_ref, vhi_ref,
                   tok0_ref, tokn_ref, slot_ref, slotp_ref, x_hbm, wg_ref, wu_ref, wd_ref, y_hbm,
                   xin_ref, xf_ref, xb_ref, ys_ref, yout_ref, wgb_ref, wub_ref, wdb_ref,
                   state_ref, gsem, ssem):
    rb = xb_ref.shape[0]
    v = pl.program_id(0)
    b = vb_ref[v]
    lo = vlo_ref[v]
    hi = vhi_ref[v]
    unsent, in_flight = 0, 1

    def gather_row(idx_ref, r):
        return pltpu.make_async_copy(x_hbm.at[pl.ds(idx_ref[0, 0, r], 1)], xin_ref.at[pl.ds(r, 1)], gsem)

    def scatter_row(buf, idx_ref, r):
        return pltpu.make_async_copy(yout_ref.at[buf, pl.ds(r, 1)], y_hbm.at[pl.ds(idx_ref[0, 0, r], 1)], ssem.at[buf])

    def start_gather(idx_ref):
        def body(r, carry):
            gather_row(idx_ref, r).start()
            return carry
        lax.fori_loop(0, rb, body, 0, unroll=8)

    def wait_gather():
        def body(r, carry):
            gather_row(tok0_ref, r).wait()
            return carry
        lax.fori_loop(0, rb, body, 0, unroll=8)

    def wait_scatter(buf):
        def body(r, carry):
            scatter_row(buf, slot_ref, r).wait()
            return carry
        lax.fori_loop(0, rb, body, 0, unroll=8)
        state_ref[in_flight + buf] = 0

    def park_block(get_rows):
        buf = b & 1

        @pl.when(state_ref[in_flight + buf] == 1)
        def _():
            wait_scatter(buf)

        yout_ref[buf] = get_rows().reshape(rb, 1, D_MODEL)
        state_ref[unsent] = 1

    def mark_sent(buf):
        state_ref[unsent] = 0
        state_ref[in_flight + buf] = 1

    @pl.when(v == 0)
    def _():
        state_ref[unsent] = 0
        state_ref[in_flight] = 0
        state_ref[in_flight + 1] = 0
        start_gather(tok0_ref)

    @pl.when(hi > lo)
    def _():
        @pl.when(lo == 0)
        def _():
            wait_gather()
            xf_ref[...] = xin_ref[...].reshape(rb, D_MODEL)
            xb_ref[...] = xf_ref[...].astype(BF16)

        @pl.when((v == 0) | (ve_ref[v] != ve_ref[jnp.maximum(v - 1, 0)]))
        def _():
            wgb_ref[...] = wg_ref[...].astype(BF16)
            wub_ref[...] = wu_ref[...].astype(BF16)
            wdb_ref[...] = wd_ref[...].astype(BF16)

        def expert_rows():
            x = xb_ref[...]
            hidden = jax.nn.silu(_dot(x, wgb_ref[...])) * _dot(x, wub_ref[...])
            return _dot(hidden.astype(BF16), wdb_ref[...])

        whole = (lo == 0) & (hi == rb)
        prev_buf = (b + 1) & 1

        @pl.when(whole & (b > 0))
        def _():
            for r in range(rb):
                gather_row(tokn_ref, r).start()
                scatter_row(prev_buf, slotp_ref, r).start()
            mark_sent(prev_buf)
            y = expert_rows()
            park_block(lambda: y)

        @pl.when(whole & (b == 0))
        def _():
            for r in range(rb):
                gather_row(tokn_ref, r).start()
            y = expert_rows()
            park_block(lambda: y)

        @pl.when(jnp.logical_not(whole))
        def _():
            @pl.when(lo == 0)
            def _():
                start_gather(tokn_ref)

                @pl.when(state_ref[unsent] == 1)
                def _():
                    def body(r, carry):
                        scatter_row(prev_buf, slotp_ref, r).start()
                        return carry
                    lax.fori_loop(0, rb, body, 0, unroll=8)
                    mark_sent(prev_buf)

            y = expert_rows()
            rows = lax.broadcasted_iota(jnp.int32, (rb, 1), 0)
            mine = (rows >= lo) & (rows < hi)

            @pl.when(lo == 0)
            def _():
                ys_ref[...] = jnp.where(mine, y, 0.0)

            @pl.when(lo > 0)
            def _():
                ys_ref[...] = jnp.where(mine, y, ys_ref[...])

            @pl.when(hi == rb)
            def _():
                park_block(lambda: ys_ref[...])

    @pl.when(v == pl.num_programs(0) - 1)
    def _():
        last_buf = b & 1

        @pl.when(state_ref[unsent] == 1)
        def _():
            def body(r, carry):
                scatter_row(last_buf, slot_ref, r).start()
                return carry
            lax.fori_loop(0, rb, body, 0, unroll=8)
            mark_sent(last_buf)

        for buf in range(2):
            @pl.when(state_ref[in_flight + buf] == 1)
            def _():
                wait_scatter(buf)

        wait_gather()


def _expert_call(plan, tok_sorted, slot_sorted, x1rows, w_gate, w_up, w_down, rb):
    n_assign = tok_sorted.shape[0]
    n_blocks = n_assign // rb
    n_visits = plan[0].shape[0]
    tok3 = tok_sorted.reshape(n_blocks, 1, rb)
    slot3 = slot_sorted.reshape(n_blocks, 1, rb)
    w_map = lambda v, vb, ve, vlo, vhi: (ve[v], 0, 0)
    idx_block = (1, 1, rb)
    grid_spec = pltpu.PrefetchScalarGridSpec(
        num_scalar_prefetch=4,
        grid=(n_visits,),
        in_specs=[
            pl.BlockSpec(idx_block, lambda v, vb, ve, vlo, vhi: (0, 0, 0), memory_space=pltpu.SMEM),
            pl.BlockSpec(idx_block, lambda v, vb, ve, vlo, vhi: (jnp.minimum(vb[v] + 1, n_blocks - 1), 0, 0),
                         memory_space=pltpu.SMEM),
            pl.BlockSpec(idx_block, lambda v, vb, ve, vlo, vhi: (vb[v], 0, 0), memory_space=pltpu.SMEM),
            pl.BlockSpec(idx_block, lambda v, vb, ve, vlo, vhi: (jnp.maximum(vb[v] - 1, 0), 0, 0),
                         memory_space=pltpu.SMEM),
            pl.BlockSpec(memory_space=pl.ANY),
            pl.BlockSpec((None, D_MODEL, D_EXPERT), w_map),
            pl.BlockSpec((None, D_MODEL, D_EXPERT), w_map),
            pl.BlockSpec((None, D_EXPERT, D_MODEL), w_map),
        ],
        out_specs=pl.BlockSpec(memory_space=pl.ANY),
        scratch_shapes=[
            pltpu.VMEM((rb, 1, D_MODEL), F32),
            pltpu.VMEM((rb, D_MODEL), F32),
            pltpu.VMEM((rb, D_MODEL), BF16),
            pltpu.VMEM((rb, D_MODEL), F32),
            pltpu.VMEM((2, rb, 1, D_MODEL), F32),
            pltpu.VMEM((D_MODEL, D_EXPERT), BF16),
            pltpu.VMEM((D_MODEL, D_EXPERT), BF16),
            pltpu.VMEM((D_EXPERT, D_MODEL), BF16),
            pltpu.SMEM((3,), jnp.int32),
            pltpu.SemaphoreType.DMA(()),
            pltpu.SemaphoreType.DMA((2,)),
        ],
    )
    return pl.pallas_call(
        _expert_kernel,
        grid_spec=grid_spec,
        out_shape=jax.ShapeDtypeStruct((n_assign, 1, D_MODEL), F32),
        compiler_params=pltpu.CompilerParams(
            dimension_semantics=("arbitrary",), vmem_limit_bytes=VMEM_LIMIT),
        name="experts",
    )(*plan, tok3, tok3, slot3, slot3, x1rows, w_gate, w_up, w_down)


def _final_kernel(x_ref, y0_ref, y1_ref, w_ref, g_ref, b_ref, o_ref, xs_ref, ya_ref, yb_ref):
    tm = o_ref.shape[0]
    xs_ref[...] = x_ref[...].reshape(tm, D_MODEL)
    ya_ref[...] = y0_ref[...].reshape(tm, D_MODEL)
    yb_ref[...] = y1_ref[...].reshape(tm, D_MODEL)
    w = w_ref[...]
    moe = ya_ref[...] * w[:, 0:1] + yb_ref[...] * w[:, 1:2]
    o_ref[...] = _layer_norm_rows(ALPHA * xs_ref[...] + moe, g_ref[...], b_ref[...])


def _final_call(x1rows, y2, w_tok, g, b, row0, n):
    tm = FINAL_TM
    assert row0 % tm == 0 and n % tm == 0
    t0 = row0 // tm
    fixed = lambda i: (0, 0)
    return pl.pallas_call(
        _final_kernel,
        grid=(n // tm,),
        in_specs=[
            pl.BlockSpec((tm, 1, D_MODEL), lambda i: (t0 + i, 0, 0)),
            pl.BlockSpec((None, tm, 1, D_MODEL), lambda i: (0, t0 + i, 0, 0)),
            pl.BlockSpec((None, tm, 1, D_MODEL), lambda i: (1, t0 + i, 0, 0)),
            pl.BlockSpec((tm, TOP_K), lambda i: (t0 + i, 0)),
            pl.BlockSpec((1, D_MODEL), fixed),
            pl.BlockSpec((1, D_MODEL), fixed),
        ],
        out_specs=pl.BlockSpec((tm, D_MODEL), lambda i: (i, 0)),
        out_shape=jax.ShapeDtypeStruct((n, D_MODEL), F32),
        scratch_shapes=[pltpu.VMEM((tm, D_MODEL), F32)] * 3,
        compiler_params=pltpu.CompilerParams(
            dimension_semantics=("arbitrary",), vmem_limit_bytes=VMEM_LIMIT),
        name="combine_ln",
    )(x1rows, y2, y2, w_tok, g, b)


def _t5_bucket(rel):
    half = N_BUCKETS // 2
    exact = half // 2
    n = jnp.abs(rel)
    nf = jnp.maximum(n, 1).astype(F32)
    large = exact + (jnp.log(nf / exact) / math.log(MAX_DISTANCE / exact) * (half - exact)).astype(jnp.int32)
    large = jnp.minimum(large, half - 1)
    return jnp.where(rel > 0, half, 0) + jnp.where(n < exact, n, large)


def _window_bias(rel_bias):
    rel = jnp.arange(3 * BLOCK)[None, :] - BLOCK - jnp.arange(BLOCK)[:, None]
    onehot = (_t5_bucket(rel)[..., None] == jnp.arange(N_BUCKETS)).astype(F32)
    bias = jnp.einsum("qkb,bh->hqk", onehot, rel_bias.astype(F32), precision=lax.Precision.HIGHEST)
    return jnp.where((jnp.abs(rel) <= WINDOW)[None], bias, NEG_BIG)


def _rope_tables(seq_len):
    rows = seq_len // GRID_W
    row_ids = jnp.repeat(jnp.arange(rows), GRID_W).astype(F32)
    col_ids = jnp.tile(jnp.arange(GRID_W), rows).astype(F32)
    half = HEAD_DIM // 2
    inv = 1.0 / (ROPE_THETA ** (jnp.arange(0, half, 2, dtype=F32) / half))
    ang = jnp.concatenate([row_ids[:, None] * inv, col_ids[:, None] * inv], -1)
    cos, sin = jnp.cos(ang), jnp.sin(ang)
    return jnp.concatenate([cos, cos], -1), jnp.concatenate([-sin, sin], -1)


def _deinterleave_cols(w, n_heads):
    d = w.shape[0]
    return w.reshape(d, n_heads, HEAD_DIM // 2, 2).transpose(0, 1, 3, 2).reshape(d, n_heads * HEAD_DIM)


def _deinterleave_gain(g):
    return g.reshape(HEAD_DIM // 2, 2).T.reshape(1, HEAD_DIM)


def _expert_row_block(n_assign):
    return MOE_RB if n_assign // N_EXPERTS >= 8 * MOE_RB else MOE_RB // 2


def _mixers(x, p):
    bsz, seq_len, d = x.shape
    cos_t, sin_t = _rope_tables(seq_len)
    xn, qa, ka, va, qb, kb, vb = _qkv_call(x.reshape(bsz * seq_len, d), p["emb_g"], p["emb_b"], p["w_qkv"],
                                           cos_t, sin_t, p["q_gain"], p["k_gain"], seq_len)
    oa = _attn_a_call(p["sink"], qa, ka, va, p["bias_a"], bsz, seq_len)
    ob = _attn_b_call(qb, kb, vb, bsz, seq_len)
    return xn, oa, ob


def _layer(x0, x1, p):
    d = x0.shape[-1]
    n0 = x0.shape[0] * x0.shape[1]
    n1 = x1.shape[0] * x1.shape[1]
    n = n0 + n1
    n_assign = n * TOP_K
    x1rows, lt = _merge_call(_mixers(x0, p), _mixers(x1, p), p["w_a"], p["w_b"], p["w_g"], p["b_g"], p["w_o"],
                             p["ln1_g"], p["ln1_b"], p["wr_hi"], p["wr_lo"], p["b_r"])
    e_rows, w_rows, cnt = _route_call(lt)
    e_flat = e_rows[:TOP_K].reshape(n_assign)
    _, slot_sorted = lax.sort((e_flat, jnp.arange(n_assign, dtype=jnp.int32)), num_keys=1)
    tok_sorted = jnp.where(slot_sorted >= n, slot_sorted - n, slot_sorted)
    ends = jnp.cumsum(cnt[:, 0].astype(jnp.int32))
    starts = jnp.concatenate([jnp.zeros((1,), jnp.int32), ends])
    rb = _expert_row_block(n_assign)
    plan = _plan_call(starts, n_assign, rb)
    y2 = _expert_call(plan, tok_sorted, slot_sorted, x1rows, p["w_gate"], p["w_up"], p["w_down"], rb)
    y2 = y2.reshape(TOP_K, n, 1, d)
    w_tok = w_rows[:TOP_K].T
    out0 = _final_call(x1rows, y2, w_tok, p["ln2_g"], p["ln2_b"], 0, n0)
    out1 = _final_call(x1rows, y2, w_tok, p["ln2_g"], p["ln2_b"], n0, n1)
    return out0.reshape(x0.shape), out1.reshape(x1.shape)


def kernel(x_prompt, x_sample, emb_ln_g, emb_ln_b, rel_bias, w_in, b_gate, sink_a, q_norm_g, k_norm_g,
           w_branch_a, w_branch_b, w_out, ln1_g, ln1_b, w_coarse, b_coarse, w_fine, b_fine,
           w_gate, w_up, w_down, ln2_g, ln2_b):
    l = 0
    w_in_l = w_in[l]
    qa_end = Q_COLS
    kva_end = qa_end + 2 * KV_COLS
    qb_end = kva_end + Q_COLS
    kb_end = qb_end + KV_COLS
    vb_end = kb_end + KV_COLS
    w_qkv = jnp.concatenate([
        w_in_l[:, :kva_end],
        _deinterleave_cols(w_in_l[:, kva_end:qb_end], N_HEADS),
        _deinterleave_cols(w_in_l[:, qb_end:kb_end], N_KV),
        w_in_l[:, kb_end:vb_end],
    ], axis=1).astype(BF16)
    w_router = jnp.zeros((D_MODEL, ROUTER_ROWS), F32)
    w_router = w_router.at[:, 0:N_GROUPS].set(w_coarse[l])
    w_router = w_router.at[:, FINE_ROW0:FINE_ROW0 + N_EXPERTS].set(w_fine[l])
    wr_hi = w_router.astype(BF16)
    wr_lo = (w_router - wr_hi.astype(F32)).astype(BF16)
    b_router = jnp.zeros((ROUTER_ROWS, 1), F32)
    b_router = b_router.at[0:N_GROUPS, 0].set(b_coarse[l].astype(F32))
    b_router = b_router.at[FINE_ROW0:FINE_ROW0 + N_EXPERTS, 0].set(b_fine[l].astype(F32))
    p = {
        "emb_g": emb_ln_g.reshape(1, D_MODEL), "emb_b": emb_ln_b.reshape(1, D_MODEL),
        "w_qkv": w_qkv,
        "q_gain": _deinterleave_gain(q_norm_g[l]), "k_gain": _deinterleave_gain(k_norm_g[l]),
        "sink": sink_a[l].astype(F32), "bias_a": _window_bias(rel_bias),
        "w_a": w_branch_a[l].astype(BF16), "w_b": w_branch_b[l].astype(BF16),
        "w_g": w_in_l[:, vb_end:].astype(BF16), "b_g": b_gate[l].reshape(1, 2 * D_MODEL),
        "w_o": w_out[l].astype(BF16),
        "ln1_g": ln1_g[l].reshape(1, D_MODEL), "ln1_b": ln1_b[l].reshape(1, D_MODEL),
        "wr_hi": wr_hi, "wr_lo": wr_lo, "b_r": b_router,
        "w_gate": w_gate[l], "w_up": w_up[l], "w_down": w_down[l],
        "ln2_g": ln2_g[l].reshape(1, D_MODEL), "ln2_b": ln2_b[l].reshape(1, D_MODEL),
    }
    return _layer(x_prompt, x_sample, p)
```

```python
import functools
import math

import numpy as np
import jax
import jax.numpy as jnp
from jax import lax
from jax.experimental import pallas as pl
from jax.experimental.pallas import tpu as pltpu

F32 = jnp.float32
BF16 = jnp.bfloat16

D_MODEL = 2048
HEAD_DIM = 128
N_HEADS = 8
N_KV = 2
GROUP = N_HEADS // N_KV
Q_COLS = N_HEADS * HEAD_DIM
KV_COLS = N_KV * HEAD_DIM
QKV_COLS = 2 * (Q_COLS + 2 * KV_COLS)
WINDOW = 128
BLOCK = 128
GRID_W = 64
ROPE_THETA = 10000.0
N_BUCKETS = 32
MAX_DISTANCE = 128
N_GROUPS = 4
EXPERTS_PER_GROUP = 8
N_EXPERTS = N_GROUPS * EXPERTS_PER_GROUP
TOP_K = 2
D_EXPERT = 512
LN_EPS = 1e-5
RMS_EPS = 1e-6
DEPTH = 1
ALPHA = (2 * DEPTH) ** 0.25
SCALE = HEAD_DIM ** -0.5
LOG2E = math.log2(math.e)
NEG_BIG = -1e30

LANES = 128
VMEM_LIMIT = 56 * 1024 * 1024

QKV_TM = 512
ATT_A_TQ = 512
ATT_B_TQ = 256
ATT_B_TK = 512
MERGE_TM = 256
MERGE_TC = 1024
ROUTE_TN = 1024
MOE_RB = 256
FINAL_TM = 512
ROUTER_ROWS = 128
FINE_ROW0 = 8


def _layer_norm_rows(x, g, b):
    mu = jnp.mean(x, axis=-1, keepdims=True)
    xc = x - mu
    var = jnp.mean(xc * xc, axis=-1, keepdims=True)
    return xc * lax.rsqrt(var + LN_EPS) * g + b


def _dot(a, b):
    return jnp.dot(a, b, preferred_element_type=F32)


def _dot_nt(a, b):
    return lax.dot_general(a, b, (((1,), (1,)), ((), ())), preferred_element_type=F32)


def _qkv_kernel(x_ref, g_ref, b_ref, w_ref, cos_ref, sin_ref, qg_ref, kg_ref,
                xn_ref, qa_ref, ka_ref, va_ref, qb_ref, kb_ref, vb_ref):
    xn = _layer_norm_rows(x_ref[...], g_ref[...], b_ref[...])
    xn_ref[...] = xn
    xb = xn.astype(BF16)
    cos = cos_ref[...]
    sin = sin_ref[...]

    def proj(c0, width):
        return _dot(xb, w_ref[:, c0:c0 + width])

    def norm_rope(h, gain):
        y = h * lax.rsqrt(jnp.mean(h * h, axis=-1, keepdims=True) + RMS_EPS) * gain
        return y * cos + pltpu.roll(y, HEAD_DIM // 2, 1) * sin

    c = Q_COLS + 2 * KV_COLS
    qg = qg_ref[...]
    for hp in range(N_HEADS // 2):
        hq2 = proj(c + 2 * hp * HEAD_DIM, 2 * HEAD_DIM)
        for h in (2 * hp, 2 * hp + 1):
            hq = hq2[:, (h - 2 * hp) * HEAD_DIM:(h - 2 * hp + 1) * HEAD_DIM]
            qb_ref[:, h * HEAD_DIM:(h + 1) * HEAD_DIM] = (norm_rope(hq, qg) * (SCALE * LOG2E)).astype(BF16)
    c += Q_COLS
    kg = kg_ref[...]
    hk2 = proj(c, KV_COLS)
    for h in range(N_KV):
        kb_ref[:, h * HEAD_DIM:(h + 1) * HEAD_DIM] = norm_rope(hk2[:, h * HEAD_DIM:(h + 1) * HEAD_DIM], kg).astype(BF16)
    c += KV_COLS
    hv2 = proj(c, KV_COLS)
    ones = jnp.ones((x_ref.shape[0], HEAD_DIM), BF16)
    for h in range(N_KV):
        vb_ref[:, 2 * h * HEAD_DIM:(2 * h + 1) * HEAD_DIM] = hv2[:, h * HEAD_DIM:(h + 1) * HEAD_DIM].astype(BF16)
        vb_ref[:, (2 * h + 1) * HEAD_DIM:(2 * h + 2) * HEAD_DIM] = ones
    c = 0
    qa_ref[...] = (proj(c, Q_COLS) * SCALE).astype(BF16)
    c += Q_COLS
    ka_ref[...] = proj(c, KV_COLS).astype(BF16)
    c += KV_COLS
    va_ref[...] = proj(c, KV_COLS).astype(BF16)


def _qkv_call(x2, g, b, w_qkv, cos_t, sin_t, qg, kg, seq_len):
    n = x2.shape[0]
    tm = QKV_TM
    pos_blocks = seq_len // tm
    row = lambda i: (i, 0)
    fixed = lambda i: (0, 0)
    pos = lambda i: (i % pos_blocks, 0)
    out_shapes = (
        jax.ShapeDtypeStruct((n, D_MODEL), F32),
        jax.ShapeDtypeStruct((n, Q_COLS), BF16),
        jax.ShapeDtypeStruct((n, KV_COLS), BF16),
        jax.ShapeDtypeStruct((n, KV_COLS), BF16),
        jax.ShapeDtypeStruct((n, Q_COLS), BF16),
        jax.ShapeDtypeStruct((n, KV_COLS), BF16),
        jax.ShapeDtypeStruct((n, 2 * KV_COLS), BF16),
    )
    return pl.pallas_call(
        _qkv_kernel,
        grid=(n // tm,),
        in_specs=[
            pl.BlockSpec((tm, D_MODEL), row),
            pl.BlockSpec((1, D_MODEL), fixed),
            pl.BlockSpec((1, D_MODEL), fixed),
            pl.BlockSpec((D_MODEL, QKV_COLS), fixed, pipeline_mode=pl.Buffered(1)),
            pl.BlockSpec((tm, HEAD_DIM), pos),
            pl.BlockSpec((tm, HEAD_DIM), pos),
            pl.BlockSpec((1, HEAD_DIM), fixed),
            pl.BlockSpec((1, HEAD_DIM), fixed),
        ],
        out_specs=(
            pl.BlockSpec((tm, D_MODEL), row),
            pl.BlockSpec((tm, Q_COLS), row),
            pl.BlockSpec((tm, KV_COLS), row),
            pl.BlockSpec((tm, KV_COLS), row),
            pl.BlockSpec((tm, Q_COLS), row),
            pl.BlockSpec((tm, KV_COLS), row),
            pl.BlockSpec((tm, 2 * KV_COLS), row),
        ),
        out_shape=out_shapes,
        compiler_params=pltpu.CompilerParams(
            dimension_semantics=("arbitrary",), vmem_limit_bytes=VMEM_LIMIT),
        name="ln_qkv",
    )(x2, g, b, w_qkv, cos_t, sin_t, qg, kg)


def _attn_a_kernel(sink_ref, q_ref, kc_ref, kp_ref, kn_ref, vc_ref, vp_ref, vn_ref, bias_ref,
                   o_ref, kcat_ref, vcat_ref, sa_ref, sb_ref, *, seq_len):
    tq = ATT_A_TQ
    i = pl.program_id(1)
    kcat_ref[0:BLOCK, :] = kp_ref[...]
    kcat_ref[BLOCK:BLOCK + tq, :] = kc_ref[...]
    kcat_ref[BLOCK + tq:, :] = kn_ref[...]
    vcat_ref[0:BLOCK, :] = vp_ref[...]
    vcat_ref[BLOCK:BLOCK + tq, :] = vc_ref[...]
    vcat_ref[BLOCK + tq:, :] = vn_ref[...]
    col = lax.broadcasted_iota(jnp.int32, (1, 3 * BLOCK), 1)
    s_refs = (sa_ref, sb_ref)

    def block_scores(j, s_ref):
        qj = q_ref[j * BLOCK:(j + 1) * BLOCK, :]
        for g in range(N_KV):
            qs = jnp.concatenate(
                [qj[:, h * HEAD_DIM:(h + 1) * HEAD_DIM] for h in range(g * GROUP, (g + 1) * GROUP)], axis=0)
            kw = kcat_ref[j * BLOCK:(j + 3) * BLOCK, g * HEAD_DIM:(g + 1) * HEAD_DIM]
            s_ref[g * GROUP * BLOCK:(g + 1) * GROUP * BLOCK, :] = _dot_nt(qs, kw)

    def block_out(j, s_ref):
        key_pos = i * tq + (j - 1) * BLOCK + col
        inside = (key_pos >= 0) & (key_pos < seq_len)
        for g in range(N_KV):
            probs = []
            for h in range(g * GROUP, (g + 1) * GROUP):
                s = s_ref[h * BLOCK:(h + 1) * BLOCK, :]
                s = jnp.where(inside, s + bias_ref[h], NEG_BIG)
                sk = sink_ref[h]
                m = jnp.maximum(jnp.max(s, axis=-1, keepdims=True), sk)
                p = jnp.exp(s - m)
                denom = jnp.sum(p, axis=-1, keepdims=True) + jnp.exp(sk - m)
                probs.append((p * (1.0 / denom)).astype(BF16))
            vw = vcat_ref[j * BLOCK:(j + 3) * BLOCK, g * HEAD_DIM:(g + 1) * HEAD_DIM]
            o = _dot(jnp.concatenate(probs, axis=0), vw)
            for hh in range(GROUP):
                h = g * GROUP + hh
                o_ref[j * BLOCK:(j + 1) * BLOCK, h * HEAD_DIM:(h + 1) * HEAD_DIM] = (
                    o[hh * BLOCK:(hh + 1) * BLOCK, :].astype(BF16))

    n_blk = tq // BLOCK
    block_scores(0, s_refs[0])
    for j in range(n_blk):
        if j + 1 < n_blk:
            block_scores(j + 1, s_refs[(j + 1) % 2])
        block_out(j, s_refs[j % 2])


def _attn_a_call(sink, qa, ka, va, bias, bsz, seq_len):
    tq = ATT_A_TQ
    nb = seq_len // BLOCK
    bpt = tq // BLOCK
    q3 = qa.reshape(bsz, seq_len, Q_COLS)
    k3 = ka.reshape(bsz, seq_len, KV_COLS)
    v3 = va.reshape(bsz, seq_len, KV_COLS)
    cur = lambda b, i: (b, i, 0)
    prev = lambda b, i: (b, jnp.maximum(i * bpt - 1, 0), 0)
    nxt = lambda b, i: (b, jnp.minimum((i + 1) * bpt, nb - 1), 0)
    kv_cur = pl.BlockSpec((None, tq, KV_COLS), cur)
    kv_prev = pl.BlockSpec((None, BLOCK, KV_COLS), prev)
    kv_next = pl.BlockSpec((None, BLOCK, KV_COLS), nxt)
    out = pl.pallas_call(
        functools.partial(_attn_a_kernel, seq_len=seq_len),
        grid=(bsz, seq_len // tq),
        in_specs=[
            pl.BlockSpec(memory_space=pltpu.SMEM),
            pl.BlockSpec((None, tq, Q_COLS), cur),
            kv_cur, kv_prev, kv_next,
            kv_cur, kv_prev, kv_next,
            pl.BlockSpec((N_HEADS, BLOCK, 3 * BLOCK), lambda b, i: (0, 0, 0)),
        ],
        out_specs=pl.BlockSpec((None, tq, Q_COLS), cur),
        out_shape=jax.ShapeDtypeStruct((bsz, seq_len, Q_COLS), BF16),
        scratch_shapes=[
            pltpu.VMEM((tq + 2 * BLOCK, KV_COLS), BF16),
            pltpu.VMEM((tq + 2 * BLOCK, KV_COLS), BF16),
            pltpu.VMEM((N_HEADS * BLOCK, 3 * BLOCK), F32),
            pltpu.VMEM((N_HEADS * BLOCK, 3 * BLOCK), F32),
        ],
        compiler_params=pltpu.CompilerParams(
            dimension_semantics=("arbitrary", "arbitrary"), vmem_limit_bytes=VMEM_LIMIT),
        name="attn_window",
    )(sink, q3, k3, k3, k3, v3, v3, v3, bias)
    return out.reshape(bsz * seq_len, Q_COLS)


def _attn_b_kernel(q_ref, qn_ref, k_ref, v_ref, o_ref, qs_ref, qsn_ref, m_ref, acc_ref,
                   s0_ref, s1_ref, p0_ref, p1_ref, a0_ref, a1_ref, *, seq_len):
    tq = ATT_B_TQ
    tk = ATT_B_TK
    n_chunks = seq_len // tk
    rows = GROUP * tq
    for h in range(GROUP):
        qs_ref[h * tq:(h + 1) * tq, :] = q_ref[:, h * HEAD_DIM:(h + 1) * HEAD_DIM]
        qsn_ref[h * tq:(h + 1) * tq, :] = qn_ref[:, h * HEAD_DIM:(h + 1) * HEAD_DIM]

    def scores(q_src, c, s_ref):
        k0 = pl.multiple_of(c * tk, tk)
        s_ref[...] = _dot_nt(q_src[...], k_ref[pl.ds(k0, tk), :])

    def softmax(s_ref, p_ref, a_ref, first=False):
        m_tile = s_ref[:, 0:LANES]
        for j in range(1, tk // LANES):
            m_tile = jnp.maximum(m_tile, s_ref[:, j * LANES:(j + 1) * LANES])
        m_cur = jnp.broadcast_to(jnp.max(m_tile, axis=-1, keepdims=True), (rows, LANES))
        if first:
            m_new = m_cur
            a_ref[...] = jnp.zeros((rows, LANES), F32)
        else:
            m_prev = m_ref[...]
            m_new = jnp.maximum(m_prev, m_cur)
            a_ref[...] = jnp.exp2(m_prev - m_new)
        m_ref[...] = m_new
        for j in range(tk // LANES):
            p_ref[:, j * LANES:(j + 1) * LANES] = jnp.exp2(s_ref[:, j * LANES:(j + 1) * LANES] - m_new).astype(BF16)

    def weighted_values(c, p_ref, a_ref):
        k0 = pl.multiple_of(c * tk, tk)
        pv = _dot(p_ref[...], v_ref[pl.ds(k0, tk), :])
        a = a_ref[...]
        acc_ref[:, 0:HEAD_DIM] = a * acc_ref[:, 0:HEAD_DIM] + pv[:, 0:HEAD_DIM]
        acc_ref[:, HEAD_DIM:] = a * acc_ref[:, HEAD_DIM:] + pv[:, HEAD_DIM:]

    @pl.when(pl.program_id(2) == 0)
    def _():
        scores(qs_ref, 0, s0_ref)
        scores(qs_ref, 1, s1_ref)
        softmax(s0_ref, p0_ref, a0_ref, first=True)

    acc_ref[...] = jnp.zeros(acc_ref.shape, F32)

    def body(t, carry):
        c = 2 * t + 1
        scores(qs_ref, c + 1, s0_ref)
        softmax(s1_ref, p1_ref, a1_ref)
        weighted_values(c - 1, p0_ref, a0_ref)
        scores(qs_ref, c + 2, s1_ref)
        softmax(s0_ref, p0_ref, a0_ref)
        weighted_values(c, p1_ref, a1_ref)
        return carry

    lax.fori_loop(0, (n_chunks - 2) // 2, body, 0, unroll=True)
    scores(qsn_ref, 0, s0_ref)
    softmax(s1_ref, p1_ref, a1_ref)
    weighted_values(n_chunks - 2, p0_ref, a0_ref)
    scores(qsn_ref, 1, s1_ref)
    softmax(s0_ref, p0_ref, a0_ref, first=True)
    weighted_values(n_chunks - 1, p1_ref, a1_ref)
    o = acc_ref[:, 0:HEAD_DIM] / acc_ref[:, HEAD_DIM:]
    for h in range(GROUP):
        o_ref[:, h * HEAD_DIM:(h + 1) * HEAD_DIM] = o[h * tq:(h + 1) * tq, :].astype(BF16)


def _attn_b_call(qb, kb, vb, bsz, seq_len):
    tq = ATT_B_TQ
    tk = ATT_B_TK
    assert seq_len % tk == 0 and (seq_len // tk) % 2 == 0, "dense mixer pipeline needs an even number of key chunks"
    q3 = qb.reshape(bsz, seq_len, Q_COLS)
    k3 = kb.reshape(bsz, seq_len, KV_COLS)
    v3 = vb.reshape(bsz, seq_len, 2 * KV_COLS)
    gcols = GROUP * HEAD_DIM
    n_q = seq_len // tq
    rows = GROUP * tq
    out = pl.pallas_call(
        functools.partial(_attn_b_kernel, seq_len=seq_len),
        grid=(bsz, N_KV, n_q),
        in_specs=[
            pl.BlockSpec((None, tq, gcols), lambda b, g, i: (b, i, g)),
            pl.BlockSpec((None, tq, gcols), lambda b, g, i: (b, jnp.minimum(i + 1, n_q - 1), g)),
            pl.BlockSpec((None, seq_len, HEAD_DIM), lambda b, g, i: (b, 0, g)),
            pl.BlockSpec((None, seq_len, 2 * HEAD_DIM), lambda b, g, i: (b, 0, g)),
        ],
        out_specs=pl.BlockSpec((None, tq, gcols), lambda b, g, i: (b, i, g)),
        out_shape=jax.ShapeDtypeStruct((bsz, seq_len, Q_COLS), BF16),
        scratch_shapes=[
            pltpu.VMEM((rows, HEAD_DIM), BF16),
            pltpu.VMEM((rows, HEAD_DIM), BF16),
            pltpu.VMEM((rows, LANES), F32),
            pltpu.VMEM((rows, 2 * HEAD_DIM), F32),
            pltpu.VMEM((rows, tk), F32),
            pltpu.VMEM((rows, tk), F32),
            pltpu.VMEM((rows, tk), BF16),
            pltpu.VMEM((rows, tk), BF16),
            pltpu.VMEM((rows, LANES), F32),
            pltpu.VMEM((rows, LANES), F32),
        ],
        compiler_params=pltpu.CompilerParams(
            dimension_semantics=("arbitrary", "arbitrary", "arbitrary"), vmem_limit_bytes=VMEM_LIMIT),
        name="attn_dense",
    )(q3, q3, k3, v3)
    return out.reshape(bsz * seq_len, Q_COLS)


def _merge_kernel(xn0_ref, oa0_ref, ob0_ref, xn1_ref, oa1_ref, ob1_ref, *rest, tiles0):
    i = pl.program_id(0)

    @pl.when(i < tiles0)
    def _():
        _merge_tile(xn0_ref, oa0_ref, ob0_ref, *rest)

    @pl.when(i >= tiles0)
    def _():
        _merge_tile(xn1_ref, oa1_ref, ob1_ref, *rest)


def _merge_tile(xn_ref, oa_ref, ob_ref, wa_ref, wb_ref, wg_ref, bg_ref, wo_ref,
                g_ref, b_ref, wrh_ref, wrl_ref, br_ref, x1rows_ref, lt_ref, acc_ref):
    tc = MERGE_TC
    xb = xn_ref[...].astype(BF16)
    oa = oa_ref[...]
    ob = ob_ref[...]
    for c in range(D_MODEL // tc):
        lo, hi = c * tc, (c + 1) * tc
        ga = jax.nn.sigmoid(_dot(xb, wg_ref[:, lo:hi]) + bg_ref[:, lo:hi])
        gb = jax.nn.sigmoid(_dot(xb, wg_ref[:, D_MODEL + lo:D_MODEL + hi]) + bg_ref[:, D_MODEL + lo:D_MODEL + hi])
        mixed = (ga * _dot(oa, wa_ref[:, lo:hi]) + gb * _dot(ob, wb_ref[:, lo:hi])).astype(BF16)
        part = _dot(mixed, wo_ref[lo:hi, :])
        if c == 0:
            acc_ref[...] = part
        else:
            acc_ref[...] += part
    x1 = _layer_norm_rows(ALPHA * xn_ref[...] + acc_ref[...], g_ref[...], b_ref[...])
    x1rows_ref[...] = x1.reshape(x1.shape[0], 1, D_MODEL)
    xh = x1.astype(BF16)
    xl = (x1 - xh.astype(F32)).astype(BF16)
    wrh = wrh_ref[...]
    logits = _dot(xh, wrh) + (_dot(xl, wrh) + _dot(xh, wrl_ref[...]))
    lt_ref[...] = logits.T + br_ref[...]


def _merge_call(batch0, batch1, wa, wb, wg, bg, wo, g, b, wr_hi, wr_lo, br):
    tm = MERGE_TM
    tiles0 = batch0[0].shape[0] // tm
    n = batch0[0].shape[0] + batch1[0].shape[0]
    rows0 = lambda i: (jnp.minimum(i, tiles0 - 1), 0)
    rows1 = lambda i: (jnp.maximum(i - tiles0, 0), 0)
    fixed = lambda i: (0, 0)

    def resident(shape):
        return pl.BlockSpec(shape, fixed, pipeline_mode=pl.Buffered(1))

    return pl.pallas_call(
        functools.partial(_merge_kernel, tiles0=tiles0),
        grid=(n // tm,),
        in_specs=[
            pl.BlockSpec((tm, D_MODEL), rows0),
            pl.BlockSpec((tm, Q_COLS), rows0),
            pl.BlockSpec((tm, Q_COLS), rows0),
            pl.BlockSpec((tm, D_MODEL), rows1),
            pl.BlockSpec((tm, Q_COLS), rows1),
            pl.BlockSpec((tm, Q_COLS), rows1),
            resident((Q_COLS, D_MODEL)),
            resident((Q_COLS, D_MODEL)),
            resident((D_MODEL, 2 * D_MODEL)),
            resident((1, 2 * D_MODEL)),
            resident((D_MODEL, D_MODEL)),
            resident((1, D_MODEL)),
            resident((1, D_MODEL)),
            resident((D_MODEL, ROUTER_ROWS)),
            resident((D_MODEL, ROUTER_ROWS)),
            resident((ROUTER_ROWS, 1)),
        ],
        out_specs=(
            pl.BlockSpec((tm, 1, D_MODEL), lambda i: (i, 0, 0)),
            pl.BlockSpec((ROUTER_ROWS, tm), lambda i: (0, i)),
        ),
        out_shape=(
            jax.ShapeDtypeStruct((n, 1, D_MODEL), F32),
            jax.ShapeDtypeStruct((ROUTER_ROWS, n), F32),
        ),
        scratch_shapes=[pltpu.VMEM((tm, D_MODEL), F32)],
        compiler_params=pltpu.CompilerParams(
            dimension_semantics=("arbitrary",), vmem_limit_bytes=VMEM_LIMIT),
        name="merge_ln_router",
    )(*batch0, *batch1, wa, wb, wg, bg, wo, g, b, wr_hi, wr_lo, br)


def _route_kernel(lt_ref, e_ref, w_ref, cnt_ref, carry_ref):
    tn = lt_ref.shape[1]

    @pl.when(pl.program_id(0) == 0)
    def _():
        carry_ref[...] = jnp.zeros(carry_ref.shape, F32)

    cl = lt_ref[0:N_GROUPS, :]
    gi = lax.broadcasted_iota(jnp.int32, (N_GROUPS, tn), 0)
    cmax = jnp.max(cl, axis=0, keepdims=True)
    ce = jnp.exp(cl - cmax)
    cp = ce / jnp.sum(ce, axis=0, keepdims=True)
    g_idx = jnp.min(jnp.where(cl == cmax, gi, N_GROUPS), axis=0, keepdims=True)
    g_prob = jnp.sum(jnp.where(gi == g_idx, cp, 0.0), axis=0, keepdims=True)

    fl = jnp.zeros((EXPERTS_PER_GROUP, tn), F32)
    for g in range(N_GROUPS):
        r0 = FINE_ROW0 + g * EXPERTS_PER_GROUP
        fl = jnp.where(g_idx == g, lt_ref[r0:r0 + EXPERTS_PER_GROUP, :], fl)
    fmax = jnp.max(fl, axis=0, keepdims=True)
    fe = jnp.exp(fl - fmax)
    fp = fe / jnp.sum(fe, axis=0, keepdims=True)
    ei = lax.broadcasted_iota(jnp.int32, (EXPERTS_PER_GROUP, tn), 0)
    p1 = jnp.max(fp, axis=0, keepdims=True)
    i1 = jnp.min(jnp.where(fp == p1, ei, EXPERTS_PER_GROUP), axis=0, keepdims=True)
    fp2 = jnp.where(ei == i1, -1.0, fp)
    p2 = jnp.max(fp2, axis=0, keepdims=True)
    i2 = jnp.min(jnp.where(fp2 == p2, ei, EXPERTS_PER_GROUP), axis=0, keepdims=True)
    tot = p1 + p2
    w1 = g_prob * p1 / tot
    w2 = g_prob * p2 / tot
    e1 = g_idx * EXPERTS_PER_GROUP + i1
    e2 = g_idx * EXPERTS_PER_GROUP + i2
    ri = lax.broadcasted_iota(jnp.int32, (8, tn), 0)
    e_ref[...] = jnp.where(ri == 0, e1, jnp.where(ri == 1, e2, 0))
    w_ref[...] = jnp.where(ri == 0, w1, jnp.where(ri == 1, w2, 0.0))

    xi = lax.broadcasted_iota(jnp.int32, (N_EXPERTS, tn), 0)
    hits = jnp.where(xi == e1, 1.0, 0.0) + jnp.where(xi == e2, 1.0, 0.0)
    total = carry_ref[...] + jnp.sum(hits, axis=1, keepdims=True)
    carry_ref[...] = total
    cnt_ref[...] = jnp.broadcast_to(total, cnt_ref.shape)


def _route_call(lt):
    n = lt.shape[1]
    tn = min(ROUTE_TN, n)
    tok = lambda i: (0, i)
    return pl.pallas_call(
        _route_kernel,
        grid=(n // tn,),
        in_specs=[pl.BlockSpec((ROUTER_ROWS, tn), tok)],
        out_specs=(pl.BlockSpec((8, tn), tok), pl.BlockSpec((8, tn), tok),
                   pl.BlockSpec((N_EXPERTS, LANES), lambda i: (0, 0))),
        out_shape=(jax.ShapeDtypeStruct((8, n), jnp.int32), jax.ShapeDtypeStruct((8, n), F32),
                   jax.ShapeDtypeStruct((N_EXPERTS, LANES), F32)),
        scratch_shapes=[pltpu.VMEM((N_EXPERTS, 1), F32)],
        compiler_params=pltpu.CompilerParams(dimension_semantics=("arbitrary",)),
        name="router",
    )(lt)


def _plan_kernel(starts_ref, vb_ref, ve_ref, vlo_ref, vhi_ref, *, rb):
    n_visits = vb_ref.shape[0]
    shift = rb.bit_length() - 1

    def expert_body(e, v):
        end = starts_ref[e + 1]

        def cond(state):
            return state[0] < end

        def body(state):
            r, v = state
            b = lax.shift_right_logical(r, shift)
            hi = jnp.minimum(end, (b + 1) * rb)
            vb_ref[v] = b
            ve_ref[v] = e
            vlo_ref[v] = r - b * rb
            vhi_ref[v] = hi - b * rb
            return hi, v + 1

        return lax.while_loop(cond, body, (starts_ref[e], v))[1]

    used = lax.fori_loop(0, N_EXPERTS, expert_body, 0)
    last_b = vb_ref[used - 1]
    last_e = ve_ref[used - 1]

    def pad(i, carry):
        vb_ref[i] = last_b
        ve_ref[i] = last_e
        vlo_ref[i] = 0
        vhi_ref[i] = 0
        return carry

    lax.fori_loop(used, n_visits, pad, 0)


def _plan_call(starts, n_assign, rb):
    assert rb & (rb - 1) == 0 and n_assign % rb == 0
    n_visits = n_assign // rb + N_EXPERTS - 1
    smem = pl.BlockSpec(memory_space=pltpu.SMEM)
    sds = jax.ShapeDtypeStruct((n_visits,), jnp.int32)
    return pl.pallas_call(
        functools.partial(_plan_kernel, rb=rb),
        in_specs=[smem], out_specs=(smem,) * 4, out_shape=(sds,) * 4, name="visit_plan",
    )(starts)


def _expert_kernel(vb_ref, ve_ref, vlo_ref, vhi_ref,
                   tok0_ref, tokn_ref, slot_ref, slotp_ref, x_hbm, wg_ref, wu_ref, wd_ref, y_hbm,
                   xin_ref, xf_ref, xb_ref, ys_ref, yout_ref, wgb_ref, wub_ref, wdb_ref,
                   state_ref, gsem, ssem):
    rb = xb_ref.shape[0]
    v = pl.program_id(0)
    b = vb_ref[v]
    lo = vlo_ref[v]
    hi = vhi_ref[v]
    unsent, in_flight = 0, 1

    def gather_row(idx_ref, r):
        return pltpu.make_async_copy(x_hbm.at[pl.ds(idx_ref[0, 0, r], 1)], xin_ref.at[pl.ds(r, 1)], gsem)

    def scatter_row(buf, idx_ref, r):
        return pltpu.make_async_copy(yout_ref.at[buf, pl.ds(r, 1)], y_hbm.at[pl.ds(idx_ref[0, 0, r], 1)], ssem.at[buf])

    def start_gather(idx_ref):
        def body(r, carry):
            gather_row(idx_ref, r).start()
            return carry
        lax.fori_loop(0, rb, body, 0, unroll=8)

    def wait_gather():
        def body(r, carry):
            gather_row(tok0_ref, r).wait()
            return carry
        lax.fori_loop(0, rb, body, 0, unroll=8)

    def wait_scatter(buf):
        def body(r, carry):
            scatter_row(buf, slot_ref, r).wait()
            return carry
        lax.fori_loop(0, rb, body, 0, unroll=8)
        state_ref[in_flight + buf] = 0

    def park_block(get_rows):
        buf = b & 1

        @pl.when(state_ref[in_flight + buf] == 1)
        def _():
            wait_scatter(buf)

        yout_ref[buf] = get_rows().reshape(rb, 1, D_MODEL)
        state_ref[unsent] = 1

    def mark_sent(buf):
        state_ref[unsent] = 0
        state_ref[in_flight + buf] = 1

    @pl.when(v == 0)
    def _():
        state_ref[unsent] = 0
        state_ref[in_flight] = 0
        state_ref[in_flight + 1] = 0
        start_gather(tok0_ref)

    @pl.when(hi > lo)
    def _():
        @pl.when(lo == 0)
        def _():
            wait_gather()
            xf_ref[...] = xin_ref[...].reshape(rb, D_MODEL)
            xb_ref[...] = xf_ref[...].astype(BF16)

        @pl.when((v == 0) | (ve_ref[v] != ve_ref[jnp.maximum(v - 1, 0)]))
        def _():
            wgb_ref[...] = wg_ref[...].astype(BF16)
            wub_ref[...] = wu_ref[...].astype(BF16)
            wdb_ref[...] = wd_ref[...].astype(BF16)

        def expert_rows():
            x = xb_ref[...]
            hidden = jax.nn.silu(_dot(x, wgb_ref[...])) * _dot(x, wub_ref[...])
            return _dot(hidden.astype(BF16), wdb_ref[...])

        whole = (lo == 0) & (hi == rb)
        prev_buf = (b + 1) & 1

        @pl.when(whole & (b > 0))
        def _():
            for r in range(rb):
                gather_row(tokn_ref, r).start()
                scatter_row(prev_buf, slotp_ref, r).start(priority=r % 2)
            mark_sent(prev_buf)
            y = expert_rows()
            park_block(lambda: y)

        @pl.when(whole & (b == 0))
        def _():
            for r in range(rb):
                gather_row(tokn_ref, r).start()
            y = expert_rows()
            park_block(lambda: y)

        @pl.when(jnp.logical_not(whole))
        def _():
            @pl.when(lo == 0)
            def _():
                start_gather(tokn_ref)

                @pl.when(state_ref[unsent] == 1)
                def _():
                    def body(r, carry):
                        scatter_row(prev_buf, slotp_ref, r).start()
                        return carry
                    lax.fori_loop(0, rb, body, 0, unroll=8)
                    mark_sent(prev_buf)

            y = expert_rows()
            rows = lax.broadcasted_iota(jnp.int32, (rb, 1), 0)
            mine = (rows >= lo) & (rows < hi)

            @pl.when(lo == 0)
            def _():
                ys_ref[...] = jnp.where(mine, y, 0.0)

            @pl.when(lo > 0)
            def _():
                ys_ref[...] = jnp.where(mine, y, ys_ref[...])

            @pl.when(hi == rb)
            def _():
                park_block(lambda: ys_ref[...])

    @pl.when(v == pl.num_programs(0) - 1)
    def _():
        last_buf = b & 1

        @pl.when(state_ref[unsent] == 1)
        def _():
            def body(r, carry):
                scatter_row(last_buf, slot_ref, r).start()
                return carry
            lax.fori_loop(0, rb, body, 0, unroll=8)
            mark_sent(last_buf)

        for buf in range(2):
            @pl.when(state_ref[in_flight + buf] == 1)
            def _():
                wait_scatter(buf)

        wait_gather()


def _expert_call(plan, tok_sorted, slot_sorted, x1rows, w_gate, w_up, w_down, rb):
    n_assign = tok_sorted.shape[0]
    n_blocks = n_assign // rb
    n_visits = plan[0].shape[0]
    tok3 = tok_sorted.reshape(n_blocks, 1, rb)
    slot3 = slot_sorted.reshape(n_blocks, 1, rb)
    w_map = lambda v, vb, ve, vlo, vhi: (ve[v], 0, 0)
    idx_block = (1, 1, rb)
    grid_spec = pltpu.PrefetchScalarGridSpec(
        num_scalar_prefetch=4,
        grid=(n_visits,),
        in_specs=[
            pl.BlockSpec(idx_block, lambda v, vb, ve, vlo, vhi: (0, 0, 0), memory_space=pltpu.SMEM),
            pl.BlockSpec(idx_block, lambda v, vb, ve, vlo, vhi: (jnp.minimum(vb[v] + 1, n_blocks - 1), 0, 0),
                         memory_space=pltpu.SMEM),
            pl.BlockSpec(idx_block, lambda v, vb, ve, vlo, vhi: (vb[v], 0, 0), memory_space=pltpu.SMEM),
            pl.BlockSpec(idx_block, lambda v, vb, ve, vlo, vhi: (jnp.maximum(vb[v] - 1, 0), 0, 0),
                         memory_space=pltpu.SMEM),
            pl.BlockSpec(memory_space=pl.ANY),
            pl.BlockSpec((None, D_MODEL, D_EXPERT), w_map),
            pl.BlockSpec((None, D_MODEL, D_EXPERT), w_map),
            pl.BlockSpec((None, D_EXPERT, D_MODEL), w_map),
        ],
        out_specs=pl.BlockSpec(memory_space=pl.ANY),
        scratch_shapes=[
            pltpu.VMEM((rb, 1, D_MODEL), F32),
            pltpu.VMEM((rb, D_MODEL), F32),
            pltpu.VMEM((rb, D_MODEL), BF16),
            pltpu.VMEM((rb, D_MODEL), F32),
            pltpu.VMEM((2, rb, 1, D_MODEL), F32),
            pltpu.VMEM((D_MODEL, D_EXPERT), BF16),
            pltpu.VMEM((D_MODEL, D_EXPERT), BF16),
            pltpu.VMEM((D_EXPERT, D_MODEL), BF16),
            pltpu.SMEM((3,), jnp.int32),
            pltpu.SemaphoreType.DMA(()),
            pltpu.SemaphoreType.DMA((2,)),
        ],
    )
    return pl.pallas_call(
        _expert_kernel,
        grid_spec=grid_spec,
        out_shape=jax.ShapeDtypeStruct((n_assign, 1, D_MODEL), F32),
        compiler_params=pltpu.CompilerParams(
            dimension_semantics=("arbitrary",), vmem_limit_bytes=VMEM_LIMIT),
        name="experts",
    )(*plan, tok3, tok3, slot3, slot3, x1rows, w_gate, w_up, w_down)


def _final_kernel(x_ref, y0_ref, y1_ref, w_ref, g_ref, b_ref, o_ref, xs_ref, ya_ref, yb_ref):
    tm = o_ref.shape[0]
    xs_ref[...] = x_ref[...].reshape(tm, D_MODEL)
    ya_ref[...] = y0_ref[...].reshape(tm, D_MODEL)
    yb_ref[...] = y1_ref[...].reshape(tm, D_MODEL)
    w = w_ref[...]
    moe = ya_ref[...] * w[:, 0:1] + yb_ref[...] * w[:, 1:2]
    o_ref[...] = _layer_norm_rows(ALPHA * xs_ref[...] + moe, g_ref[...], b_ref[...])


def _final_call(x1rows, y2, w_tok, g, b, row0, n):
    tm = FINAL_TM
    assert row0 % tm == 0 and n % tm == 0
    t0 = row0 // tm
    fixed = lambda i: (0, 0)
    return pl.pallas_call(
        _final_kernel,
        grid=(n // tm,),
        in_specs=[
            pl.BlockSpec((tm, 1, D_MODEL), lambda i: (t0 + i, 0, 0)),
            pl.BlockSpec((None, tm, 1, D_MODEL), lambda i: (0, t0 + i, 0, 0)),
            pl.BlockSpec((None, tm, 1, D_MODEL), lambda i: (1, t0 + i, 0, 0)),
            pl.BlockSpec((tm, TOP_K), lambda i: (t0 + i, 0)),
            pl.BlockSpec((1, D_MODEL), fixed),
            pl.BlockSpec((1, D_MODEL), fixed),
        ],
        out_specs=pl.BlockSpec((tm, D_MODEL), lambda i: (i, 0)),
        out_shape=jax.ShapeDtypeStruct((n, D_MODEL), F32),
        scratch_shapes=[pltpu.VMEM((tm, D_MODEL), F32)] * 3,
        compiler_params=pltpu.CompilerParams(
            dimension_semantics=("arbitrary",), vmem_limit_bytes=VMEM_LIMIT),
        name="combine_ln",
    )(x1rows, y2, y2, w_tok, g, b)


def _t5_bucket(rel):
    half = N_BUCKETS // 2
    exact = half // 2
    n = jnp.abs(rel)
    nf = jnp.maximum(n, 1).astype(F32)
    large = exact + (jnp.log(nf / exact) / math.log(MAX_DISTANCE / exact) * (half - exact)).astype(jnp.int32)
    large = jnp.minimum(large, half - 1)
    return jnp.where(rel > 0, half, 0) + jnp.where(n < exact, n, large)


def _window_bias(rel_bias):
    rel = jnp.arange(3 * BLOCK)[None, :] - BLOCK - jnp.arange(BLOCK)[:, None]
    onehot = (_t5_bucket(rel)[..., None] == jnp.arange(N_BUCKETS)).astype(F32)
    bias = jnp.einsum("qkb,bh->hqk", onehot, rel_bias.astype(F32), precision=lax.Precision.HIGHEST)
    return jnp.where((jnp.abs(rel) <= WINDOW)[None], bias, NEG_BIG)


def _rope_tables(seq_len):
    rows = seq_len // GRID_W
    row_ids = jnp.repeat(jnp.arange(rows), GRID_W).astype(F32)
    col_ids = jnp.tile(jnp.arange(GRID_W), rows).astype(F32)
    half = HEAD_DIM // 2
    inv = 1.0 / (ROPE_THETA ** (jnp.arange(0, half, 2, dtype=F32) / half))
    ang = jnp.concatenate([row_ids[:, None] * inv, col_ids[:, None] * inv], -1)
    cos, sin = jnp.cos(ang), jnp.sin(ang)
    return jnp.concatenate([cos, cos], -1), jnp.concatenate([-sin, sin], -1)


def _deinterleave_cols(w, n_heads):
    d = w.shape[0]
    return w.reshape(d, n_heads, HEAD_DIM // 2, 2).transpose(0, 1, 3, 2).reshape(d, n_heads * HEAD_DIM)


def _deinterleave_gain(g):
    return g.reshape(HEAD_DIM // 2, 2).T.reshape(1, HEAD_DIM)


def _expert_row_block(n_assign):
    return MOE_RB if n_assign // N_EXPERTS >= 8 * MOE_RB else MOE_RB // 2


def _mixers(x, p):
    bsz, seq_len, d = x.shape
    cos_t, sin_t = _rope_tables(seq_len)
    xn, qa, ka, va, qb, kb, vb = _qkv_call(x.reshape(bsz * seq_len, d), p["emb_g"], p["emb_b"], p["w_qkv"],
                                           cos_t, sin_t, p["q_gain"], p["k_gain"], seq_len)
    oa = _attn_a_call(p["sink"], qa, ka, va, p["bias_a"], bsz, seq_len)
    ob = _attn_b_call(qb, kb, vb, bsz, seq_len)
    return xn, oa, ob


def _layer(x0, x1, p):
    d = x0.shape[-1]
    n0 = x0.shape[0] * x0.shape[1]
    n1 = x1.shape[0] * x1.shape[1]
    n = n0 + n1
    n_assign = n * TOP_K
    x1rows, lt = _merge_call(_mixers(x0, p), _mixers(x1, p), p["w_a"], p["w_b"], p["w_g"], p["b_g"], p["w_o"],
                             p["ln1_g"], p["ln1_b"], p["wr_hi"], p["wr_lo"], p["b_r"])
    e_rows, w_rows, cnt = _route_call(lt)
    e_flat = e_rows[:TOP_K].reshape(n_assign)
    _, slot_sorted = lax.sort((e_flat, jnp.arange(n_assign, dtype=jnp.int32)), num_keys=1)
    tok_sorted = jnp.where(slot_sorted >= n, slot_sorted - n, slot_sorted)
    ends = jnp.cumsum(cnt[:, 0].astype(jnp.int32))
    starts = jnp.concatenate([jnp.zeros((1,), jnp.int32), ends])
    rb = _expert_row_block(n_assign)
    plan = _plan_call(starts, n_assign, rb)
    y2 = _expert_call(plan, tok_sorted, slot_sorted, x1rows, p["w_gate"], p["w_up"], p["w_down"], rb)
    y2 = y2.reshape(TOP_K, n, 1, d)
    w_tok = w_rows[:TOP_K].T
    out0 = _final_call(x1rows, y2, w_tok, p["ln2_g"], p["ln2_b"], 0, n0)
    out1 = _final_call(x1rows, y2, w_tok, p["ln2_g"], p["ln2_b"], n0, n1)
    return out0.reshape(x0.shape), out1.reshape(x1.shape)


def kernel(x_prompt, x_sample, emb_ln_g, emb_ln_b, rel_bias, w_in, b_gate, sink_a, q_norm_g, k_norm_g,
           w_branch_a, w_branch_b, w_out, ln1_g, ln1_b, w_coarse, b_coarse, w_fine, b_fine,
           w_gate, w_up, w_down, ln2_g, ln2_b):
    l = 0
    w_in_l = w_in[l]
    qa_end = Q_COLS
    kva_end = qa_end + 2 * KV_COLS
    qb_end = kva_end + Q_COLS
    kb_end = qb_end + KV_COLS
    vb_end = kb_end + KV_COLS
    w_qkv = jnp.concatenate([
        w_in_l[:, :kva_end],
        _deinterleave_cols(w_in_l[:, kva_end:qb_end], N_HEADS),
        _deinterleave_cols(w_in_l[:, qb_end:kb_end], N_KV),
        w_in_l[:, kb_end:vb_end],
    ], axis=1).astype(BF16)
    w_router = jnp.zeros((D_MODEL, ROUTER_ROWS), F32)
    w_router = w_router.at[:, 0:N_GROUPS].set(w_coarse[l])
    w_router = w_router.at[:, FINE_ROW0:FINE_ROW0 + N_EXPERTS].set(w_fine[l])
    wr_hi = w_router.astype(BF16)
    wr_lo = (w_router - wr_hi.astype(F32)).astype(BF16)
    b_router = jnp.zeros((ROUTER_ROWS, 1), F32)
    b_router = b_router.at[0:N_GROUPS, 0].set(b_coarse[l].astype(F32))
    b_router = b_router.at[FINE_ROW0:FINE_ROW0 + N_EXPERTS, 0].set(b_fine[l].astype(F32))
    p = {
        "emb_g": emb_ln_g.reshape(1, D_MODEL), "emb_b": emb_ln_b.reshape(1, D_MODEL),
        "w_qkv": w_qkv,
        "q_gain": _deinterleave_gain(q_norm_g[l]), "k_gain": _deinterleave_gain(k_norm_g[l]),
        "sink": sink_a[l].astype(F32), "bias_a": _window_bias(rel_bias),
        "w_a": w_branch_a[l].astype(BF16), "w_b": w_branch_b[l].astype(BF16),
        "w_g": w_in_l[:, vb_end:].astype(BF16), "b_g": b_gate[l].reshape(1, 2 * D_MODEL),
        "w_o": w_out[l].astype(BF16),
        "ln1_g": ln1_g[l].reshape(1, D_MODEL), "ln1_b": ln1_b[l].reshape(1, D_MODEL),
        "wr_hi": wr_hi, "wr_lo": wr_lo, "b_r": b_router,
        "w_gate": w_gate[l], "w_up": w_up[l], "w_down": w_down[l],
        "ln2_g": ln2_g[l].reshape(1, D_MODEL), "ln2_b": ln2_b[l].reshape(1, D_MODEL),
    }
    return _layer(x_prompt, x_sample, p)
```

```python
import functools
import math

import jax
import jax.numpy as jnp
from jax import lax
from jax.experimental import pallas as pl
from jax.experimental.pallas import tpu as pltpu

F32 = jnp.float32
BF16 = jnp.bfloat16

D_MODEL = 2048
HEAD_DIM = 128
N_HEADS = 8
N_KV = 2
GROUP = N_HEADS // N_KV
Q_COLS = N_HEADS * HEAD_DIM
KV_COLS = N_KV * HEAD_DIM
QKV_COLS = 2 * (Q_COLS + 2 * KV_COLS)
WINDOW = 128
BLOCK = 128
GRID_W = 64
ROPE_THETA = 10000.0
N_BUCKETS = 32
MAX_DISTANCE = 128
N_GROUPS = 4
EXPERTS_PER_GROUP = 8
N_EXPERTS = N_GROUPS * EXPERTS_PER_GROUP
TOP_K = 2
D_EXPERT = 512
LN_EPS = 1e-5
RMS_EPS = 1e-6
DEPTH = 1
ALPHA = (2 * DEPTH) ** 0.25
SCALE = HEAD_DIM ** -0.5
LOG2E = math.log2(math.e)
NEG_BIG = -1e30

LANES = 128
VMEM_LIMIT = 56 * 1024 * 1024

QKV_TM = 512
ATT_A_TQ = 512
ATT_B_TQ = 256
ATT_B_TK = 512
MERGE_TM = 256
MERGE_TC = 1024
ROUTE_TN = 1024
MOE_RB = 256
FINAL_TM = 512
ROUTER_ROWS = 128
FINE_ROW0 = 8


def _layer_norm_rows(x, g, b):
    mu = jnp.mean(x, axis=-1, keepdims=True)
    xc = x - mu
    var = jnp.mean(xc * xc, axis=-1, keepdims=True)
    return xc * lax.rsqrt(var + LN_EPS) * g + b


def _dot(a, b):
    return jnp.dot(a, b, preferred_element_type=F32)


def _dot_nt(a, b):
    return lax.dot_general(a, b, (((1,), (1,)), ((), ())), preferred_element_type=F32)


def _qkv_kernel(x_ref, g_ref, b_ref, w_ref, cos_ref, sin_ref, qg_ref, kg_ref,
                xn_ref, qa_ref, ka_ref, va_ref, qb_ref, kb_ref, vb_ref):
    xn = _layer_norm_rows(x_ref[...], g_ref[...], b_ref[...])
    xn_ref[...] = xn
    xb = xn.astype(BF16)
    cos = cos_ref[...]
    sin = sin_ref[...]

    def proj(c0, width):
        return _dot(xb, w_ref[:, c0:c0 + width])

    def norm_rope(h, gain):
        y = h * lax.rsqrt(jnp.mean(h * h, axis=-1, keepdims=True) + RMS_EPS) * gain
        return y * cos + pltpu.roll(y, HEAD_DIM // 2, 1) * sin

    c = Q_COLS + 2 * KV_COLS
    qg = qg_ref[...]
    for hp in range(N_HEADS // 2):
        hq2 = proj(c + 2 * hp * HEAD_DIM, 2 * HEAD_DIM)
        for h in (2 * hp, 2 * hp + 1):
            hq = hq2[:, (h - 2 * hp) * HEAD_DIM:(h - 2 * hp + 1) * HEAD_DIM]
            qb_ref[:, h * HEAD_DIM:(h + 1) * HEAD_DIM] = (norm_rope(hq, qg) * (SCALE * LOG2E)).astype(BF16)
    c += Q_COLS
    kg = kg_ref[...]
    hk2 = proj(c, KV_COLS)
    for h in range(N_KV):
        kb_ref[:, h * HEAD_DIM:(h + 1) * HEAD_DIM] = norm_rope(hk2[:, h * HEAD_DIM:(h + 1) * HEAD_DIM], kg).astype(BF16)
    c += KV_COLS
    hv2 = proj(c, KV_COLS)
    ones = jnp.ones((x_ref.shape[0], HEAD_DIM), BF16)
    for h in range(N_KV):
        vb_ref[:, 2 * h * HEAD_DIM:(2 * h + 1) * HEAD_DIM] = hv2[:, h * HEAD_DIM:(h + 1) * HEAD_DIM].astype(BF16)
        vb_ref[:, (2 * h + 1) * HEAD_DIM:(2 * h + 2) * HEAD_DIM] = ones
    c = 0
    qa_ref[...] = (proj(c, Q_COLS) * (SCALE * LOG2E)).astype(BF16)
    c += Q_COLS
    ka_ref[...] = proj(c, KV_COLS).astype(BF16)
    c += KV_COLS
    va_ref[...] = proj(c, KV_COLS).astype(BF16)


def _qkv_call(x2, g, b, w_qkv, cos_t, sin_t, qg, kg, seq_len):
    n = x2.shape[0]
    tm = QKV_TM
    pos_blocks = seq_len // tm
    row = lambda i: (i, 0)
    fixed = lambda i: (0, 0)
    pos = lambda i: (i % pos_blocks, 0)
    out_shapes = (
        jax.ShapeDtypeStruct((n, D_MODEL), F32),
        jax.ShapeDtypeStruct((n, Q_COLS), BF16),
        jax.ShapeDtypeStruct((n, KV_COLS), BF16),
        jax.ShapeDtypeStruct((n, KV_COLS), BF16),
        jax.ShapeDtypeStruct((n, Q_COLS), BF16),
        jax.ShapeDtypeStruct((n, KV_COLS), BF16),
        jax.ShapeDtypeStruct((n, 2 * KV_COLS), BF16),
    )
    return pl.pallas_call(
        _qkv_kernel,
        grid=(n // tm,),
        in_specs=[
            pl.BlockSpec((tm, D_MODEL), row),
            pl.BlockSpec((1, D_MODEL), fixed),
            pl.BlockSpec((1, D_MODEL), fixed),
            pl.BlockSpec((D_MODEL, QKV_COLS), fixed, pipeline_mode=pl.Buffered(1)),
            pl.BlockSpec((tm, HEAD_DIM), pos),
            pl.BlockSpec((tm, HEAD_DIM), pos),
            pl.BlockSpec((1, HEAD_DIM), fixed),
            pl.BlockSpec((1, HEAD_DIM), fixed),
        ],
        out_specs=(
            pl.BlockSpec((tm, D_MODEL), row),
            pl.BlockSpec((tm, Q_COLS), row),
            pl.BlockSpec((tm, KV_COLS), row),
            pl.BlockSpec((tm, KV_COLS), row),
            pl.BlockSpec((tm, Q_COLS), row),
            pl.BlockSpec((tm, KV_COLS), row),
            pl.BlockSpec((tm, 2 * KV_COLS), row),
        ),
        out_shape=out_shapes,
        compiler_params=pltpu.CompilerParams(
            dimension_semantics=("arbitrary",), vmem_limit_bytes=VMEM_LIMIT),
        name="ln_qkv",
    )(x2, g, b, w_qkv, cos_t, sin_t, qg, kg)


def _attn_a_kernel(sink_ref, q_ref, kc_ref, kp_ref, kn_ref, vc_ref, vp_ref, vn_ref, bias_ref,
                   o_ref, kcat_ref, vcat_ref, sa_ref, sb_ref, *, seq_len):
    tq = ATT_A_TQ
    i = pl.program_id(1)
    kcat_ref[0:BLOCK, :] = kp_ref[...]
    kcat_ref[BLOCK:BLOCK + tq, :] = kc_ref[...]
    kcat_ref[BLOCK + tq:, :] = kn_ref[...]
    vcat_ref[0:BLOCK, :] = vp_ref[...]
    vcat_ref[BLOCK:BLOCK + tq, :] = vc_ref[...]
    vcat_ref[BLOCK + tq:, :] = vn_ref[...]
    s_refs = (sa_ref, sb_ref)
    n_seq_blocks = seq_len // BLOCK

    def block_scores(j, s_ref):
        qj = q_ref[j * BLOCK:(j + 1) * BLOCK, :]
        for g in range(N_KV):
            qs = jnp.concatenate(
                [qj[:, h * HEAD_DIM:(h + 1) * HEAD_DIM] for h in range(g * GROUP, (g + 1) * GROUP)], axis=0)
            kw = kcat_ref[j * BLOCK:(j + 3) * BLOCK, g * HEAD_DIM:(g + 1) * HEAD_DIM]
            s_ref[g * GROUP * BLOCK:(g + 1) * GROUP * BLOCK, :] = _dot_nt(qs, kw)

    def block_out(j, s_ref):
        blk = i * (tq // BLOCK) + j
        edge = (blk == 0).astype(jnp.int32) + 2 * (blk == n_seq_blocks - 1).astype(jnp.int32)
        for g in range(N_KV):
            probs = []
            for h in range(g * GROUP, (g + 1) * GROUP):
                s = s_ref[h * BLOCK:(h + 1) * BLOCK, :] + bias_ref[edge, h]
                sk = sink_ref[h]
                m = jnp.maximum(jnp.max(s, axis=-1, keepdims=True), sk)
                p = jnp.exp2(s - m)
                denom = jnp.sum(p, axis=-1, keepdims=True) + jnp.exp2(sk - m)
                probs.append((p / denom).astype(BF16))
            vw = vcat_ref[j * BLOCK:(j + 3) * BLOCK, g * HEAD_DIM:(g + 1) * HEAD_DIM]
            o = _dot(jnp.concatenate(probs, axis=0), vw)
            for hh in range(GROUP):
                h = g * GROUP + hh
                o_ref[j * BLOCK:(j + 1) * BLOCK, h * HEAD_DIM:(h + 1) * HEAD_DIM] = (
                    o[hh * BLOCK:(hh + 1) * BLOCK, :].astype(BF16))

    n_blk = tq // BLOCK
    block_scores(0, s_refs[0])
    for j in range(n_blk):
        if j + 1 < n_blk:
            block_scores(j + 1, s_refs[(j + 1) % 2])
        block_out(j, s_refs[j % 2])


def _attn_a_call(sink, qa, ka, va, bias, bsz, seq_len):
    tq = ATT_A_TQ
    nb = seq_len // BLOCK
    bpt = tq // BLOCK
    q3 = qa.reshape(bsz, seq_len, Q_COLS)
    k3 = ka.reshape(bsz, seq_len, KV_COLS)
    v3 = va.reshape(bsz, seq_len, KV_COLS)
    cur = lambda b, i: (b, i, 0)
    prev = lambda b, i: (b, jnp.maximum(i * bpt - 1, 0), 0)
    nxt = lambda b, i: (b, jnp.minimum((i + 1) * bpt, nb - 1), 0)
    kv_cur = pl.BlockSpec((None, tq, KV_COLS), cur)
    kv_prev = pl.BlockSpec((None, BLOCK, KV_COLS), prev)
    kv_next = pl.BlockSpec((None, BLOCK, KV_COLS), nxt)
    out = pl.pallas_call(
        functools.partial(_attn_a_kernel, seq_len=seq_len),
        grid=(bsz, seq_len // tq),
        in_specs=[
            pl.BlockSpec(memory_space=pltpu.SMEM),
            pl.BlockSpec((None, tq, Q_COLS), cur),
            kv_cur, kv_prev, kv_next,
            kv_cur, kv_prev, kv_next,
            pl.BlockSpec((4, N_HEADS, BLOCK, 3 * BLOCK), lambda b, i: (0, 0, 0, 0)),
        ],
        out_specs=pl.BlockSpec((None, tq, Q_COLS), cur),
        out_shape=jax.ShapeDtypeStruct((bsz, seq_len, Q_COLS), BF16),
        scratch_shapes=[
            pltpu.VMEM((tq + 2 * BLOCK, KV_COLS), BF16),
            pltpu.VMEM((tq + 2 * BLOCK, KV_COLS), BF16),
            pltpu.VMEM((N_HEADS * BLOCK, 3 * BLOCK), F32),
            pltpu.VMEM((N_HEADS * BLOCK, 3 * BLOCK), F32),
        ],
        compiler_params=pltpu.CompilerParams(
            dimension_semantics=("arbitrary", "arbitrary"), vmem_limit_bytes=VMEM_LIMIT),
        name="attn_window",
    )(sink, q3, k3, k3, k3, v3, v3, v3, bias)
    return out.reshape(bsz * seq_len, Q_COLS)


def _attn_b_kernel(q_ref, qn_ref, k_ref, v_ref, o_ref, qs_ref, qsn_ref, m_ref, acc_ref,
                   s0_ref, s1_ref, p0_ref, p1_ref, a0_ref, a1_ref, *, seq_len):
    tq = ATT_B_TQ
    tk = ATT_B_TK
    n_chunks = seq_len // tk
    rows = GROUP * tq
    for h in range(GROUP):
        qs_ref[h * tq:(h + 1) * tq, :] = q_ref[:, h * HEAD_DIM:(h + 1) * HEAD_DIM]
        qsn_ref[h * tq:(h + 1) * tq, :] = qn_ref[:, h * HEAD_DIM:(h + 1) * HEAD_DIM]

    def scores(q_src, c, s_ref):
        k0 = pl.multiple_of(c * tk, tk)
        s_ref[...] = _dot_nt(q_src[...], k_ref[pl.ds(k0, tk), :])

    def softmax(s_ref, p_ref, a_ref, first=False):
        m_tile = s_ref[:, 0:LANES]
        for j in range(1, tk // LANES):
            m_tile = jnp.maximum(m_tile, s_ref[:, j * LANES:(j + 1) * LANES])
        m_cur = jnp.broadcast_to(jnp.max(m_tile, axis=-1, keepdims=True), (rows, LANES))
        if first:
            m_new = m_cur
            a_ref[...] = jnp.zeros((rows, LANES), F32)
        else:
            m_prev = m_ref[...]
            m_new = jnp.maximum(m_prev, m_cur)
            a_ref[...] = jnp.exp2(m_prev - m_new)
        m_ref[...] = m_new
        for j in range(tk // LANES):
            p_ref[:, j * LANES:(j + 1) * LANES] = jnp.exp2(s_ref[:, j * LANES:(j + 1) * LANES] - m_new).astype(BF16)

    def weighted_values(c, p_ref, a_ref):
        k0 = pl.multiple_of(c * tk, tk)
        pv = _dot(p_ref[...], v_ref[pl.ds(k0, tk), :])
        a = a_ref[...]
        acc_ref[:, 0:HEAD_DIM] = a * acc_ref[:, 0:HEAD_DIM] + pv[:, 0:HEAD_DIM]
        acc_ref[:, HEAD_DIM:] = a * acc_ref[:, HEAD_DIM:] + pv[:, HEAD_DIM:]

    @pl.when(pl.program_id(2) == 0)
    def _():
        scores(qs_ref, 0, s0_ref)
        scores(qs_ref, 1, s1_ref)
        softmax(s0_ref, p0_ref, a0_ref, first=True)

    acc_ref[...] = jnp.zeros(acc_ref.shape, F32)

    def body(t, carry):
        c = 2 * t + 1
        scores(qs_ref, c + 1, s0_ref)
        softmax(s1_ref, p1_ref, a1_ref)
        weighted_values(c - 1, p0_ref, a0_ref)
        scores(qs_ref, c + 2, s1_ref)
        softmax(s0_ref, p0_ref, a0_ref)
        weighted_values(c, p1_ref, a1_ref)
        return carry

    lax.fori_loop(0, (n_chunks - 2) // 2, body, 0, unroll=True)
    scores(qsn_ref, 0, s0_ref)
    softmax(s1_ref, p1_ref, a1_ref)
    weighted_values(n_chunks - 2, p0_ref, a0_ref)
    scores(qsn_ref, 1, s1_ref)
    softmax(s0_ref, p0_ref, a0_ref, first=True)
    weighted_values(n_chunks - 1, p1_ref, a1_ref)
    o = acc_ref[:, 0:HEAD_DIM] / acc_ref[:, HEAD_DIM:]
    for h in range(GROUP):
        o_ref[:, h * HEAD_DIM:(h + 1) * HEAD_DIM] = o[h * tq:(h + 1) * tq, :].astype(BF16)


def _attn_b_call(qb, kb, vb, bsz, seq_len):
    tq = ATT_B_TQ
    tk = ATT_B_TK
    assert seq_len % tk == 0 and (seq_len // tk) % 2 == 0, "dense mixer pipeline needs an even number of key chunks"
    q3 = qb.reshape(bsz, seq_len, Q_COLS)
    k3 = kb.reshape(bsz, seq_len, KV_COLS)
    v3 = vb.reshape(bsz, seq_len, 2 * KV_COLS)
    gcols = GROUP * HEAD_DIM
    n_q = seq_len // tq
    rows = GROUP * tq
    out = pl.pallas_call(
        functools.partial(_attn_b_kernel, seq_len=seq_len),
        grid=(bsz, N_KV, n_q),
        in_specs=[
            pl.BlockSpec((None, tq, gcols), lambda b, g, i: (b, i, g)),
            pl.BlockSpec((None, tq, gcols), lambda b, g, i: (b, jnp.minimum(i + 1, n_q - 1), g)),
            pl.BlockSpec((None, seq_len, HEAD_DIM), lambda b, g, i: (b, 0, g)),
            pl.BlockSpec((None, seq_len, 2 * HEAD_DIM), lambda b, g, i: (b, 0, g)),
        ],
        out_specs=pl.BlockSpec((None, tq, gcols), lambda b, g, i: (b, i, g)),
        out_shape=jax.ShapeDtypeStruct((bsz, seq_len, Q_COLS), BF16),
        scratch_shapes=[
            pltpu.VMEM((rows, HEAD_DIM), BF16),
            pltpu.VMEM((rows, HEAD_DIM), BF16),
            pltpu.VMEM((rows, LANES), F32),
            pltpu.VMEM((rows, 2 * HEAD_DIM), F32),
            pltpu.VMEM((rows, tk), F32),
            pltpu.VMEM((rows, tk), F32),
            pltpu.VMEM((rows, tk), BF16),
            pltpu.VMEM((rows, tk), BF16),
            pltpu.VMEM((rows, LANES), F32),
            pltpu.VMEM((rows, LANES), F32),
        ],
        compiler_params=pltpu.CompilerParams(
            dimension_semantics=("arbitrary", "arbitrary", "arbitrary"), vmem_limit_bytes=VMEM_LIMIT),
        name="attn_dense",
    )(q3, q3, k3, v3)
    return out.reshape(bsz * seq_len, Q_COLS)


def _merge_kernel(xn0_ref, oa0_ref, ob0_ref, xn1_ref, oa1_ref, ob1_ref, *rest, tiles0):
    i = pl.program_id(0)

    @pl.when(i < tiles0)
    def _():
        _merge_tile(xn0_ref, oa0_ref, ob0_ref, *rest)

    @pl.when(i >= tiles0)
    def _():
        _merge_tile(xn1_ref, oa1_ref, ob1_ref, *rest)


def _merge_tile(xn_ref, oa_ref, ob_ref, wa_ref, wb_ref, wg_ref, bg_ref, wo_ref,
                g_ref, b_ref, wrh_ref, wrl_ref, br_ref, x1rows_ref, lt_ref, acc_ref):
    tc = MERGE_TC
    xb = xn_ref[...].astype(BF16)
    oa = oa_ref[...]
    ob = ob_ref[...]
    for c in range(D_MODEL // tc):
        lo, hi = c * tc, (c + 1) * tc
        ga = jax.nn.sigmoid(_dot(xb, wg_ref[:, lo:hi]) + bg_ref[:, lo:hi])
        gb = jax.nn.sigmoid(_dot(xb, wg_ref[:, D_MODEL + lo:D_MODEL + hi]) + bg_ref[:, D_MODEL + lo:D_MODEL + hi])
        mixed = (ga * _dot(oa, wa_ref[:, lo:hi]) + gb * _dot(ob, wb_ref[:, lo:hi])).astype(BF16)
        part = _dot(mixed, wo_ref[lo:hi, :])
        if c == 0:
            acc_ref[...] = part
        else:
            acc_ref[...] += part
    x1 = _layer_norm_rows(ALPHA * xn_ref[...] + acc_ref[...], g_ref[...], b_ref[...])
    x1rows_ref[...] = x1.reshape(x1.shape[0], 1, D_MODEL)
    xh = x1.astype(BF16)
    xl = (x1 - xh.astype(F32)).astype(BF16)
    wrh = wrh_ref[...]
    logits = _dot(xh, wrh) + (_dot(xl, wrh) + _dot(xh, wrl_ref[...]))
    lt_ref[...] = logits.T + br_ref[...]


def _merge_call(batch0, batch1, wa, wb, wg, bg, wo, g, b, wr_hi, wr_lo, br):
    tm = MERGE_TM
    tiles0 = batch0[0].shape[0] // tm
    n = batch0[0].shape[0] + batch1[0].shape[0]
    rows0 = lambda i: (jnp.minimum(i, tiles0 - 1), 0)
    rows1 = lambda i: (jnp.maximum(i - tiles0, 0), 0)
    fixed = lambda i: (0, 0)

    def resident(shape):
        return pl.BlockSpec(shape, fixed, pipeline_mode=pl.Buffered(1))

    return pl.pallas_call(
        functools.partial(_merge_kernel, tiles0=tiles0),
        grid=(n // tm,),
        in_specs=[
            pl.BlockSpec((tm, D_MODEL), rows0),
            pl.BlockSpec((tm, Q_COLS), rows0),
            pl.BlockSpec((tm, Q_COLS), rows0),
            pl.BlockSpec((tm, D_MODEL), rows1),
            pl.BlockSpec((tm, Q_COLS), rows1),
            pl.BlockSpec((tm, Q_COLS), rows1),
            resident((Q_COLS, D_MODEL)),
            resident((Q_COLS, D_MODEL)),
            resident((D_MODEL, 2 * D_MODEL)),
            resident((1, 2 * D_MODEL)),
            resident((D_MODEL, D_MODEL)),
            resident((1, D_MODEL)),
            resident((1, D_MODEL)),
            resident((D_MODEL, ROUTER_ROWS)),
            resident((D_MODEL, ROUTER_ROWS)),
            resident((ROUTER_ROWS, 1)),
        ],
        out_specs=(
            pl.BlockSpec((tm, 1, D_MODEL), lambda i: (i, 0, 0)),
            pl.BlockSpec((ROUTER_ROWS, tm), lambda i: (0, i)),
        ),
        out_shape=(
            jax.ShapeDtypeStruct((n, 1, D_MODEL), F32),
            jax.ShapeDtypeStruct((ROUTER_ROWS, n), F32),
        ),
        scratch_shapes=[pltpu.VMEM((tm, D_MODEL), F32)],
        compiler_params=pltpu.CompilerParams(
            dimension_semantics=("arbitrary",), vmem_limit_bytes=VMEM_LIMIT),
        name="merge_ln_router",
    )(*batch0, *batch1, wa, wb, wg, bg, wo, g, b, wr_hi, wr_lo, br)


def _route_kernel(lt_ref, e_ref, w_ref, cnt_ref, carry_ref):
    tn = lt_ref.shape[1]

    @pl.when(pl.program_id(0) == 0)
    def _():
        carry_ref[...] = jnp.zeros(carry_ref.shape, F32)

    cl = lt_ref[0:N_GROUPS, :]
    gi = lax.broadcasted_iota(jnp.int32, (N_GROUPS, tn), 0)
    cmax = jnp.max(cl, axis=0, keepdims=True)
    ce = jnp.exp(cl - cmax)
    cp = ce / jnp.sum(ce, axis=0, keepdims=True)
    g_idx = jnp.min(jnp.where(cl == cmax, gi, N_GROUPS), axis=0, keepdims=True)
    g_prob = jnp.sum(jnp.where(gi == g_idx, cp, 0.0), axis=0, keepdims=True)

    fl = jnp.zeros((EXPERTS_PER_GROUP, tn), F32)
    for g in range(N_GROUPS):
        r0 = FINE_ROW0 + g * EXPERTS_PER_GROUP
        fl = jnp.where(g_idx == g, lt_ref[r0:r0 + EXPERTS_PER_GROUP, :], fl)
    fmax = jnp.max(fl, axis=0, keepdims=True)
    fe = jnp.exp(fl - fmax)
    fp = fe / jnp.sum(fe, axis=0, keepdims=True)
    ei = lax.broadcasted_iota(jnp.int32, (EXPERTS_PER_GROUP, tn), 0)
    p1 = jnp.max(fp, axis=0, keepdims=True)
    i1 = jnp.min(jnp.where(fp == p1, ei, EXPERTS_PER_GROUP), axis=0, keepdims=True)
    fp2 = jnp.where(ei == i1, -1.0, fp)
    p2 = jnp.max(fp2, axis=0, keepdims=True)
    i2 = jnp.min(jnp.where(fp2 == p2, ei, EXPERTS_PER_GROUP), axis=0, keepdims=True)
    tot = p1 + p2
    w1 = g_prob * p1 / tot
    w2 = g_prob * p2 / tot
    e1 = g_idx * EXPERTS_PER_GROUP + i1
    e2 = g_idx * EXPERTS_PER_GROUP + i2
    ri = lax.broadcasted_iota(jnp.int32, (8, tn), 0)
    e_ref[...] = jnp.where(ri == 0, e1, jnp.where(ri == 1, e2, 0))
    w_ref[...] = jnp.where(ri == 0, w1, jnp.where(ri == 1, w2, 0.0))

    xi = lax.broadcasted_iota(jnp.int32, (N_EXPERTS, tn), 0)
    hits = jnp.where(xi == e1, 1.0, 0.0) + jnp.where(xi == e2, 1.0, 0.0)
    total = carry_ref[...] + jnp.sum(hits, axis=1, keepdims=True)
    carry_ref[...] = total
    cnt_ref[...] = jnp.broadcast_to(total, cnt_ref.shape)


def _route_call(lt):
    n = lt.shape[1]
    tn = min(ROUTE_TN, n)
    tok = lambda i: (0, i)
    return pl.pallas_call(
        _route_kernel,
        grid=(n // tn,),
        in_specs=[pl.BlockSpec((ROUTER_ROWS, tn), tok)],
        out_specs=(pl.BlockSpec((8, tn), tok), pl.BlockSpec((8, tn), tok),
                   pl.BlockSpec((N_EXPERTS, LANES), lambda i: (0, 0))),
        out_shape=(jax.ShapeDtypeStruct((8, n), jnp.int32), jax.ShapeDtypeStruct((8, n), F32),
                   jax.ShapeDtypeStruct((N_EXPERTS, LANES), F32)),
        scratch_shapes=[pltpu.VMEM((N_EXPERTS, 1), F32)],
        compiler_params=pltpu.CompilerParams(dimension_semantics=("arbitrary",)),
        name="router",
    )(lt)


def _plan_kernel(starts_ref, vb_ref, ve_ref, vlo_ref, vhi_ref, *, rb):
    n_visits = vb_ref.shape[0]
    shift = rb.bit_length() - 1

    def expert_body(e, v):
        end = starts_ref[e + 1]

        def cond(state):
            return state[0] < end

        def body(state):
            r, v = state
            b = lax.shift_right_logical(r, shift)
            hi = jnp.minimum(end, (b + 1) * rb)
            vb_ref[v] = b
            ve_ref[v] = e
            vlo_ref[v] = r - b * rb
            vhi_ref[v] = hi - b * rb
            return hi, v + 1

        return lax.while_loop(cond, body, (starts_ref[e], v))[1]

    used = lax.fori_loop(0, N_EXPERTS, expert_body, 0)
    last_b = vb_ref[used - 1]
    last_e = ve_ref[used - 1]

    def pad(i, carry):
        vb_ref[i] = last_b
        ve_ref[i] = last_e
        vlo_ref[i] = 0
        vhi_ref[i] = 0
        return carry

    lax.fori_loop(used, n_visits, pad, 0)


def _plan_call(starts, n_assign, rb):
    assert rb & (rb - 1) == 0 and n_assign % rb == 0
    n_visits = n_assign // rb + N_EXPERTS - 1
    smem = pl.BlockSpec(memory_space=pltpu.SMEM)
    sds = jax.ShapeDtypeStruct((n_visits,), jnp.int32)
    return pl.pallas_call(
        functools.partial(_plan_kernel, rb=rb),
        in_specs=[smem], out_specs=(smem,) * 4, out_shape=(sds,) * 4, name="visit_plan",
    )(starts)


def _expert_kernel(vb_ref, ve_ref, vlo_ref, vhi_ref,
                   tok0_ref, tokn_ref, slot_ref, slotp_ref, x_hbm, wg_ref, wu_ref, wd_ref, y_hbm,
                   xin_ref, xf_ref, xb_ref, ys_ref, yout_ref, wgb_ref, wub_ref, wdb_ref,
                   state_ref, gsem, ssem):
    rb = xb_ref.shape[0]
    v = pl.program_id(0)
    b = vb_ref[v]
    lo = vlo_ref[v]
    hi = vhi_ref[v]
    unsent, in_flight = 0, 1

    def gather_row(idx_ref, r):
        return pltpu.make_async_copy(x_hbm.at[pl.ds(idx_ref[0, 0, r], 1)], xin_ref.at[pl.ds(r, 1)], gsem)

    def scatter_row(buf, idx_ref, r):
        return pltpu.make_async_copy(yout_ref.at[buf, pl.ds(r, 1)], y_hbm.at[pl.ds(idx_ref[0, 0, r], 1)], ssem.at[buf])

    def start_gather(idx_ref):
        def body(r, carry):
            gather_row(idx_ref, r).start()
            return carry
        lax.fori_loop(0, rb, body, 0, unroll=8)

    def wait_gather():
        def body(r, carry):
            gather_row(tok0_ref, r).wait()
            return carry
        lax.fori_loop(0, rb, body, 0, unroll=8)

    def wait_scatter(buf):
        def body(r, carry):
            scatter_row(buf, slot_ref, r).wait()
            return carry
        lax.fori_loop(0, rb, body, 0, unroll=8)
        state_ref[in_flight + buf] = 0

    def park_block(get_rows):
        buf = b & 1

        @pl.when(state_ref[in_flight + buf] == 1)
        def _():
            wait_scatter(buf)

        yout_ref[buf] = get_rows().reshape(rb, 1, D_MODEL)
        state_ref[unsent] = 1

    def mark_sent(buf):
        state_ref[unsent] = 0
        state_ref[in_flight + buf] = 1

    @pl.when(v == 0)
    def _():
        state_ref[unsent] = 0
        state_ref[in_flight] = 0
        state_ref[in_flight + 1] = 0
        start_gather(tok0_ref)

    @pl.when(hi > lo)
    def _():
        @pl.when(lo == 0)
        def _():
            wait_gather()
            xf_ref[...] = xin_ref[...].reshape(rb, D_MODEL)
            xb_ref[...] = xf_ref[...].astype(BF16)

        @pl.when((v == 0) | (ve_ref[v] != ve_ref[jnp.maximum(v - 1, 0)]))
        def _():
            wgb_ref[...] = wg_ref[...].astype(BF16)
            wub_ref[...] = wu_ref[...].astype(BF16)
            wdb_ref[...] = wd_ref[...].astype(BF16)

        def expert_rows():
            x = xb_ref[...]
            hidden = jax.nn.silu(_dot(x, wgb_ref[...])) * _dot(x, wub_ref[...])
            return _dot(hidden.astype(BF16), wdb_ref[...])

        whole = (lo == 0) & (hi == rb)
        prev_buf = (b + 1) & 1

        @pl.when(whole & (b > 0))
        def _():
            for r in range(rb):
                gather_row(tokn_ref, r).start()
                scatter_row(prev_buf, slotp_ref, r).start()
            mark_sent(prev_buf)
            y = expert_rows()
            park_block(lambda: y)

        @pl.when(whole & (b == 0))
        def _():
            for r in range(rb):
                gather_row(tokn_ref, r).start()
            y = expert_rows()
            park_block(lambda: y)

        @pl.when(jnp.logical_not(whole))
        def _():
            @pl.when(lo == 0)
            def _():
                start_gather(tokn_ref)

                @pl.when(state_ref[unsent] == 1)
                def _():
                    def body(r, carry):
                        scatter_row(prev_buf, slotp_ref, r).start()
                        return carry
                    lax.fori_loop(0, rb, body, 0, unroll=8)
                    mark_sent(prev_buf)

            y = expert_rows()
            rows = lax.broadcasted_iota(jnp.int32, (rb, 1), 0)
            mine = (rows >= lo) & (rows < hi)

            @pl.when(lo == 0)
            def _():
                ys_ref[...] = jnp.where(mine, y, 0.0)

            @pl.when(lo > 0)
            def _():
                ys_ref[...] = jnp.where(mine, y, ys_ref[...])

            @pl.when(hi == rb)
            def _():
                park_block(lambda: ys_ref[...])

    @pl.when(v == pl.num_programs(0) - 1)
    def _():
        last_buf = b & 1

        @pl.when(state_ref[unsent] == 1)
        def _():
            def body(r, carry):
                scatter_row(last_buf, slot_ref, r).start()
                return carry
            lax.fori_loop(0, rb, body, 0, unroll=8)
            mark_sent(last_buf)

        for buf in range(2):
            @pl.when(state_ref[in_flight + buf] == 1)
            def _():
                wait_scatter(buf)

        wait_gather()


def _expert_call(plan, tok_sorted, slot_sorted, x1rows, w_gate, w_up, w_down, rb):
    n_assign = tok_sorted.shape[0]
    n_blocks = n_assign // rb
    n_visits = plan[0].shape[0]
    tok3 = tok_sorted.reshape(n_blocks, 1, rb)
    slot3 = slot_sorted.reshape(n_blocks, 1, rb)
    w_map = lambda v, vb, ve, vlo, vhi: (ve[v], 0, 0)
    idx_block = (1, 1, rb)
    grid_spec = pltpu.PrefetchScalarGridSpec(
        num_scalar_prefetch=4,
        grid=(n_visits,),
        in_specs=[
            pl.BlockSpec(idx_block, lambda v, vb, ve, vlo, vhi: (0, 0, 0), memory_space=pltpu.SMEM),
            pl.BlockSpec(idx_block, lambda v, vb, ve, vlo, vhi: (jnp.minimum(vb[v] + 1, n_blocks - 1), 0, 0),
                         memory_space=pltpu.SMEM),
            pl.BlockSpec(idx_block, lambda v, vb, ve, vlo, vhi: (vb[v], 0, 0), memory_space=pltpu.SMEM),
            pl.BlockSpec(idx_block, lambda v, vb, ve, vlo, vhi: (jnp.maximum(vb[v] - 1, 0), 0, 0),
                         memory_space=pltpu.SMEM),
            pl.BlockSpec(memory_space=pl.ANY),
            pl.BlockSpec((None, D_MODEL, D_EXPERT), w_map),
            pl.BlockSpec((None, D_MODEL, D_EXPERT), w_map),
            pl.BlockSpec((None, D_EXPERT, D_MODEL), w_map),
        ],
        out_specs=pl.BlockSpec(memory_space=pl.ANY),
        scratch_shapes=[
            pltpu.VMEM((rb, 1, D_MODEL), F32),
            pltpu.VMEM((rb, D_MODEL), F32),
            pltpu.VMEM((rb, D_MODEL), BF16),
            pltpu.VMEM((rb, D_MODEL), F32),
            pltpu.VMEM((2, rb, 1, D_MODEL), F32),
            pltpu.VMEM((D_MODEL, D_EXPERT), BF16),
            pltpu.VMEM((D_MODEL, D_EXPERT), BF16),
            pltpu.VMEM((D_EXPERT, D_MODEL), BF16),
            pltpu.SMEM((3,), jnp.int32),
            pltpu.SemaphoreType.DMA(()),
            pltpu.SemaphoreType.DMA((2,)),
        ],
    )
    return pl.pallas_call(
        _expert_kernel,
        grid_spec=grid_spec,
        out_shape=jax.ShapeDtypeStruct((n_assign, 1, D_MODEL), F32),
        compiler_params=pltpu.CompilerParams(
            dimension_semantics=("arbitrary",), vmem_limit_bytes=VMEM_LIMIT),
        name="experts",
    )(*plan, tok3, tok3, slot3, slot3, x1rows, w_gate, w_up, w_down)


def _final_kernel(x_ref, y0_ref, y1_ref, w_ref, g_ref, b_ref, o_ref, xs_ref, ya_ref, yb_ref):
    tm = o_ref.shape[0]
    xs_ref[...] = x_ref[...].reshape(tm, D_MODEL)
    ya_ref[...] = y0_ref[...].reshape(tm, D_MODEL)
    yb_ref[...] = y1_ref[...].reshape(tm, D_MODEL)
    w = w_ref[...]
    moe = ya_ref[...] * w[:, 0:1] + yb_ref[...] * w[:, 1:2]
    o_ref[...] = _layer_norm_rows(ALPHA * xs_ref[...] + moe, g_ref[...], b_ref[...])


def _final_call(x1rows, y2, w_tok, g, b, row0, n):
    tm = FINAL_TM
    assert row0 % tm == 0 and n % tm == 0
    t0 = row0 // tm
    fixed = lambda i: (0, 0)
    return pl.pallas_call(
        _final_kernel,
        grid=(n // tm,),
        in_specs=[
            pl.BlockSpec((tm, 1, D_MODEL), lambda i: (t0 + i, 0, 0)),
            pl.BlockSpec((None, tm, 1, D_MODEL), lambda i: (0, t0 + i, 0, 0)),
            pl.BlockSpec((None, tm, 1, D_MODEL), lambda i: (1, t0 + i, 0, 0)),
            pl.BlockSpec((tm, TOP_K), lambda i: (t0 + i, 0)),
            pl.BlockSpec((1, D_MODEL), fixed),
            pl.BlockSpec((1, D_MODEL), fixed),
        ],
        out_specs=pl.BlockSpec((tm, D_MODEL), lambda i: (i, 0)),
        out_shape=jax.ShapeDtypeStruct((n, D_MODEL), F32),
        scratch_shapes=[pltpu.VMEM((tm, D_MODEL), F32)] * 3,
        compiler_params=pltpu.CompilerParams(
            dimension_semantics=("arbitrary",), vmem_limit_bytes=VMEM_LIMIT),
        name="combine_ln",
    )(x1rows, y2, y2, w_tok, g, b)


def _t5_bucket(rel):
    half = N_BUCKETS // 2
    exact = half // 2
    n = jnp.abs(rel)
    nf = jnp.maximum(n, 1).astype(F32)
    large = exact + (jnp.log(nf / exact) / math.log(MAX_DISTANCE / exact) * (half - exact)).astype(jnp.int32)
    large = jnp.minimum(large, half - 1)
    return jnp.where(rel > 0, half, 0) + jnp.where(n < exact, n, large)


def _window_bias(rel_bias):
    rel = jnp.arange(3 * BLOCK)[None, :] - BLOCK - jnp.arange(BLOCK)[:, None]
    onehot = (_t5_bucket(rel)[..., None] == jnp.arange(N_BUCKETS)).astype(F32)
    bias = jnp.einsum("qkb,bh->hqk", onehot, rel_bias.astype(F32), precision=lax.Precision.HIGHEST)
    band = jnp.abs(rel) <= WINDOW
    col = jnp.arange(3 * BLOCK)
    tables = []
    for edge in range(4):
        keep = band & ((col >= BLOCK) | (edge & 1 == 0))[None, :] & ((col < 2 * BLOCK) | (edge & 2 == 0))[None, :]
        tables.append(jnp.where(keep[None], bias * LOG2E, NEG_BIG))
    return jnp.stack(tables)


def _rope_tables(seq_len):
    rows = seq_len // GRID_W
    row_ids = jnp.repeat(jnp.arange(rows), GRID_W).astype(F32)
    col_ids = jnp.tile(jnp.arange(GRID_W), rows).astype(F32)
    half = HEAD_DIM // 2
    inv = 1.0 / (ROPE_THETA ** (jnp.arange(0, half, 2, dtype=F32) / half))
    ang = jnp.concatenate([row_ids[:, None] * inv, col_ids[:, None] * inv], -1)
    cos, sin = jnp.cos(ang), jnp.sin(ang)
    return jnp.concatenate([cos, cos], -1), jnp.concatenate([-sin, sin], -1)


def _deinterleave_cols(w, n_heads):
    d = w.shape[0]
    return w.reshape(d, n_heads, HEAD_DIM // 2, 2).transpose(0, 1, 3, 2).reshape(d, n_heads * HEAD_DIM)


def _deinterleave_gain(g):
    return g.reshape(HEAD_DIM // 2, 2).T.reshape(1, HEAD_DIM)


def _expert_row_block(n_assign):
    return MOE_RB if n_assign // N_EXPERTS >= 8 * MOE_RB else MOE_RB // 2


def _mixers(x, p):
    bsz, seq_len, d = x.shape
    cos_t, sin_t = _rope_tables(seq_len)
    xn, qa, ka, va, qb, kb, vb = _qkv_call(x.reshape(bsz * seq_len, d), p["emb_g"], p["emb_b"], p["w_qkv"],
                                           cos_t, sin_t, p["q_gain"], p["k_gain"], seq_len)
    oa = _attn_a_call(p["sink"], qa, ka, va, p["bias_a"], bsz, seq_len)
    ob = _attn_b_call(qb, kb, vb, bsz, seq_len)
    return xn, oa, ob


def _layer(x0, x1, p):
    d = x0.shape[-1]
    n0 = x0.shape[0] * x0.shape[1]
    n1 = x1.shape[0] * x1.shape[1]
    n = n0 + n1
    n_assign = n * TOP_K
    x1rows, lt = _merge_call(_mixers(x0, p), _mixers(x1, p), p["w_a"], p["w_b"], p["w_g"], p["b_g"], p["w_o"],
                             p["ln1_g"], p["ln1_b"], p["wr_hi"], p["wr_lo"], p["b_r"])
    e_rows, w_rows, cnt = _route_call(lt)
    e_flat = e_rows[:TOP_K].reshape(n_assign)
    assert N_EXPERTS * n_assign < 2 ** 31
    slot_sorted = lax.rem(lax.sort(e_flat * n_assign + jnp.arange(n_assign, dtype=jnp.int32)), n_assign)
    tok_sorted = jnp.where(slot_sorted >= n, slot_sorted - n, slot_sorted)
    ends = jnp.cumsum(cnt[:, 0].astype(jnp.int32))
    starts = jnp.concatenate([jnp.zeros((1,), jnp.int32), ends])
    rb = _expert_row_block(n_assign)
    plan = _plan_call(starts, n_assign, rb)
    y2 = _expert_call(plan, tok_sorted, slot_sorted, x1rows, p["w_gate"], p["w_up"], p["w_down"], rb)
    y2 = y2.reshape(TOP_K, n, 1, d)
    w_tok = w_rows[:TOP_K].T
    out0 = _final_call(x1rows, y2, w_tok, p["ln2_g"], p["ln2_b"], 0, n0)
    out1 = _final_call(x1rows, y2, w_tok, p["ln2_g"], p["ln2_b"], n0, n1)
    return out0.reshape(x0.shape), out1.reshape(x1.shape)


def kernel(x_prompt, x_sample, emb_ln_g, emb_ln_b, rel_bias, w_in, b_gate, sink_a, q_norm_g, k_norm_g,
           w_branch_a, w_branch_b, w_out, ln1_g, ln1_b, w_coarse, b_coarse, w_fine, b_fine,
           w_gate, w_up, w_down, ln2_g, ln2_b):
    l = 0
    w_in_l = w_in[l]
    qa_end = Q_COLS
    kva_end = qa_end + 2 * KV_COLS
    qb_end = kva_end + Q_COLS
    kb_end = qb_end + KV_COLS
    vb_end = kb_end + KV_COLS
    w_qkv = jnp.concatenate([
        w_in_l[:, :kva_end],
        _deinterleave_cols(w_in_l[:, kva_end:qb_end], N_HEADS),
        _deinterleave_cols(w_in_l[:, qb_end:kb_end], N_KV),
        w_in_l[:, kb_end:vb_end],
    ], axis=1).astype(BF16)
    w_router = jnp.zeros((D_MODEL, ROUTER_ROWS), F32)
    w_router = w_router.at[:, 0:N_GROUPS].set(w_coarse[l])
    w_router = w_router.at[:, FINE_ROW0:FINE_ROW0 + N_EXPERTS].set(w_fine[l])
    wr_hi = w_router.astype(BF16)
    wr_lo = (w_router - wr_hi.astype(F32)).astype(BF16)
    b_router = jnp.zeros((ROUTER_ROWS, 1), F32)
    b_router = b_router.at[0:N_GROUPS, 0].set(b_coarse[l].astype(F32))
    b_router = b_router.at[FINE_ROW0:FINE_ROW0 + N_EXPERTS, 0].set(b_fine[l].astype(F32))
    p = {
        "emb_g": emb_ln_g.reshape(1, D_MODEL), "emb_b": emb_ln_b.reshape(1, D_MODEL),
        "w_qkv": w_qkv,
        "q_gain": _deinterleave_gain(q_norm_g[l]), "k_gain": _deinterleave_gain(k_norm_g[l]),
        "sink": sink_a[l].astype(F32) * LOG2E, "bias_a": _window_bias(rel_bias),
        "w_a": w_branch_a[l].astype(BF16), "w_b": w_branch_b[l].astype(BF16),
        "w_g": w_in_l[:, vb_end:].astype(BF16), "b_g": b_gate[l].reshape(1, 2 * D_MODEL),
        "w_o": w_out[l].astype(BF16),
        "ln1_g": ln1_g[l].reshape(1, D_MODEL), "ln1_b": ln1_b[l].reshape(1, D_MODEL),
        "wr_hi": wr_hi, "wr_lo": wr_lo, "b_r": b_router,
        "w_gate": w_gate[l], "w_up": w_up[l], "w_down": w_down[l],
        "ln2_g": ln2_g[l].reshape(1, D_MODEL), "ln2_b": ln2_b[l].reshape(1, D_MODEL),
    }
    return _layer(x_prompt, x_sample, p)
```

```python
import functools
import math

import jax
import jax.numpy as jnp
from jax import lax
from jax.experimental import pallas as pl
from jax.experimental.pallas import tpu as pltpu

F32 = jnp.float32
BF16 = jnp.bfloat16

D_MODEL = 2048
HEAD_DIM = 128
N_HEADS = 8
N_KV = 2
GROUP = N_HEADS // N_KV
Q_COLS = N_HEADS * HEAD_DIM
KV_COLS = N_KV * HEAD_DIM
QKV_COLS = 2 * (Q_COLS + 2 * KV_COLS)
WINDOW = 128
BLOCK = 128
GRID_W = 64
ROPE_THETA = 10000.0
N_BUCKETS = 32
MAX_DISTANCE = 128
N_GROUPS = 4
EXPERTS_PER_GROUP = 8
N_EXPERTS = N_GROUPS * EXPERTS_PER_GROUP
TOP_K = 2
D_EXPERT = 512
LN_EPS = 1e-5
RMS_EPS = 1e-6
DEPTH = 1
ALPHA = (2 * DEPTH) ** 0.25
SCALE = HEAD_DIM ** -0.5
LOG2E = math.log2(math.e)
NEG_BIG = -1e30

LANES = 128
VMEM_LIMIT = 56 * 1024 * 1024

QKV_TM = 512
ATT_A_TQ = 512
ATT_B_TQ = 256
ATT_B_TK = 512
MERGE_TM = 256
MERGE_TC = 1024
ROUTE_TN = 1024
MOE_RB = 256
FINAL_TM = 512
ROUTER_ROWS = 128
FINE_ROW0 = 8


def _layer_norm_rows(x, g, b):
    mu = jnp.mean(x, axis=-1, keepdims=True)
    xc = x - mu
    var = jnp.mean(xc * xc, axis=-1, keepdims=True)
    return xc * lax.rsqrt(var + LN_EPS) * g + b


def _dot(a, b):
    return jnp.dot(a, b, preferred_element_type=F32)


def _dot_nt(a, b):
    return lax.dot_general(a, b, (((1,), (1,)), ((), ())), preferred_element_type=F32)


def _qkv_kernel(x_ref, g_ref, b_ref, w_ref, cos_ref, sin_ref, qg_ref, kg_ref,
                xn_ref, qa_ref, ka_ref, va_ref, qb_ref, kb_ref, vb_ref):
    xn = _layer_norm_rows(x_ref[...], g_ref[...], b_ref[...])
    xn_ref[...] = xn
    xb = xn.astype(BF16)
    cos = cos_ref[...]
    sin = sin_ref[...]

    def proj(c0, width):
        return _dot(xb, w_ref[:, c0:c0 + width])

    def norm_rope(h, gain):
        y = h * lax.rsqrt(jnp.mean(h * h, axis=-1, keepdims=True) + RMS_EPS) * gain
        return y * cos + pltpu.roll(y, HEAD_DIM // 2, 1) * sin

    c = Q_COLS + 2 * KV_COLS
    qg = qg_ref[...]
    for hp in range(N_HEADS // 2):
        hq2 = proj(c + 2 * hp * HEAD_DIM, 2 * HEAD_DIM)
        for h in (2 * hp, 2 * hp + 1):
            hq = hq2[:, (h - 2 * hp) * HEAD_DIM:(h - 2 * hp + 1) * HEAD_DIM]
            qb_ref[:, h * HEAD_DIM:(h + 1) * HEAD_DIM] = (norm_rope(hq, qg) * (SCALE * LOG2E)).astype(BF16)
    c += Q_COLS
    kg = kg_ref[...]
    hk2 = proj(c, KV_COLS)
    for h in range(N_KV):
        kb_ref[:, h * HEAD_DIM:(h + 1) * HEAD_DIM] = norm_rope(hk2[:, h * HEAD_DIM:(h + 1) * HEAD_DIM], kg).astype(BF16)
    c += KV_COLS
    hv2 = proj(c, KV_COLS)
    ones = jnp.ones((x_ref.shape[0], HEAD_DIM), BF16)
    for h in range(N_KV):
        vb_ref[:, 2 * h * HEAD_DIM:(2 * h + 1) * HEAD_DIM] = hv2[:, h * HEAD_DIM:(h + 1) * HEAD_DIM].astype(BF16)
        vb_ref[:, (2 * h + 1) * HEAD_DIM:(2 * h + 2) * HEAD_DIM] = ones
    c = 0
    qa_ref[...] = (proj(c, Q_COLS) * (SCALE * LOG2E)).astype(BF16)
    c += Q_COLS
    ka_ref[...] = proj(c, KV_COLS).astype(BF16)
    c += KV_COLS
    va_ref[...] = proj(c, KV_COLS).astype(BF16)


def _qkv_call(x2, g, b, w_qkv, cos_t, sin_t, qg, kg, seq_len):
    n = x2.shape[0]
    tm = QKV_TM
    pos_blocks = seq_len // tm
    row = lambda i: (i, 0)
    fixed = lambda i: (0, 0)
    pos = lambda i: (i % pos_blocks, 0)
    out_shapes = (
        jax.ShapeDtypeStruct((n, D_MODEL), F32),
        jax.ShapeDtypeStruct((n, Q_COLS), BF16),
        jax.ShapeDtypeStruct((n, KV_COLS), BF16),
        jax.ShapeDtypeStruct((n, KV_COLS), BF16),
        jax.ShapeDtypeStruct((n, Q_COLS), BF16),
        jax.ShapeDtypeStruct((n, KV_COLS), BF16),
        jax.ShapeDtypeStruct((n, 2 * KV_COLS), BF16),
    )
    return pl.pallas_call(
        _qkv_kernel,
        grid=(n // tm,),
        in_specs=[
            pl.BlockSpec((tm, D_MODEL), row),
            pl.BlockSpec((1, D_MODEL), fixed),
            pl.BlockSpec((1, D_MODEL), fixed),
            pl.BlockSpec((D_MODEL, QKV_COLS), fixed, pipeline_mode=pl.Buffered(1)),
            pl.BlockSpec((tm, HEAD_DIM), pos),
            pl.BlockSpec((tm, HEAD_DIM), pos),
            pl.BlockSpec((1, HEAD_DIM), fixed),
            pl.BlockSpec((1, HEAD_DIM), fixed),
        ],
        out_specs=(
            pl.BlockSpec((tm, D_MODEL), row),
            pl.BlockSpec((tm, Q_COLS), row),
            pl.BlockSpec((tm, KV_COLS), row),
            pl.BlockSpec((tm, KV_COLS), row),
            pl.BlockSpec((tm, Q_COLS), row),
            pl.BlockSpec((tm, KV_COLS), row),
            pl.BlockSpec((tm, 2 * KV_COLS), row),
        ),
        out_shape=out_shapes,
        compiler_params=pltpu.CompilerParams(
            dimension_semantics=("arbitrary",), vmem_limit_bytes=VMEM_LIMIT),
        name="ln_qkv",
    )(x2, g, b, w_qkv, cos_t, sin_t, qg, kg)


def _attn_a_kernel(sink_ref, q_ref, kc_ref, kp_ref, kn_ref, vc_ref, vp_ref, vn_ref, bias_ref,
                   o_ref, kcat_ref, vcat_ref, sa_ref, sb_ref, *, seq_len):
    tq = ATT_A_TQ
    i = pl.program_id(1)
    kcat_ref[0:BLOCK, :] = kp_ref[...]
    kcat_ref[BLOCK:BLOCK + tq, :] = kc_ref[...]
    kcat_ref[BLOCK + tq:, :] = kn_ref[...]
    vcat_ref[0:BLOCK, :] = vp_ref[...]
    vcat_ref[BLOCK:BLOCK + tq, :] = vc_ref[...]
    vcat_ref[BLOCK + tq:, :] = vn_ref[...]
    s_refs = (sa_ref, sb_ref)
    n_seq_blocks = seq_len // BLOCK

    def block_scores(j, s_ref):
        qj = q_ref[j * BLOCK:(j + 1) * BLOCK, :]
        for g in range(N_KV):
            qs = jnp.concatenate(
                [qj[:, h * HEAD_DIM:(h + 1) * HEAD_DIM] for h in range(g * GROUP, (g + 1) * GROUP)], axis=0)
            kw = kcat_ref[j * BLOCK:(j + 3) * BLOCK, g * HEAD_DIM:(g + 1) * HEAD_DIM]
            s_ref[g * GROUP * BLOCK:(g + 1) * GROUP * BLOCK, :] = _dot_nt(qs, kw)

    def block_out(j, s_ref):
        blk = i * (tq // BLOCK) + j
        edge = (blk == 0).astype(jnp.int32) + 2 * (blk == n_seq_blocks - 1).astype(jnp.int32)
        for g in range(N_KV):
            probs = []
            for h in range(g * GROUP, (g + 1) * GROUP):
                s = s_ref[h * BLOCK:(h + 1) * BLOCK, :] + bias_ref[edge, h]
                sk = sink_ref[h]
                m = jnp.maximum(jnp.max(s, axis=-1, keepdims=True), sk)
                p = jnp.exp2(s - m)
                denom = jnp.sum(p, axis=-1, keepdims=True) + jnp.exp2(sk - m)
                probs.append((p / denom).astype(BF16))
            vw = vcat_ref[j * BLOCK:(j + 3) * BLOCK, g * HEAD_DIM:(g + 1) * HEAD_DIM]
            o = _dot(jnp.concatenate(probs, axis=0), vw)
            for hh in range(GROUP):
                h = g * GROUP + hh
                o_ref[j * BLOCK:(j + 1) * BLOCK, h * HEAD_DIM:(h + 1) * HEAD_DIM] = (
                    o[hh * BLOCK:(hh + 1) * BLOCK, :].astype(BF16))

    n_blk = tq // BLOCK
    block_scores(0, s_refs[0])
    for j in range(n_blk):
        if j + 1 < n_blk:
            block_scores(j + 1, s_refs[(j + 1) % 2])
        block_out(j, s_refs[j % 2])


def _attn_a_call(sink, qa, ka, va, bias, bsz, seq_len):
    tq = ATT_A_TQ
    nb = seq_len // BLOCK
    bpt = tq // BLOCK
    q3 = qa.reshape(bsz, seq_len, Q_COLS)
    k3 = ka.reshape(bsz, seq_len, KV_COLS)
    v3 = va.reshape(bsz, seq_len, KV_COLS)
    cur = lambda b, i: (b, i, 0)
    prev = lambda b, i: (b, jnp.maximum(i * bpt - 1, 0), 0)
    nxt = lambda b, i: (b, jnp.minimum((i + 1) * bpt, nb - 1), 0)
    kv_cur = pl.BlockSpec((None, tq, KV_COLS), cur)
    kv_prev = pl.BlockSpec((None, BLOCK, KV_COLS), prev)
    kv_next = pl.BlockSpec((None, BLOCK, KV_COLS), nxt)
    out = pl.pallas_call(
        functools.partial(_attn_a_kernel, seq_len=seq_len),
        grid=(bsz, seq_len // tq),
        in_specs=[
            pl.BlockSpec(memory_space=pltpu.SMEM),
            pl.BlockSpec((None, tq, Q_COLS), cur),
            kv_cur, kv_prev, kv_next,
            kv_cur, kv_prev, kv_next,
            pl.BlockSpec((4, N_HEADS, BLOCK, 3 * BLOCK), lambda b, i: (0, 0, 0, 0)),
        ],
        out_specs=pl.BlockSpec((None, tq, Q_COLS), cur),
        out_shape=jax.ShapeDtypeStruct((bsz, seq_len, Q_COLS), BF16),
        scratch_shapes=[
            pltpu.VMEM((tq + 2 * BLOCK, KV_COLS), BF16),
            pltpu.VMEM((tq + 2 * BLOCK, KV_COLS), BF16),
            pltpu.VMEM((N_HEADS * BLOCK, 3 * BLOCK), F32),
            pltpu.VMEM((N_HEADS * BLOCK, 3 * BLOCK), F32),
        ],
        compiler_params=pltpu.CompilerParams(
            dimension_semantics=("arbitrary", "arbitrary"), vmem_limit_bytes=VMEM_LIMIT),
        name="attn_window",
    )(sink, q3, k3, k3, k3, v3, v3, v3, bias)
    return out.reshape(bsz * seq_len, Q_COLS)


def _attn_b_kernel(q_ref, qn_ref, k_ref, v_ref, o_ref, qs_ref, qsn_ref, m_ref, acc_ref,
                   s0_ref, s1_ref, p0_ref, p1_ref, a0_ref, a1_ref, *, seq_len):
    tq = ATT_B_TQ
    tk = ATT_B_TK
    n_chunks = seq_len // tk
    rows = GROUP * tq
    for h in range(GROUP):
        qs_ref[h * tq:(h + 1) * tq, :] = q_ref[:, h * HEAD_DIM:(h + 1) * HEAD_DIM]
        qsn_ref[h * tq:(h + 1) * tq, :] = qn_ref[:, h * HEAD_DIM:(h + 1) * HEAD_DIM]

    def scores(q_src, c, s_ref):
        k0 = pl.multiple_of(c * tk, tk)
        s_ref[...] = _dot_nt(q_src[...], k_ref[pl.ds(k0, tk), :])

    def softmax(s_ref, p_ref, a_ref, first=False):
        m_tile = s_ref[:, 0:LANES]
        for j in range(1, tk // LANES):
            m_tile = jnp.maximum(m_tile, s_ref[:, j * LANES:(j + 1) * LANES])
        m_cur = jnp.broadcast_to(jnp.max(m_tile, axis=-1, keepdims=True), (rows, LANES))
        if first:
            m_new = m_cur
            a_ref[...] = jnp.zeros((rows, LANES), F32)
        else:
            m_prev = m_ref[...]
            m_new = jnp.maximum(m_prev, m_cur)
            a_ref[...] = jnp.exp2(m_prev - m_new)
        m_ref[...] = m_new
        for j in range(tk // LANES):
            p_ref[:, j * LANES:(j + 1) * LANES] = jnp.exp2(s_ref[:, j * LANES:(j + 1) * LANES] - m_new).astype(BF16)

    def weighted_values(c, p_ref, a_ref):
        k0 = pl.multiple_of(c * tk, tk)
        pv = _dot(p_ref[...], v_ref[pl.ds(k0, tk), :])
        a = a_ref[...]
        acc_ref[:, 0:HEAD_DIM] = a * acc_ref[:, 0:HEAD_DIM] + pv[:, 0:HEAD_DIM]
        acc_ref[:, HEAD_DIM:] = a * acc_ref[:, HEAD_DIM:] + pv[:, HEAD_DIM:]

    @pl.when(pl.program_id(2) == 0)
    def _():
        scores(qs_ref, 0, s0_ref)
        scores(qs_ref, 1, s1_ref)
        softmax(s0_ref, p0_ref, a0_ref, first=True)

    acc_ref[...] = jnp.zeros(acc_ref.shape, F32)

    def body(t, carry):
        c = 2 * t + 1
        scores(qs_ref, c + 1, s0_ref)
        softmax(s1_ref, p1_ref, a1_ref)
        weighted_values(c - 1, p0_ref, a0_ref)
        scores(qs_ref, c + 2, s1_ref)
        softmax(s0_ref, p0_ref, a0_ref)
        weighted_values(c, p1_ref, a1_ref)
        return carry

    lax.fori_loop(0, (n_chunks - 2) // 2, body, 0, unroll=True)
    scores(qsn_ref, 0, s0_ref)
    softmax(s1_ref, p1_ref, a1_ref)
    weighted_values(n_chunks - 2, p0_ref, a0_ref)
    scores(qsn_ref, 1, s1_ref)
    softmax(s0_ref, p0_ref, a0_ref, first=True)
    weighted_values(n_chunks - 1, p1_ref, a1_ref)
    o = acc_ref[:, 0:HEAD_DIM] / acc_ref[:, HEAD_DIM:]
    for h in range(GROUP):
        o_ref[:, h * HEAD_DIM:(h + 1) * HEAD_DIM] = o[h * tq:(h + 1) * tq, :].astype(BF16)


def _attn_b_call(qb, kb, vb, bsz, seq_len):
    tq = ATT_B_TQ
    tk = ATT_B_TK
    assert seq_len % tk == 0 and (seq_len // tk) % 2 == 0, "dense mixer pipeline needs an even number of key chunks"
    q3 = qb.reshape(bsz, seq_len, Q_COLS)
    k3 = kb.reshape(bsz, seq_len, KV_COLS)
    v3 = vb.reshape(bsz, seq_len, 2 * KV_COLS)
    gcols = GROUP * HEAD_DIM
    n_q = seq_len // tq
    rows = GROUP * tq
    out = pl.pallas_call(
        functools.partial(_attn_b_kernel, seq_len=seq_len),
        grid=(bsz, N_KV, n_q),
        in_specs=[
            pl.BlockSpec((None, tq, gcols), lambda b, g, i: (b, i, g)),
            pl.BlockSpec((None, tq, gcols), lambda b, g, i: (b, jnp.minimum(i + 1, n_q - 1), g)),
            pl.BlockSpec((None, seq_len, HEAD_DIM), lambda b, g, i: (b, 0, g)),
            pl.BlockSpec((None, seq_len, 2 * HEAD_DIM), lambda b, g, i: (b, 0, g)),
        ],
        out_specs=pl.BlockSpec((None, tq, gcols), lambda b, g, i: (b, i, g)),
        out_shape=jax.ShapeDtypeStruct((bsz, seq_len, Q_COLS), BF16),
        scratch_shapes=[
            pltpu.VMEM((rows, HEAD_DIM), BF16),
            pltpu.VMEM((rows, HEAD_DIM), BF16),
            pltpu.VMEM((rows, LANES), F32),
            pltpu.VMEM((rows, 2 * HEAD_DIM), F32),
            pltpu.VMEM((rows, tk), F32),
            pltpu.VMEM((rows, tk), F32),
            pltpu.VMEM((rows, tk), BF16),
            pltpu.VMEM((rows, tk), BF16),
            pltpu.VMEM((rows, LANES), F32),
            pltpu.VMEM((rows, LANES), F32),
        ],
        compiler_params=pltpu.CompilerParams(
            dimension_semantics=("arbitrary", "arbitrary", "arbitrary"), vmem_limit_bytes=VMEM_LIMIT),
        name="attn_dense",
    )(q3, q3, k3, v3)
    return out.reshape(bsz * seq_len, Q_COLS)


def _merge_kernel(xn0_ref, oa0_ref, ob0_ref, xn1_ref, oa1_ref, ob1_ref, *rest, tiles0):
    i = pl.program_id(0)
    *rest, pre_a_ref, pre_b_ref = rest

    @pl.when(i == 0)
    def _():
        pre_b_ref[...] = jnp.zeros(pre_b_ref.shape, F32)

    for parity, (park_ref, done_ref) in enumerate(((pre_a_ref, pre_b_ref), (pre_b_ref, pre_a_ref))):
        @pl.when((i < tiles0) & ((i & 1) == parity))
        def _():
            _merge_tile(xn0_ref, oa0_ref, ob0_ref, *rest, park_ref, done_ref)

        @pl.when((i >= tiles0) & ((i & 1) == parity))
        def _():
            _merge_tile(xn1_ref, oa1_ref, ob1_ref, *rest, park_ref, done_ref)


def _merge_tile(xn_ref, oa_ref, ob_ref, wa_ref, wb_ref, wg_ref, bg_ref, wo_ref,
                g_ref, b_ref, wrh_ref, wrl_ref, br_ref, x1rows_ref, lt_ref, park_ref, done_ref):
    tc = MERGE_TC
    xb = xn_ref[...].astype(BF16)
    oa = oa_ref[...]
    ob = ob_ref[...]
    for c in range(D_MODEL // tc):
        lo, hi = c * tc, (c + 1) * tc
        ga = jax.nn.sigmoid(_dot(xb, wg_ref[:, lo:hi]) + bg_ref[:, lo:hi])
        gb = jax.nn.sigmoid(_dot(xb, wg_ref[:, D_MODEL + lo:D_MODEL + hi]) + bg_ref[:, D_MODEL + lo:D_MODEL + hi])
        mixed = (ga * _dot(oa, wa_ref[:, lo:hi]) + gb * _dot(ob, wb_ref[:, lo:hi])).astype(BF16)
        part = _dot(mixed, wo_ref[lo:hi, :])
        if c == 0:
            park_ref[...] = ALPHA * xn_ref[...] + part
        else:
            park_ref[...] += part

    x1 = _layer_norm_rows(done_ref[...], g_ref[...], b_ref[...])
    x1rows_ref[...] = x1.reshape(x1.shape[0], 1, D_MODEL)
    xh = x1.astype(BF16)
    xl = (x1 - xh.astype(F32)).astype(BF16)
    wrh = wrh_ref[...]
    logits = _dot(xh, wrh) + (_dot(xl, wrh) + _dot(xh, wrl_ref[...]))
    lt_ref[...] = logits.T + br_ref[...]


def _merge_call(batch0, batch1, wa, wb, wg, bg, wo, g, b, wr_hi, wr_lo, br):
    tm = MERGE_TM
    tiles0 = batch0[0].shape[0] // tm
    tiles1 = batch1[0].shape[0] // tm
    n_tiles = tiles0 + tiles1
    n = n_tiles * tm
    rows0 = lambda i: (jnp.minimum(i, tiles0 - 1), 0)
    rows1 = lambda i: (jnp.clip(i - tiles0, 0, tiles1 - 1), 0)
    fixed = lambda i: (0, 0)
    out_rows = lambda i: (jnp.maximum(i - 1, 0), 0, 0)
    out_cols = lambda i: (0, jnp.maximum(i - 1, 0))

    def resident(shape):
        return pl.BlockSpec(shape, fixed, pipeline_mode=pl.Buffered(1))

    def single(shape, index_map):
        return pl.BlockSpec(shape, index_map, pipeline_mode=pl.Buffered(1))

    return pl.pallas_call(
        functools.partial(_merge_kernel, tiles0=tiles0),
        grid=(n_tiles + 1,),
        in_specs=[
            pl.BlockSpec((tm, D_MODEL), rows0),
            pl.BlockSpec((tm, Q_COLS), rows0),
            pl.BlockSpec((tm, Q_COLS), rows0),
            single((tm, D_MODEL), rows1),
            single((tm, Q_COLS), rows1),
            single((tm, Q_COLS), rows1),
            resident((Q_COLS, D_MODEL)),
            resident((Q_COLS, D_MODEL)),
            resident((D_MODEL, 2 * D_MODEL)),
            resident((1, 2 * D_MODEL)),
            resident((D_MODEL, D_MODEL)),
            resident((1, D_MODEL)),
            resident((1, D_MODEL)),
            resident((D_MODEL, ROUTER_ROWS)),
            resident((D_MODEL, ROUTER_ROWS)),
            resident((ROUTER_ROWS, 1)),
        ],
        out_specs=(
            pl.BlockSpec((tm, 1, D_MODEL), out_rows),
            pl.BlockSpec((ROUTER_ROWS, tm), out_cols),
        ),
        out_shape=(
            jax.ShapeDtypeStruct((n, 1, D_MODEL), F32),
            jax.ShapeDtypeStruct((ROUTER_ROWS, n), F32),
        ),
        scratch_shapes=[pltpu.VMEM((tm, D_MODEL), F32), pltpu.VMEM((tm, D_MODEL), F32)],
        compiler_params=pltpu.CompilerParams(
            dimension_semantics=("arbitrary",), vmem_limit_bytes=VMEM_LIMIT),
        name="merge_ln_router",
    )(*batch0, *batch1, wa, wb, wg, bg, wo, g, b, wr_hi, wr_lo, br)


def _route_kernel(lt_ref, e_ref, w_ref, cnt_ref, carry_ref):
    tn = lt_ref.shape[1]

    @pl.when(pl.program_id(0) == 0)
    def _():
        carry_ref[...] = jnp.zeros(carry_ref.shape, F32)

    cl = lt_ref[0:N_GROUPS, :]
    gi = lax.broadcasted_iota(jnp.int32, (N_GROUPS, tn), 0)
    cmax = jnp.max(cl, axis=0, keepdims=True)
    ce = jnp.exp(cl - cmax)
    cp = ce / jnp.sum(ce, axis=0, keepdims=True)
    g_idx = jnp.min(jnp.where(cl == cmax, gi, N_GROUPS), axis=0, keepdims=True)
    g_prob = jnp.sum(jnp.where(gi == g_idx, cp, 0.0), axis=0, keepdims=True)

    fl = jnp.zeros((EXPERTS_PER_GROUP, tn), F32)
    for g in range(N_GROUPS):
        r0 = FINE_ROW0 + g * EXPERTS_PER_GROUP
        fl = jnp.where(g_idx == g, lt_ref[r0:r0 + EXPERTS_PER_GROUP, :], fl)
    fmax = jnp.max(fl, axis=0, keepdims=True)
    fe = jnp.exp(fl - fmax)
    fp = fe / jnp.sum(fe, axis=0, keepdims=True)
    ei = lax.broadcasted_iota(jnp.int32, (EXPERTS_PER_GROUP, tn), 0)
    p1 = jnp.max(fp, axis=0, keepdims=True)
    i1 = jnp.min(jnp.where(fp == p1, ei, EXPERTS_PER_GROUP), axis=0, keepdims=True)
    fp2 = jnp.where(ei == i1, -1.0, fp)
    p2 = jnp.max(fp2, axis=0, keepdims=True)
    i2 = jnp.min(jnp.where(fp2 == p2, ei, EXPERTS_PER_GROUP), axis=0, keepdims=True)
    tot = p1 + p2
    w1 = g_prob * p1 / tot
    w2 = g_prob * p2 / tot
    e1 = g_idx * EXPERTS_PER_GROUP + i1
    e2 = g_idx * EXPERTS_PER_GROUP + i2
    ri = lax.broadcasted_iota(jnp.int32, (8, tn), 0)
    e_ref[...] = jnp.where(ri == 0, e1, jnp.where(ri == 1, e2, 0))
    w_ref[...] = jnp.where(ri == 0, w1, jnp.where(ri == 1, w2, 0.0))

    xi = lax.broadcasted_iota(jnp.int32, (N_EXPERTS, tn), 0)
    hits = jnp.where(xi == e1, 1.0, 0.0) + jnp.where(xi == e2, 1.0, 0.0)
    total = carry_ref[...] + jnp.sum(hits, axis=1, keepdims=True)
    carry_ref[...] = total
    cnt_ref[...] = jnp.broadcast_to(total, cnt_ref.shape)


def _route_call(lt):
    n = lt.shape[1]
    tn = min(ROUTE_TN, n)
    tok = lambda i: (0, i)
    return pl.pallas_call(
        _route_kernel,
        grid=(n // tn,),
        in_specs=[pl.BlockSpec((ROUTER_ROWS, tn), tok)],
        out_specs=(pl.BlockSpec((8, tn), tok), pl.BlockSpec((8, tn), tok),
                   pl.BlockSpec((N_EXPERTS, LANES), lambda i: (0, 0))),
        out_shape=(jax.ShapeDtypeStruct((8, n), jnp.int32), jax.ShapeDtypeStruct((8, n), F32),
                   jax.ShapeDtypeStruct((N_EXPERTS, LANES), F32)),
        scratch_shapes=[pltpu.VMEM((N_EXPERTS, 1), F32)],
        compiler_params=pltpu.CompilerParams(dimension_semantics=("arbitrary",)),
        name="router",
    )(lt)


def _plan_kernel(starts_ref, vb_ref, ve_ref, vlo_ref, vhi_ref, *, rb):
    n_visits = vb_ref.shape[0]
    shift = rb.bit_length() - 1

    def expert_body(e, v):
        end = starts_ref[e + 1]

        def cond(state):
            return state[0] < end

        def body(state):
            r, v = state
            b = lax.shift_right_logical(r, shift)
            hi = jnp.minimum(end, (b + 1) * rb)
            vb_ref[v] = b
            ve_ref[v] = e
            vlo_ref[v] = r - b * rb
            vhi_ref[v] = hi - b * rb
            return hi, v + 1

        return lax.while_loop(cond, body, (starts_ref[e], v))[1]

    used = lax.fori_loop(0, N_EXPERTS, expert_body, 0)
    last_b = vb_ref[used - 1]
    last_e = ve_ref[used - 1]

    def pad(i, carry):
        vb_ref[i] = last_b
        ve_ref[i] = last_e
        vlo_ref[i] = 0
        vhi_ref[i] = 0
        return carry

    lax.fori_loop(used, n_visits, pad, 0)


def _plan_call(starts, n_assign, rb):
    assert rb & (rb - 1) == 0 and n_assign % rb == 0
    n_visits = n_assign // rb + N_EXPERTS - 1
    smem = pl.BlockSpec(memory_space=pltpu.SMEM)
    sds = jax.ShapeDtypeStruct((n_visits,), jnp.int32)
    return pl.pallas_call(
        functools.partial(_plan_kernel, rb=rb),
        in_specs=[smem], out_specs=(smem,) * 4, out_shape=(sds,) * 4, name="visit_plan",
    )(starts)


def _expert_kernel(vb_ref, ve_ref, vlo_ref, vhi_ref,
                   tok0_ref, tokn_ref, slot_ref, slotp_ref, x_hbm, wg_ref, wu_ref, wd_ref, y_hbm,
                   xin_ref, xf_ref, xb_ref, ys_ref, yout_ref, wgb_ref, wub_ref, wdb_ref,
                   state_ref, gsem, ssem):
    rb = xb_ref.shape[0]
    v = pl.program_id(0)
    b = vb_ref[v]
    lo = vlo_ref[v]
    hi = vhi_ref[v]
    unsent, in_flight = 0, 1

    def gather_row(idx_ref, r):
        return pltpu.make_async_copy(x_hbm.at[pl.ds(idx_ref[0, 0, r], 1)], xin_ref.at[pl.ds(r, 1)], gsem)

    def scatter_row(buf, idx_ref, r):
        return pltpu.make_async_copy(yout_ref.at[buf, pl.ds(r, 1)], y_hbm.at[pl.ds(idx_ref[0, 0, r], 1)], ssem.at[buf])

    def start_gather(idx_ref):
        def body(r, carry):
            gather_row(idx_ref, r).start()
            return carry
        lax.fori_loop(0, rb, body, 0, unroll=8)

    def wait_gather():
        def body(r, carry):
            gather_row(tok0_ref, r).wait()
            return carry
        lax.fori_loop(0, rb, body, 0, unroll=8)

    def wait_scatter(buf):
        def body(r, carry):
            scatter_row(buf, slot_ref, r).wait()
            return carry
        lax.fori_loop(0, rb, body, 0, unroll=8)
        state_ref[in_flight + buf] = 0

    def park_block(get_rows):
        buf = b & 1

        @pl.when(state_ref[in_flight + buf] == 1)
        def _():
            wait_scatter(buf)

        yout_ref[buf] = get_rows().reshape(rb, 1, D_MODEL)
        state_ref[unsent] = 1

    def mark_sent(buf):
        state_ref[unsent] = 0
        state_ref[in_flight + buf] = 1

    @pl.when(v == 0)
    def _():
        state_ref[unsent] = 0
        state_ref[in_flight] = 0
        state_ref[in_flight + 1] = 0
        start_gather(tok0_ref)

    @pl.when(hi > lo)
    def _():
        @pl.when(lo == 0)
        def _():
            wait_gather()
            xf_ref[...] = xin_ref[...].reshape(rb, D_MODEL)
            xb_ref[...] = xf_ref[...].astype(BF16)

        @pl.when((v == 0) | (ve_ref[v] != ve_ref[jnp.maximum(v - 1, 0)]))
        def _():
            wgb_ref[...] = wg_ref[...].astype(BF16)
            wub_ref[...] = wu_ref[...].astype(BF16)
            wdb_ref[...] = wd_ref[...].astype(BF16)

        def expert_rows():
            x = xb_ref[...]
            hidden = jax.nn.silu(_dot(x, wgb_ref[...])) * _dot(x, wub_ref[...])
            return _dot(hidden.astype(BF16), wdb_ref[...])

        whole = (lo == 0) & (hi == rb)
        prev_buf = (b + 1) & 1

        @pl.when(whole & (b > 0))
        def _():
            for r in range(rb):
                gather_row(tokn_ref, r).start()
                scatter_row(prev_buf, slotp_ref, r).start()
            mark_sent(prev_buf)
            y = expert_rows()
            park_block(lambda: y)

        @pl.when(whole & (b == 0))
        def _():
            for r in range(rb):
                gather_row(tokn_ref, r).start()
            y = expert_rows()
            park_block(lambda: y)

        @pl.when(jnp.logical_not(whole))
        def _():
            @pl.when(lo == 0)
            def _():
                start_gather(tokn_ref)

                @pl.when(state_ref[unsent] == 1)
                def _():
                    def body(r, carry):
                        scatter_row(prev_buf, slotp_ref, r).start()
                        return carry
                    lax.fori_loop(0, rb, body, 0, unroll=8)
                    mark_sent(prev_buf)

            y = expert_rows()
            rows = lax.broadcasted_iota(jnp.int32, (rb, 1), 0)
            mine = (rows >= lo) & (rows < hi)

            @pl.when(lo == 0)
            def _():
                ys_ref[...] = jnp.where(mine, y, 0.0)

            @pl.when(lo > 0)
            def _():
                ys_ref[...] = jnp.where(mine, y, ys_ref[...])

            @pl.when(hi == rb)
            def _():
                park_block(lambda: ys_ref[...])

    @pl.when(v == pl.num_programs(0) - 1)
    def _():
        last_buf = b & 1

        @pl.when(state_ref[unsent] == 1)
        def _():
            def body(r, carry):
                scatter_row(last_buf, slot_ref, r).start()
                return carry
            lax.fori_loop(0, rb, body, 0, unroll=8)
            mark_sent(last_buf)

        for buf in range(2):
            @pl.when(state_ref[in_flight + buf] == 1)
            def _():
                wait_scatter(buf)

        wait_gather()


def _expert_call(plan, tok_sorted, slot_sorted, x1rows, w_gate, w_up, w_down, rb):
    n_assign = tok_sorted.shape[0]
    n_blocks = n_assign // rb
    n_visits = plan[0].shape[0]
    tok3 = tok_sorted.reshape(n_blocks, 1, rb)
    slot3 = slot_sorted.reshape(n_blocks, 1, rb)
    w_map = lambda v, vb, ve, vlo, vhi: (ve[v], 0, 0)
    idx_block = (1, 1, rb)
    grid_spec = pltpu.PrefetchScalarGridSpec(
        num_scalar_prefetch=4,
        grid=(n_visits,),
        in_specs=[
            pl.BlockSpec(idx_block, lambda v, vb, ve, vlo, vhi: (0, 0, 0), memory_space=pltpu.SMEM),
            pl.BlockSpec(idx_block, lambda v, vb, ve, vlo, vhi: (jnp.minimum(vb[v] + 1, n_blocks - 1), 0, 0),
                         memory_space=pltpu.SMEM),
            pl.BlockSpec(idx_block, lambda v, vb, ve, vlo, vhi: (vb[v], 0, 0), memory_space=pltpu.SMEM),
            pl.BlockSpec(idx_block, lambda v, vb, ve, vlo, vhi: (jnp.maximum(vb[v] - 1, 0), 0, 0),
                         memory_space=pltpu.SMEM),
            pl.BlockSpec(memory_space=pl.ANY),
            pl.BlockSpec((None, D_MODEL, D_EXPERT), w_map),
            pl.BlockSpec((None, D_MODEL, D_EXPERT), w_map),
            pl.BlockSpec((None, D_EXPERT, D_MODEL), w_map),
        ],
        out_specs=pl.BlockSpec(memory_space=pl.ANY),
        scratch_shapes=[
            pltpu.VMEM((rb, 1, D_MODEL), F32),
            pltpu.VMEM((rb, D_MODEL), F32),
            pltpu.VMEM((rb, D_MODEL), BF16),
            pltpu.VMEM((rb, D_MODEL), F32),
            pltpu.VMEM((2, rb, 1, D_MODEL), F32),
            pltpu.VMEM((D_MODEL, D_EXPERT), BF16),
            pltpu.VMEM((D_MODEL, D_EXPERT), BF16),
            pltpu.VMEM((D_EXPERT, D_MODEL), BF16),
            pltpu.SMEM((3,), jnp.int32),
            pltpu.SemaphoreType.DMA(()),
            pltpu.SemaphoreType.DMA((2,)),
        ],
    )
    return pl.pallas_call(
        _expert_kernel,
        grid_spec=grid_spec,
        out_shape=jax.ShapeDtypeStruct((n_assign, 1, D_MODEL), F32),
        compiler_params=pltpu.CompilerParams(
            dimension_semantics=("arbitrary",), vmem_limit_bytes=VMEM_LIMIT),
        name="experts",
    )(*plan, tok3, tok3, slot3, slot3, x1rows, w_gate, w_up, w_down)


def _final_kernel(x_ref, y0_ref, y1_ref, w_ref, g_ref, b_ref, o_ref, xs_ref, ya_ref, yb_ref):
    tm = o_ref.shape[0]
    xs_ref[...] = x_ref[...].reshape(tm, D_MODEL)
    ya_ref[...] = y0_ref[...].reshape(tm, D_MODEL)
    yb_ref[...] = y1_ref[...].reshape(tm, D_MODEL)
    w = w_ref[...]
    moe = ya_ref[...] * w[:, 0:1] + yb_ref[...] * w[:, 1:2]
    o_ref[...] = _layer_norm_rows(ALPHA * xs_ref[...] + moe, g_ref[...], b_ref[...])


def _final_call(x1rows, y2, w_tok, g, b, row0, n):
    tm = FINAL_TM
    assert row0 % tm == 0 and n % tm == 0
    t0 = row0 // tm
    fixed = lambda i: (0, 0)
    return pl.pallas_call(
        _final_kernel,
        grid=(n // tm,),
        in_specs=[
            pl.BlockSpec((tm, 1, D_MODEL), lambda i: (t0 + i, 0, 0)),
            pl.BlockSpec((None, tm, 1, D_MODEL), lambda i: (0, t0 + i, 0, 0)),
            pl.BlockSpec((None, tm, 1, D_MODEL), lambda i: (1, t0 + i, 0, 0)),
            pl.BlockSpec((tm, TOP_K), lambda i: (t0 + i, 0)),
            pl.BlockSpec((1, D_MODEL), fixed),
            pl.BlockSpec((1, D_MODEL), fixed),
        ],
        out_specs=pl.BlockSpec((tm, D_MODEL), lambda i: (i, 0)),
        out_shape=jax.ShapeDtypeStruct((n, D_MODEL), F32),
        scratch_shapes=[pltpu.VMEM((tm, D_MODEL), F32)] * 3,
        compiler_params=pltpu.CompilerParams(
            dimension_semantics=("arbitrary",), vmem_limit_bytes=VMEM_LIMIT),
        name="combine_ln",
    )(x1rows, y2, y2, w_tok, g, b)


def _t5_bucket(rel):
    half = N_BUCKETS // 2
    exact = half // 2
    n = jnp.abs(rel)
    nf = jnp.maximum(n, 1).astype(F32)
    large = exact + (jnp.log(nf / exact) / math.log(MAX_DISTANCE / exact) * (half - exact)).astype(jnp.int32)
    large = jnp.minimum(large, half - 1)
    return jnp.where(rel > 0, half, 0) + jnp.where(n < exact, n, large)


def _window_bias(rel_bias):
    rel = jnp.arange(3 * BLOCK)[None, :] - BLOCK - jnp.arange(BLOCK)[:, None]
    onehot = (_t5_bucket(rel)[..., None] == jnp.arange(N_BUCKETS)).astype(F32)
    bias = jnp.einsum("qkb,bh->hqk", onehot, rel_bias.astype(F32), precision=lax.Precision.HIGHEST)
    band = jnp.abs(rel) <= WINDOW
    col = jnp.arange(3 * BLOCK)
    tables = []
    for edge in range(4):
        keep = band & ((col >= BLOCK) | (edge & 1 == 0))[None, :] & ((col < 2 * BLOCK) | (edge & 2 == 0))[None, :]
        tables.append(jnp.where(keep[None], bias * LOG2E, NEG_BIG))
    return jnp.stack(tables)


def _rope_tables(seq_len):
    rows = seq_len // GRID_W
    row_ids = jnp.repeat(jnp.arange(rows), GRID_W).astype(F32)
    col_ids = jnp.tile(jnp.arange(GRID_W), rows).astype(F32)
    half = HEAD_DIM // 2
    inv = 1.0 / (ROPE_THETA ** (jnp.arange(0, half, 2, dtype=F32) / half))
    ang = jnp.concatenate([row_ids[:, None] * inv, col_ids[:, None] * inv], -1)
    cos, sin = jnp.cos(ang), jnp.sin(ang)
    return jnp.concatenate([cos, cos], -1), jnp.concatenate([-sin, sin], -1)


def _deinterleave_cols(w, n_heads):
    d = w.shape[0]
    return w.reshape(d, n_heads, HEAD_DIM // 2, 2).transpose(0, 1, 3, 2).reshape(d, n_heads * HEAD_DIM)


def _deinterleave_gain(g):
    return g.reshape(HEAD_DIM // 2, 2).T.reshape(1, HEAD_DIM)


def _expert_row_block(n_assign):
    return MOE_RB if n_assign // N_EXPERTS >= 8 * MOE_RB else MOE_RB // 2


def _mixers(x, p):
    bsz, seq_len, d = x.shape
    cos_t, sin_t = _rope_tables(seq_len)
    xn, qa, ka, va, qb, kb, vb = _qkv_call(x.reshape(bsz * seq_len, d), p["emb_g"], p["emb_b"], p["w_qkv"],
                                           cos_t, sin_t, p["q_gain"], p["k_gain"], seq_len)
    oa = _attn_a_call(p["sink"], qa, ka, va, p["bias_a"], bsz, seq_len)
    ob = _attn_b_call(qb, kb, vb, bsz, seq_len)
    return xn, oa, ob


def _layer(x0, x1, p):
    d = x0.shape[-1]
    n0 = x0.shape[0] * x0.shape[1]
    n1 = x1.shape[0] * x1.shape[1]
    n = n0 + n1
    n_assign = n * TOP_K
    x1rows, lt = _merge_call(_mixers(x0, p), _mixers(x1, p), p["w_a"], p["w_b"], p["w_g"], p["b_g"], p["w_o"],
                             p["ln1_g"], p["ln1_b"], p["wr_hi"], p["wr_lo"], p["b_r"])
    e_rows, w_rows, cnt = _route_call(lt)
    e_flat = e_rows[:TOP_K].reshape(n_assign)
    assert N_EXPERTS * n_assign < 2 ** 31
    slot_sorted = lax.rem(lax.sort(e_flat * n_assign + jnp.arange(n_assign, dtype=jnp.int32)), n_assign)
    tok_sorted = jnp.where(slot_sorted >= n, slot_sorted - n, slot_sorted)
    ends = jnp.cumsum(cnt[:, 0].astype(jnp.int32))
    starts = jnp.concatenate([jnp.zeros((1,), jnp.int32), ends])
    rb = _expert_row_block(n_assign)
    plan = _plan_call(starts, n_assign, rb)
    y2 = _expert_call(plan, tok_sorted, slot_sorted, x1rows, p["w_gate"], p["w_up"], p["w_down"], rb)
    y2 = y2.reshape(TOP_K, n, 1, d)
    w_tok = w_rows[:TOP_K].T
    out0 = _final_call(x1rows, y2, w_tok, p["ln2_g"], p["ln2_b"], 0, n0)
    out1 = _final_call(x1rows, y2, w_tok, p["ln2_g"], p["ln2_b"], n0, n1)
    return out0.reshape(x0.shape), out1.reshape(x1.shape)


def kernel(x_prompt, x_sample, emb_ln_g, emb_ln_b, rel_bias, w_in, b_gate, sink_a, q_norm_g, k_norm_g,
           w_branch_a, w_branch_b, w_out, ln1_g, ln1_b, w_coarse, b_coarse, w_fine, b_fine,
           w_gate, w_up, w_down, ln2_g, ln2_b):
    l = 0
    w_in_l = w_in[l]
    qa_end = Q_COLS
    kva_end = qa_end + 2 * KV_COLS
    qb_end = kva_end + Q_COLS
    kb_end = qb_end + KV_COLS
    vb_end = kb_end + KV_COLS
    w_qkv = jnp.concatenate([
        w_in_l[:, :kva_end],
        _deinterleave_cols(w_in_l[:, kva_end:qb_end], N_HEADS),
        _deinterleave_cols(w_in_l[:, qb_end:kb_end], N_KV),
        w_in_l[:, kb_end:vb_end],
    ], axis=1).astype(BF16)
    w_router = jnp.zeros((D_MODEL, ROUTER_ROWS), F32)
    w_router = w_router.at[:, 0:N_GROUPS].set(w_coarse[l])
    w_router = w_router.at[:, FINE_ROW0:FINE_ROW0 + N_EXPERTS].set(w_fine[l])
    wr_hi = w_router.astype(BF16)
    wr_lo = (w_router - wr_hi.astype(F32)).astype(BF16)
    b_router = jnp.zeros((ROUTER_ROWS, 1), F32)
    b_router = b_router.at[0:N_GROUPS, 0].set(b_coarse[l].astype(F32))
    b_router = b_router.at[FINE_ROW0:FINE_ROW0 + N_EXPERTS, 0].set(b_fine[l].astype(F32))
    p = {
        "emb_g": emb_ln_g.reshape(1, D_MODEL), "emb_b": emb_ln_b.reshape(1, D_MODEL),
        "w_qkv": w_qkv,
        "q_gain": _deinterleave_gain(q_norm_g[l]), "k_gain": _deinterleave_gain(k_norm_g[l]),
        "sink": sink_a[l].astype(F32) * LOG2E, "bias_a": _window_bias(rel_bias),
        "w_a": w_branch_a[l].astype(BF16), "w_b": w_branch_b[l].astype(BF16),
        "w_g": w_in_l[:, vb_end:].astype(BF16), "b_g": b_gate[l].reshape(1, 2 * D_MODEL),
        "w_o": w_out[l].astype(BF16),
        "ln1_g": ln1_g[l].reshape(1, D_MODEL), "ln1_b": ln1_b[l].reshape(1, D_MODEL),
        "wr_hi": wr_hi, "wr_lo": wr_lo, "b_r": b_router,
        "w_gate": w_gate[l], "w_up": w_up[l], "w_down": w_down[l],
        "ln2_g": ln2_g[l].reshape(1, D_MODEL), "ln2_b": ln2_b[l].reshape(1, D_MODEL),
    }
    return _layer(x_prompt, x_sample, p)
```

```python
import functools
import math

import jax
import jax.numpy as jnp
from jax import lax
from jax.experimental import pallas as pl
from jax.experimental.pallas import tpu as pltpu

F32 = jnp.float32
BF16 = jnp.bfloat16

D_MODEL = 2048
HEAD_DIM = 128
N_HEADS = 8
N_KV = 2
GROUP = N_HEADS // N_KV
Q_COLS = N_HEADS * HEAD_DIM
KV_COLS = N_KV * HEAD_DIM
QKV_COLS = 2 * (Q_COLS + 2 * KV_COLS)
WINDOW = 128
BLOCK = 128
GRID_W = 64
ROPE_THETA = 10000.0
N_BUCKETS = 32
MAX_DISTANCE = 128
N_GROUPS = 4
EXPERTS_PER_GROUP = 8
N_EXPERTS = N_GROUPS * EXPERTS_PER_GROUP
TOP_K = 2
D_EXPERT = 512
LN_EPS = 1e-5
RMS_EPS = 1e-6
DEPTH = 1
ALPHA = (2 * DEPTH) ** 0.25
SCALE = HEAD_DIM ** -0.5
LOG2E = math.log2(math.e)
NEG_BIG = -1e30

LANES = 128
VMEM_LIMIT = 56 * 1024 * 1024

QKV_TM = 512
ATT_A_TQ = 1024
ATT_B_TQ = 256
ATT_B_TK = 512
MERGE_TM = 256
MERGE_TC = 1024
ROUTE_TN = 1024
MOE_RB = 256
FINAL_TM = 512
ROUTER_ROWS = 128
FINE_ROW0 = 8


def _layer_norm_rows(x, g, b):
    mu = jnp.mean(x, axis=-1, keepdims=True)
    xc = x - mu
    var = jnp.mean(xc * xc, axis=-1, keepdims=True)
    return xc * lax.rsqrt(var + LN_EPS) * g + b


def _dot(a, b):
    return jnp.dot(a, b, preferred_element_type=F32)


def _dot_nt(a, b):
    return lax.dot_general(a, b, (((1,), (1,)), ((), ())), preferred_element_type=F32)


def _qkv_kernel(x_ref, g_ref, b_ref, w_ref, cos_ref, sin_ref, qg_ref, kg_ref,
                xn_ref, qa_ref, ka_ref, va_ref, qb_ref, kb_ref, vb_ref):
    xn = _layer_norm_rows(x_ref[...], g_ref[...], b_ref[...])
    xn_ref[...] = xn
    xb = xn.astype(BF16)
    cos = cos_ref[...]
    sin = sin_ref[...]

    def proj(c0, width):
        return _dot(xb, w_ref[:, c0:c0 + width])

    def norm_rope(h, gain):
        y = h * lax.rsqrt(jnp.mean(h * h, axis=-1, keepdims=True) + RMS_EPS) * gain
        return y * cos + pltpu.roll(y, HEAD_DIM // 2, 1) * sin

    c = Q_COLS + 2 * KV_COLS
    qg = qg_ref[...]
    for hp in range(N_HEADS // 2):
        hq2 = proj(c + 2 * hp * HEAD_DIM, 2 * HEAD_DIM)
        for h in (2 * hp, 2 * hp + 1):
            hq = hq2[:, (h - 2 * hp) * HEAD_DIM:(h - 2 * hp + 1) * HEAD_DIM]
            qb_ref[:, h * HEAD_DIM:(h + 1) * HEAD_DIM] = (norm_rope(hq, qg) * (SCALE * LOG2E)).astype(BF16)
    c += Q_COLS
    kg = kg_ref[...]
    hk2 = proj(c, KV_COLS)
    for h in range(N_KV):
        kb_ref[:, h * HEAD_DIM:(h + 1) * HEAD_DIM] = norm_rope(hk2[:, h * HEAD_DIM:(h + 1) * HEAD_DIM], kg).astype(BF16)
    c += KV_COLS
    hv2 = proj(c, KV_COLS)
    ones = jnp.ones((x_ref.shape[0], HEAD_DIM), BF16)
    for h in range(N_KV):
        vb_ref[:, 2 * h * HEAD_DIM:(2 * h + 1) * HEAD_DIM] = hv2[:, h * HEAD_DIM:(h + 1) * HEAD_DIM].astype(BF16)
        vb_ref[:, (2 * h + 1) * HEAD_DIM:(2 * h + 2) * HEAD_DIM] = ones
    c = 0
    qa_ref[...] = (proj(c, Q_COLS) * (SCALE * LOG2E)).astype(BF16)
    c += Q_COLS
    ka_ref[...] = proj(c, KV_COLS).astype(BF16)
    c += KV_COLS
    va_ref[...] = proj(c, KV_COLS).astype(BF16)


def _qkv_call(x2, g, b, w_qkv, cos_t, sin_t, qg, kg, seq_len):
    n = x2.shape[0]
    tm = QKV_TM
    pos_blocks = seq_len // tm
    row = lambda i: (i, 0)
    fixed = lambda i: (0, 0)
    pos = lambda i: (i % pos_blocks, 0)
    out_shapes = (
        jax.ShapeDtypeStruct((n, D_MODEL), F32),
        jax.ShapeDtypeStruct((n, Q_COLS), BF16),
        jax.ShapeDtypeStruct((n, KV_COLS), BF16),
        jax.ShapeDtypeStruct((n, KV_COLS), BF16),
        jax.ShapeDtypeStruct((n, Q_COLS), BF16),
        jax.ShapeDtypeStruct((n, KV_COLS), BF16),
        jax.ShapeDtypeStruct((n, 2 * KV_COLS), BF16),
    )
    return pl.pallas_call(
        _qkv_kernel,
        grid=(n // tm,),
        in_specs=[
            pl.BlockSpec((tm, D_MODEL), row),
            pl.BlockSpec((1, D_MODEL), fixed),
            pl.BlockSpec((1, D_MODEL), fixed),
            pl.BlockSpec((D_MODEL, QKV_COLS), fixed, pipeline_mode=pl.Buffered(1)),
            pl.BlockSpec((tm, HEAD_DIM), pos),
            pl.BlockSpec((tm, HEAD_DIM), pos),
            pl.BlockSpec((1, HEAD_DIM), fixed),
            pl.BlockSpec((1, HEAD_DIM), fixed),
        ],
        out_specs=(
            pl.BlockSpec((tm, D_MODEL), row),
            pl.BlockSpec((tm, Q_COLS), row),
            pl.BlockSpec((tm, KV_COLS), row),
            pl.BlockSpec((tm, KV_COLS), row),
            pl.BlockSpec((tm, Q_COLS), row),
            pl.BlockSpec((tm, KV_COLS), row),
            pl.BlockSpec((tm, 2 * KV_COLS), row),
        ),
        out_shape=out_shapes,
        compiler_params=pltpu.CompilerParams(
            dimension_semantics=("arbitrary",), vmem_limit_bytes=VMEM_LIMIT),
        name="ln_qkv",
    )(x2, g, b, w_qkv, cos_t, sin_t, qg, kg)


def _attn_a_kernel(sink_ref, q_ref, kc_ref, kp_ref, kn_ref, vc_ref, vp_ref, vn_ref, bias_ref,
                   o_ref, kcat_ref, vcat_ref, sa_ref, sb_ref, *, seq_len):
    tq = ATT_A_TQ
    i = pl.program_id(1)
    kcat_ref[0:BLOCK, :] = kp_ref[...]
    kcat_ref[BLOCK:BLOCK + tq, :] = kc_ref[...]
    kcat_ref[BLOCK + tq:, :] = kn_ref[...]
    vcat_ref[0:BLOCK, :] = vp_ref[...]
    vcat_ref[BLOCK:BLOCK + tq, :] = vc_ref[...]
    vcat_ref[BLOCK + tq:, :] = vn_ref[...]
    s_refs = (sa_ref, sb_ref)
    n_seq_blocks = seq_len // BLOCK

    def block_scores(j, s_ref):
        qj = q_ref[j * BLOCK:(j + 1) * BLOCK, :]
        for g in range(N_KV):
            qs = jnp.concatenate(
                [qj[:, h * HEAD_DIM:(h + 1) * HEAD_DIM] for h in range(g * GROUP, (g + 1) * GROUP)], axis=0)
            kw = kcat_ref[j * BLOCK:(j + 3) * BLOCK, g * HEAD_DIM:(g + 1) * HEAD_DIM]
            s_ref[g * GROUP * BLOCK:(g + 1) * GROUP * BLOCK, :] = _dot_nt(qs, kw)

    def block_out(j, s_ref):
        blk = i * (tq // BLOCK) + j
        edge = (blk == 0).astype(jnp.int32) + 2 * (blk == n_seq_blocks - 1).astype(jnp.int32)
        for g in range(N_KV):
            probs = []
            for h in range(g * GROUP, (g + 1) * GROUP):
                s = s_ref[h * BLOCK:(h + 1) * BLOCK, :] + bias_ref[edge, h]
                sk = sink_ref[h]
                m = jnp.maximum(jnp.max(s, axis=-1, keepdims=True), sk)
                p = jnp.exp2(s - m)
                denom = jnp.sum(p, axis=-1, keepdims=True) + jnp.exp2(sk - m)
                probs.append((p / denom).astype(BF16))
            vw = vcat_ref[j * BLOCK:(j + 3) * BLOCK, g * HEAD_DIM:(g + 1) * HEAD_DIM]
            o = _dot(jnp.concatenate(probs, axis=0), vw)
            for hh in range(GROUP):
                h = g * GROUP + hh
                o_ref[j * BLOCK:(j + 1) * BLOCK, h * HEAD_DIM:(h + 1) * HEAD_DIM] = (
                    o[hh * BLOCK:(hh + 1) * BLOCK, :].astype(BF16))

    n_blk = tq // BLOCK
    block_scores(0, s_refs[0])
    for j in range(n_blk):
        if j + 1 < n_blk:
            block_scores(j + 1, s_refs[(j + 1) % 2])
        block_out(j, s_refs[j % 2])


def _attn_a_call(sink, qa, ka, va, bias, bsz, seq_len):
    tq = ATT_A_TQ
    nb = seq_len // BLOCK
    bpt = tq // BLOCK
    q3 = qa.reshape(bsz, seq_len, Q_COLS)
    k3 = ka.reshape(bsz, seq_len, KV_COLS)
    v3 = va.reshape(bsz, seq_len, KV_COLS)
    cur = lambda b, i: (b, i, 0)
    prev = lambda b, i: (b, jnp.maximum(i * bpt - 1, 0), 0)
    nxt = lambda b, i: (b, jnp.minimum((i + 1) * bpt, nb - 1), 0)
    kv_cur = pl.BlockSpec((None, tq, KV_COLS), cur)
    kv_prev = pl.BlockSpec((None, BLOCK, KV_COLS), prev)
    kv_next = pl.BlockSpec((None, BLOCK, KV_COLS), nxt)
    out = pl.pallas_call(
        functools.partial(_attn_a_kernel, seq_len=seq_len),
        grid=(bsz, seq_len // tq),
        in_specs=[
            pl.BlockSpec(memory_space=pltpu.SMEM),
            pl.BlockSpec((None, tq, Q_COLS), cur),
            kv_cur, kv_prev, kv_next,
            kv_cur, kv_prev, kv_next,
            pl.BlockSpec((4, N_HEADS, BLOCK, 3 * BLOCK), lambda b, i: (0, 0, 0, 0)),
        ],
        out_specs=pl.BlockSpec((None, tq, Q_COLS), cur),
        out_shape=jax.ShapeDtypeStruct((bsz, seq_len, Q_COLS), BF16),
        scratch_shapes=[
            pltpu.VMEM((tq + 2 * BLOCK, KV_COLS), BF16),
            pltpu.VMEM((tq + 2 * BLOCK, KV_COLS), BF16),
            pltpu.VMEM((N_HEADS * BLOCK, 3 * BLOCK), F32),
            pltpu.VMEM((N_HEADS * BLOCK, 3 * BLOCK), F32),
        ],
        compiler_params=pltpu.CompilerParams(
            dimension_semantics=("arbitrary", "arbitrary"), vmem_limit_bytes=VMEM_LIMIT),
        name="attn_window",
    )(sink, q3, k3, k3, k3, v3, v3, v3, bias)
    return out.reshape(bsz * seq_len, Q_COLS)


def _attn_b_kernel(q_ref, qn_ref, k_ref, v_ref, o_ref, qs_ref, qsn_ref, m_ref, acc_ref,
                   s0_ref, s1_ref, p0_ref, p1_ref, a0_ref, a1_ref, *, seq_len):
    tq = ATT_B_TQ
    tk = ATT_B_TK
    n_chunks = seq_len // tk
    rows = GROUP * tq
    for h in range(GROUP):
        qs_ref[h * tq:(h + 1) * tq, :] = q_ref[:, h * HEAD_DIM:(h + 1) * HEAD_DIM]
        qsn_ref[h * tq:(h + 1) * tq, :] = qn_ref[:, h * HEAD_DIM:(h + 1) * HEAD_DIM]

    def scores(q_src, c, s_ref):
        k0 = pl.multiple_of(c * tk, tk)
        s_ref[...] = _dot_nt(q_src[...], k_ref[pl.ds(k0, tk), :])

    def softmax(s_ref, p_ref, a_ref, first=False):
        m_tile = s_ref[:, 0:LANES]
        for j in range(1, tk // LANES):
            m_tile = jnp.maximum(m_tile, s_ref[:, j * LANES:(j + 1) * LANES])
        m_cur = jnp.broadcast_to(jnp.max(m_tile, axis=-1, keepdims=True), (rows, LANES))
        if first:
            m_new = m_cur
            a_ref[...] = jnp.zeros((rows, LANES), F32)
        else:
            m_prev = m_ref[...]
            m_new = jnp.maximum(m_prev, m_cur)
            a_ref[...] = jnp.exp2(m_prev - m_new)
        m_ref[...] = m_new
        for j in range(tk // LANES):
            p_ref[:, j * LANES:(j + 1) * LANES] = jnp.exp2(s_ref[:, j * LANES:(j + 1) * LANES] - m_new).astype(BF16)

    def weighted_values(c, p_ref, a_ref):
        k0 = pl.multiple_of(c * tk, tk)
        pv = _dot(p_ref[...], v_ref[pl.ds(k0, tk), :])
        a = a_ref[...]
        acc_ref[:, 0:HEAD_DIM] = a * acc_ref[:, 0:HEAD_DIM] + pv[:, 0:HEAD_DIM]
        acc_ref[:, HEAD_DIM:] = a * acc_ref[:, HEAD_DIM:] + pv[:, HEAD_DIM:]

    @pl.when(pl.program_id(2) == 0)
    def _():
        scores(qs_ref, 0, s0_ref)
        scores(qs_ref, 1, s1_ref)
        softmax(s0_ref, p0_ref, a0_ref, first=True)

    acc_ref[...] = jnp.zeros(acc_ref.shape, F32)

    def body(t, carry):
        c = 2 * t + 1
        scores(qs_ref, c + 1, s0_ref)
        softmax(s1_ref, p1_ref, a1_ref)
        weighted_values(c - 1, p0_ref, a0_ref)
        scores(qs_ref, c + 2, s1_ref)
        softmax(s0_ref, p0_ref, a0_ref)
        weighted_values(c, p1_ref, a1_ref)
        return carry

    lax.fori_loop(0, (n_chunks - 2) // 2, body, 0, unroll=True)
    scores(qsn_ref, 0, s0_ref)
    softmax(s1_ref, p1_ref, a1_ref)
    weighted_values(n_chunks - 2, p0_ref, a0_ref)
    scores(qsn_ref, 1, s1_ref)
    softmax(s0_ref, p0_ref, a0_ref, first=True)
    weighted_values(n_chunks - 1, p1_ref, a1_ref)
    o = acc_ref[:, 0:HEAD_DIM] / acc_ref[:, HEAD_DIM:]
    for h in range(GROUP):
        o_ref[:, h * HEAD_DIM:(h + 1) * HEAD_DIM] = o[h * tq:(h + 1) * tq, :].astype(BF16)


def _attn_b_call(qb, kb, vb, bsz, seq_len):
    tq = ATT_B_TQ
    tk = ATT_B_TK
    assert seq_len % tk == 0 and (seq_len // tk) % 2 == 0, "dense mixer pipeline needs an even number of key chunks"
    q3 = qb.reshape(bsz, seq_len, Q_COLS)
    k3 = kb.reshape(bsz, seq_len, KV_COLS)
    v3 = vb.reshape(bsz, seq_len, 2 * KV_COLS)
    gcols = GROUP * HEAD_DIM
    n_q = seq_len // tq
    rows = GROUP * tq
    out = pl.pallas_call(
        functools.partial(_attn_b_kernel, seq_len=seq_len),
        grid=(bsz, N_KV, n_q),
        in_specs=[
            pl.BlockSpec((None, tq, gcols), lambda b, g, i: (b, i, g)),
            pl.BlockSpec((None, tq, gcols), lambda b, g, i: (b, jnp.minimum(i + 1, n_q - 1), g)),
            pl.BlockSpec((None, seq_len, HEAD_DIM), lambda b, g, i: (b, 0, g)),
            pl.BlockSpec((None, seq_len, 2 * HEAD_DIM), lambda b, g, i: (b, 0, g)),
        ],
        out_specs=pl.BlockSpec((None, tq, gcols), lambda b, g, i: (b, i, g)),
        out_shape=jax.ShapeDtypeStruct((bsz, seq_len, Q_COLS), BF16),
        scratch_shapes=[
            pltpu.VMEM((rows, HEAD_DIM), BF16),
            pltpu.VMEM((rows, HEAD_DIM), BF16),
            pltpu.VMEM((rows, LANES), F32),
            pltpu.VMEM((rows, 2 * HEAD_DIM), F32),
            pltpu.VMEM((rows, tk), F32),
            pltpu.VMEM((rows, tk), F32),
            pltpu.VMEM((rows, tk), BF16),
            pltpu.VMEM((rows, tk), BF16),
            pltpu.VMEM((rows, LANES), F32),
            pltpu.VMEM((rows, LANES), F32),
        ],
        compiler_params=pltpu.CompilerParams(
            dimension_semantics=("arbitrary", "arbitrary", "arbitrary"), vmem_limit_bytes=VMEM_LIMIT),
        name="attn_dense",
    )(q3, q3, k3, v3)
    return out.reshape(bsz * seq_len, Q_COLS)


def _merge_kernel(xn0_ref, oa0_ref, ob0_ref, xn1_ref, oa1_ref, ob1_ref, *rest, tiles0):
    i = pl.program_id(0)

    @pl.when(i < tiles0)
    def _():
        _merge_tile(xn0_ref, oa0_ref, ob0_ref, *rest)

    @pl.when(i >= tiles0)
    def _():
        _merge_tile(xn1_ref, oa1_ref, ob1_ref, *rest)


def _merge_tile(xn_ref, oa_ref, ob_ref, wa_ref, wb_ref, wg_ref, bg_ref, wo_ref,
                g_ref, b_ref, wrh_ref, wrl_ref, br_ref, x1rows_ref, lt_ref, acc_ref):
    tc = MERGE_TC
    xb = xn_ref[...].astype(BF16)
    oa = oa_ref[...]
    ob = ob_ref[...]
    for c in range(D_MODEL // tc):
        lo, hi = c * tc, (c + 1) * tc
        ga = jax.nn.sigmoid(_dot(xb, wg_ref[:, lo:hi]) + bg_ref[:, lo:hi])
        gb = jax.nn.sigmoid(_dot(xb, wg_ref[:, D_MODEL + lo:D_MODEL + hi]) + bg_ref[:, D_MODEL + lo:D_MODEL + hi])
        mixed = (ga * _dot(oa, wa_ref[:, lo:hi]) + gb * _dot(ob, wb_ref[:, lo:hi])).astype(BF16)
        part = _dot(mixed, wo_ref[lo:hi, :])
        if c == 0:
            acc_ref[...] = part
        else:
            acc_ref[...] += part
    x1 = _layer_norm_rows(ALPHA * xn_ref[...] + acc_ref[...], g_ref[...], b_ref[...])
    x1rows_ref[...] = x1.reshape(x1.shape[0], 1, D_MODEL)
    xh = x1.astype(BF16)
    xl = (x1 - xh.astype(F32)).astype(BF16)
    wrh = wrh_ref[...]
    logits = _dot(xh, wrh) + (_dot(xl, wrh) + _dot(xh, wrl_ref[...]))
    lt_ref[...] = logits.T + br_ref[...]


def _merge_call(batch0, batch1, wa, wb, wg, bg, wo, g, b, wr_hi, wr_lo, br):
    tm = MERGE_TM
    tiles0 = batch0[0].shape[0] // tm
    n = batch0[0].shape[0] + batch1[0].shape[0]
    rows0 = lambda i: (jnp.minimum(i, tiles0 - 1), 0)
    rows1 = lambda i: (jnp.maximum(i - tiles0, 0), 0)
    fixed = lambda i: (0, 0)

    def resident(shape):
        return pl.BlockSpec(shape, fixed, pipeline_mode=pl.Buffered(1))

    return pl.pallas_call(
        functools.partial(_merge_kernel, tiles0=tiles0),
        grid=(n // tm,),
        in_specs=[
            pl.BlockSpec((tm, D_MODEL), rows0),
            pl.BlockSpec((tm, Q_COLS), rows0),
            pl.BlockSpec((tm, Q_COLS), rows0),
            pl.BlockSpec((tm, D_MODEL), rows1),
            pl.BlockSpec((tm, Q_COLS), rows1),
            pl.BlockSpec((tm, Q_COLS), rows1),
            resident((Q_COLS, D_MODEL)),
            resident((Q_COLS, D_MODEL)),
            resident((D_MODEL, 2 * D_MODEL)),
            resident((1, 2 * D_MODEL)),
            resident((D_MODEL, D_MODEL)),
            resident((1, D_MODEL)),
            resident((1, D_MODEL)),
            resident((D_MODEL, ROUTER_ROWS)),
            resident((D_MODEL, ROUTER_ROWS)),
            resident((ROUTER_ROWS, 1)),
        ],
        out_specs=(
            pl.BlockSpec((tm, 1, D_MODEL), lambda i: (i, 0, 0)),
            pl.BlockSpec((ROUTER_ROWS, tm), lambda i: (0, i)),
        ),
        out_shape=(
            jax.ShapeDtypeStruct((n, 1, D_MODEL), F32),
            jax.ShapeDtypeStruct((ROUTER_ROWS, n), F32),
        ),
        scratch_shapes=[pltpu.VMEM((tm, D_MODEL), F32)],
        compiler_params=pltpu.CompilerParams(
            dimension_semantics=("arbitrary",), vmem_limit_bytes=VMEM_LIMIT),
        name="merge_ln_router",
    )(*batch0, *batch1, wa, wb, wg, bg, wo, g, b, wr_hi, wr_lo, br)


def _route_kernel(lt_ref, e_ref, w_ref, cnt_ref, carry_ref):
    tn = lt_ref.shape[1]

    @pl.when(pl.program_id(0) == 0)
    def _():
        carry_ref[...] = jnp.zeros(carry_ref.shape, F32)

    cl = lt_ref[0:N_GROUPS, :]
    gi = lax.broadcasted_iota(jnp.int32, (N_GROUPS, tn), 0)
    cmax = jnp.max(cl, axis=0, keepdims=True)
    ce = jnp.exp(cl - cmax)
    cp = ce / jnp.sum(ce, axis=0, keepdims=True)
    g_idx = jnp.min(jnp.where(cl == cmax, gi, N_GROUPS), axis=0, keepdims=True)
    g_prob = jnp.sum(jnp.where(gi == g_idx, cp, 0.0), axis=0, keepdims=True)

    fl = jnp.zeros((EXPERTS_PER_GROUP, tn), F32)
    for g in range(N_GROUPS):
        r0 = FINE_ROW0 + g * EXPERTS_PER_GROUP
        fl = jnp.where(g_idx == g, lt_ref[r0:r0 + EXPERTS_PER_GROUP, :], fl)
    fmax = jnp.max(fl, axis=0, keepdims=True)
    fe = jnp.exp(fl - fmax)
    fp = fe / jnp.sum(fe, axis=0, keepdims=True)
    ei = lax.broadcasted_iota(jnp.int32, (EXPERTS_PER_GROUP, tn), 0)
    p1 = jnp.max(fp, axis=0, keepdims=True)
    i1 = jnp.min(jnp.where(fp == p1, ei, EXPERTS_PER_GROUP), axis=0, keepdims=True)
    fp2 = jnp.where(ei == i1, -1.0, fp)
    p2 = jnp.max(fp2, axis=0, keepdims=True)
    i2 = jnp.min(jnp.where(fp2 == p2, ei, EXPERTS_PER_GROUP), axis=0, keepdims=True)
    tot = p1 + p2
    w1 = g_prob * p1 / tot
    w2 = g_prob * p2 / tot
    e1 = g_idx * EXPERTS_PER_GROUP + i1
    e2 = g_idx * EXPERTS_PER_GROUP + i2
    ri = lax.broadcasted_iota(jnp.int32, (8, tn), 0)
    e_ref[...] = jnp.where(ri == 0, e1, jnp.where(ri == 1, e2, 0))
    w_ref[...] = jnp.where(ri == 0, w1, jnp.where(ri == 1, w2, 0.0))

    xi = lax.broadcasted_iota(jnp.int32, (N_EXPERTS, tn), 0)
    hits = jnp.where(xi == e1, 1.0, 0.0) + jnp.where(xi == e2, 1.0, 0.0)
    total = carry_ref[...] + jnp.sum(hits, axis=1, keepdims=True)
    carry_ref[...] = total
    cnt_ref[...] = jnp.broadcast_to(total, cnt_ref.shape)


def _route_call(lt):
    n = lt.shape[1]
    tn = min(ROUTE_TN, n)
    tok = lambda i: (0, i)
    return pl.pallas_call(
        _route_kernel,
        grid=(n // tn,),
        in_specs=[pl.BlockSpec((ROUTER_ROWS, tn), tok)],
        out_specs=(pl.BlockSpec((8, tn), tok), pl.BlockSpec((8, tn), tok),
                   pl.BlockSpec((N_EXPERTS, LANES), lambda i: (0, 0))),
        out_shape=(jax.ShapeDtypeStruct((8, n), jnp.int32), jax.ShapeDtypeStruct((8, n), F32),
                   jax.ShapeDtypeStruct((N_EXPERTS, LANES), F32)),
        scratch_shapes=[pltpu.VMEM((N_EXPERTS, 1), F32)],
        compiler_params=pltpu.CompilerParams(dimension_semantics=("arbitrary",)),
        name="router",
    )(lt)


def _plan_kernel(starts_ref, vb_ref, ve_ref, vlo_ref, vhi_ref, *, rb):
    n_visits = vb_ref.shape[0]
    shift = rb.bit_length() - 1

    def expert_body(e, v):
        end = starts_ref[e + 1]

        def cond(state):
            return state[0] < end

        def body(state):
            r, v = state
            b = lax.shift_right_logical(r, shift)
            hi = jnp.minimum(end, (b + 1) * rb)
            vb_ref[v] = b
            ve_ref[v] = e
            vlo_ref[v] = r - b * rb
            vhi_ref[v] = hi - b * rb
            return hi, v + 1

        return lax.while_loop(cond, body, (starts_ref[e], v))[1]

    used = lax.fori_loop(0, N_EXPERTS, expert_body, 0)
    last_b = vb_ref[used - 1]
    last_e = ve_ref[used - 1]

    def pad(i, carry):
        vb_ref[i] = last_b
        ve_ref[i] = last_e
        vlo_ref[i] = 0
        vhi_ref[i] = 0
        return carry

    lax.fori_loop(used, n_visits, pad, 0)


def _plan_call(starts, n_assign, rb):
    assert rb & (rb - 1) == 0 and n_assign % rb == 0
    n_visits = n_assign // rb + N_EXPERTS - 1
    smem = pl.BlockSpec(memory_space=pltpu.SMEM)
    sds = jax.ShapeDtypeStruct((n_visits,), jnp.int32)
    return pl.pallas_call(
        functools.partial(_plan_kernel, rb=rb),
        in_specs=[smem], out_specs=(smem,) * 4, out_shape=(sds,) * 4, name="visit_plan",
    )(starts)


def _expert_kernel(vb_ref, ve_ref, vlo_ref, vhi_ref,
                   tok0_ref, tokn_ref, slot_ref, slotp_ref, x_hbm, wg_ref, wu_ref, wd_ref, y_hbm,
                   xin_ref, xf_ref, xb_ref, ys_ref, yout_ref, wgb_ref, wub_ref, wdb_ref,
                   state_ref, gsem, ssem):
    rb = xb_ref.shape[0]
    v = pl.program_id(0)
    b = vb_ref[v]
    lo = vlo_ref[v]
    hi = vhi_ref[v]
    unsent, in_flight = 0, 1

    def gather_row(idx_ref, r):
        return pltpu.make_async_copy(x_hbm.at[pl.ds(idx_ref[0, 0, r], 1)], xin_ref.at[pl.ds(r, 1)], gsem)

    def scatter_row(buf, idx_ref, r):
        return pltpu.make_async_copy(yout_ref.at[buf, pl.ds(r, 1)], y_hbm.at[pl.ds(idx_ref[0, 0, r], 1)], ssem.at[buf])

    def start_gather(idx_ref):
        def body(r, carry):
            gather_row(idx_ref, r).start()
            return carry
        lax.fori_loop(0, rb, body, 0, unroll=8)

    def wait_gather():
        def body(r, carry):
            gather_row(tok0_ref, r).wait()
            return carry
        lax.fori_loop(0, rb, body, 0, unroll=8)

    def wait_scatter(buf):
        def body(r, carry):
            scatter_row(buf, slot_ref, r).wait()
            return carry
        lax.fori_loop(0, rb, body, 0, unroll=8)
        state_ref[in_flight + buf] = 0

    def park_block(get_rows):
        buf = b & 1

        @pl.when(state_ref[in_flight + buf] == 1)
        def _():
            wait_scatter(buf)

        yout_ref[buf] = get_rows().reshape(rb, 1, D_MODEL)
        state_ref[unsent] = 1

    def mark_sent(buf):
        state_ref[unsent] = 0
        state_ref[in_flight + buf] = 1

    @pl.when(v == 0)
    def _():
        state_ref[unsent] = 0
        state_ref[in_flight] = 0
        state_ref[in_flight + 1] = 0
        start_gather(tok0_ref)

    @pl.when(hi > lo)
    def _():
        @pl.when(lo == 0)
        def _():
            wait_gather()
            xf_ref[...] = xin_ref[...].reshape(rb, D_MODEL)
            xb_ref[...] = xf_ref[...].astype(BF16)

        @pl.when((v == 0) | (ve_ref[v] != ve_ref[jnp.maximum(v - 1, 0)]))
        def _():
            wgb_ref[...] = wg_ref[...].astype(BF16)
            wub_ref[...] = wu_ref[...].astype(BF16)
            wdb_ref[...] = wd_ref[...].astype(BF16)

        def expert_rows():
            x = xb_ref[...]
            hidden = jax.nn.silu(_dot(x, wgb_ref[...])) * _dot(x, wub_ref[...])
            return _dot(hidden.astype(BF16), wdb_ref[...])

        whole = (lo == 0) & (hi == rb)
        prev_buf = (b + 1) & 1

        @pl.when(whole & (b > 0))
        def _():
            for r in range(rb):
                gather_row(tokn_ref, r).start()
                scatter_row(prev_buf, slotp_ref, r).start()
            mark_sent(prev_buf)
            y = expert_rows()
            park_block(lambda: y)

        @pl.when(whole & (b == 0))
        def _():
            for r in range(rb):
                gather_row(tokn_ref, r).start()
            y = expert_rows()
            park_block(lambda: y)

        def my_rows():
            rows = lax.broadcasted_iota(jnp.int32, (rb, 1), 0)
            return (rows >= lo) & (rows < hi)

        first_part = (lo == 0) & (hi < rb)

        @pl.when(first_part & (b > 0))
        def _():
            for r in range(rb):
                gather_row(tokn_ref, r).start()
                scatter_row(prev_buf, slotp_ref, r).start()
            mark_sent(prev_buf)
            ys_ref[...] = jnp.where(my_rows(), expert_rows(), 0.0)

        @pl.when(first_part & (b == 0))
        def _():
            for r in range(rb):
                gather_row(tokn_ref, r).start()
            ys_ref[...] = jnp.where(my_rows(), expert_rows(), 0.0)

        @pl.when(lo > 0)
        def _():
            ys_ref[...] = jnp.where(my_rows(), expert_rows(), ys_ref[...])

            @pl.when(hi == rb)
            def _():
                park_block(lambda: ys_ref[...])

    @pl.when(v == pl.num_programs(0) - 1)
    def _():
        last_buf = b & 1

        @pl.when(state_ref[unsent] == 1)
        def _():
            def body(r, carry):
                scatter_row(last_buf, slot_ref, r).start()
                return carry
            lax.fori_loop(0, rb, body, 0, unroll=8)
            mark_sent(last_buf)

        for buf in range(2):
            @pl.when(state_ref[in_flight + buf] == 1)
            def _():
                wait_scatter(buf)

        wait_gather()


def _expert_call(plan, tok_sorted, slot_sorted, x1rows, w_gate, w_up, w_down, rb):
    n_assign = tok_sorted.shape[0]
    n_blocks = n_assign // rb
    n_visits = plan[0].shape[0]
    tok3 = tok_sorted.reshape(n_blocks, 1, rb)
    slot3 = slot_sorted.reshape(n_blocks, 1, rb)
    w_map = lambda v, vb, ve, vlo, vhi: (ve[v], 0, 0)
    idx_block = (1, 1, rb)
    grid_spec = pltpu.PrefetchScalarGridSpec(
        num_scalar_prefetch=4,
        grid=(n_visits,),
        in_specs=[
            pl.BlockSpec(idx_block, lambda v, vb, ve, vlo, vhi: (0, 0, 0), memory_space=pltpu.SMEM),
            pl.BlockSpec(idx_block, lambda v, vb, ve, vlo, vhi: (jnp.minimum(vb[v] + 1, n_blocks - 1), 0, 0),
                         memory_space=pltpu.SMEM),
            pl.BlockSpec(idx_block, lambda v, vb, ve, vlo, vhi: (vb[v], 0, 0), memory_space=pltpu.SMEM),
            pl.BlockSpec(idx_block, lambda v, vb, ve, vlo, vhi: (jnp.maximum(vb[v] - 1, 0), 0, 0),
                         memory_space=pltpu.SMEM),
            pl.BlockSpec(memory_space=pl.ANY),
            pl.BlockSpec((None, D_MODEL, D_EXPERT), w_map),
            pl.BlockSpec((None, D_MODEL, D_EXPERT), w_map),
            pl.BlockSpec((None, D_EXPERT, D_MODEL), w_map),
        ],
        out_specs=pl.BlockSpec(memory_space=pl.ANY),
        scratch_shapes=[
            pltpu.VMEM((rb, 1, D_MODEL), F32),
            pltpu.VMEM((rb, D_MODEL), F32),
            pltpu.VMEM((rb, D_MODEL), BF16),
            pltpu.VMEM((rb, D_MODEL), F32),
            pltpu.VMEM((2, rb, 1, D_MODEL), F32),
            pltpu.VMEM((D_MODEL, D_EXPERT), BF16),
            pltpu.VMEM((D_MODEL, D_EXPERT), BF16),
            pltpu.VMEM((D_EXPERT, D_MODEL), BF16),
            pltpu.SMEM((3,), jnp.int32),
            pltpu.SemaphoreType.DMA(()),
            pltpu.SemaphoreType.DMA((2,)),
        ],
    )
    return pl.pallas_call(
        _expert_kernel,
        grid_spec=grid_spec,
        out_shape=jax.ShapeDtypeStruct((n_assign, 1, D_MODEL), F32),
        compiler_params=pltpu.CompilerParams(
            dimension_semantics=("arbitrary",), vmem_limit_bytes=VMEM_LIMIT),
        name="experts",
    )(*plan, tok3, tok3, slot3, slot3, x1rows, w_gate, w_up, w_down)


def _final_kernel(x_ref, y0_ref, y1_ref, w_ref, g_ref, b_ref, o_ref, xs_ref, ya_ref, yb_ref):
    tm = o_ref.shape[0]
    xs_ref[...] = x_ref[...].reshape(tm, D_MODEL)
    ya_ref[...] = y0_ref[...].reshape(tm, D_MODEL)
    yb_ref[...] = y1_ref[...].reshape(tm, D_MODEL)
    w = w_ref[...]
    moe = ya_ref[...] * w[:, 0:1] + yb_ref[...] * w[:, 1:2]
    o_ref[...] = _layer_norm_rows(ALPHA * xs_ref[...] + moe, g_ref[...], b_ref[...])


def _final_call(x1rows, y2, w_tok, g, b, row0, n):
    tm = FINAL_TM
    assert row0 % tm == 0 and n % tm == 0
    t0 = row0 // tm
    fixed = lambda i: (0, 0)
    return pl.pallas_call(
        _final_kernel,
        grid=(n // tm,),
        in_specs=[
            pl.BlockSpec((tm, 1, D_MODEL), lambda i: (t0 + i, 0, 0)),
            pl.BlockSpec((None, tm, 1, D_MODEL), lambda i: (0, t0 + i, 0, 0)),
            pl.BlockSpec((None, tm, 1, D_MODEL), lambda i: (1, t0 + i, 0, 0)),
            pl.BlockSpec((tm, TOP_K), lambda i: (t0 + i, 0)),
            pl.BlockSpec((1, D_MODEL), fixed),
            pl.BlockSpec((1, D_MODEL), fixed),
        ],
        out_specs=pl.BlockSpec((tm, D_MODEL), lambda i: (i, 0)),
        out_shape=jax.ShapeDtypeStruct((n, D_MODEL), F32),
        scratch_shapes=[pltpu.VMEM((tm, D_MODEL), F32)] * 3,
        compiler_params=pltpu.CompilerParams(
            dimension_semantics=("arbitrary",), vmem_limit_bytes=VMEM_LIMIT),
        name="combine_ln",
    )(x1rows, y2, y2, w_tok, g, b)


def _t5_bucket(rel):
    half = N_BUCKETS // 2
    exact = half // 2
    n = jnp.abs(rel)
    nf = jnp.maximum(n, 1).astype(F32)
    large = exact + (jnp.log(nf / exact) / math.log(MAX_DISTANCE / exact) * (half - exact)).astype(jnp.int32)
    large = jnp.minimum(large, half - 1)
    return jnp.where(rel > 0, half, 0) + jnp.where(n < exact, n, large)


def _window_bias(rel_bias):
    rel = jnp.arange(3 * BLOCK)[None, :] - BLOCK - jnp.arange(BLOCK)[:, None]
    onehot = (_t5_bucket(rel)[..., None] == jnp.arange(N_BUCKETS)).astype(F32)
    bias = jnp.einsum("qkb,bh->hqk", onehot, rel_bias.astype(F32), precision=lax.Precision.HIGHEST)
    band = jnp.abs(rel) <= WINDOW
    col = jnp.arange(3 * BLOCK)
    tables = []
    for edge in range(4):
        keep = band & ((col >= BLOCK) | (edge & 1 == 0))[None, :] & ((col < 2 * BLOCK) | (edge & 2 == 0))[None, :]
        tables.append(jnp.where(keep[None], bias * LOG2E, NEG_BIG))
    return jnp.stack(tables)


def _rope_tables(seq_len):
    rows = seq_len // GRID_W
    row_ids = jnp.repeat(jnp.arange(rows), GRID_W).astype(F32)
    col_ids = jnp.tile(jnp.arange(GRID_W), rows).astype(F32)
    half = HEAD_DIM // 2
    inv = 1.0 / (ROPE_THETA ** (jnp.arange(0, half, 2, dtype=F32) / half))
    ang = jnp.concatenate([row_ids[:, None] * inv, col_ids[:, None] * inv], -1)
    cos, sin = jnp.cos(ang), jnp.sin(ang)
    return jnp.concatenate([cos, cos], -1), jnp.concatenate([-sin, sin], -1)


def _deinterleave_cols(w, n_heads):
    d = w.shape[0]
    return w.reshape(d, n_heads, HEAD_DIM // 2, 2).transpose(0, 1, 3, 2).reshape(d, n_heads * HEAD_DIM)


def _deinterleave_gain(g):
    return g.reshape(HEAD_DIM // 2, 2).T.reshape(1, HEAD_DIM)


def _expert_row_block(n_assign):
    return MOE_RB if n_assign // N_EXPERTS >= 8 * MOE_RB else MOE_RB // 2


def _mixers(x, p):
    bsz, seq_len, d = x.shape
    cos_t, sin_t = _rope_tables(seq_len)
    xn, qa, ka, va, qb, kb, vb = _qkv_call(x.reshape(bsz * seq_len, d), p["emb_g"], p["emb_b"], p["w_qkv"],
                                           cos_t, sin_t, p["q_gain"], p["k_gain"], seq_len)
    oa = _attn_a_call(p["sink"], qa, ka, va, p["bias_a"], bsz, seq_len)
    ob = _attn_b_call(qb, kb, vb, bsz, seq_len)
    return xn, oa, ob


def _layer(x0, x1, p):
    d = x0.shape[-1]
    n0 = x0.shape[0] * x0.shape[1]
    n1 = x1.shape[0] * x1.shape[1]
    n = n0 + n1
    n_assign = n * TOP_K
    x1rows, lt = _merge_call(_mixers(x0, p), _mixers(x1, p), p["w_a"], p["w_b"], p["w_g"], p["b_g"], p["w_o"],
                             p["ln1_g"], p["ln1_b"], p["wr_hi"], p["wr_lo"], p["b_r"])
    e_rows, w_rows, cnt = _route_call(lt)
    e_flat = e_rows[:TOP_K].reshape(n_assign)
    assert N_EXPERTS * n_assign < 2 ** 31
    slot_sorted = lax.rem(lax.sort(e_flat * n_assign + jnp.arange(n_assign, dtype=jnp.int32)), n_assign)
    tok_sorted = jnp.where(slot_sorted >= n, slot_sorted - n, slot_sorted)
    ends = jnp.cumsum(cnt[:, 0].astype(jnp.int32))
    starts = jnp.concatenate([jnp.zeros((1,), jnp.int32), ends])
    rb = _expert_row_block(n_assign)
    plan = _plan_call(starts, n_assign, rb)
    y2 = _expert_call(plan, tok_sorted, slot_sorted, x1rows, p["w_gate"], p["w_up"], p["w_down"], rb)
    y2 = y2.reshape(TOP_K, n, 1, d)
    w_tok = w_rows[:TOP_K].T
    out0 = _final_call(x1rows, y2, w_tok, p["ln2_g"], p["ln2_b"], 0, n0)
    out1 = _final_call(x1rows, y2, w_tok, p["ln2_g"], p["ln2_b"], n0, n1)
    return out0.reshape(x0.shape), out1.reshape(x1.shape)


def kernel(x_prompt, x_sample, emb_ln_g, emb_ln_b, rel_bias, w_in, b_gate, sink_a, q_norm_g, k_norm_g,
           w_branch_a, w_branch_b, w_out, ln1_g, ln1_b, w_coarse, b_coarse, w_fine, b_fine,
           w_gate, w_up, w_down, ln2_g, ln2_b):
    l = 0
    w_in_l = w_in[l]
    qa_end = Q_COLS
    kva_end = qa_end + 2 * KV_COLS
    qb_end = kva_end + Q_COLS
    kb_end = qb_end + KV_COLS
    vb_end = kb_end + KV_COLS
    w_qkv = jnp.concatenate([
        w_in_l[:, :kva_end],
        _deinterleave_cols(w_in_l[:, kva_end:qb_end], N_HEADS),
        _deinterleave_cols(w_in_l[:, qb_end:kb_end], N_KV),
        w_in_l[:, kb_end:vb_end],
    ], axis=1).astype(BF16)
    w_router = jnp.zeros((D_MODEL, ROUTER_ROWS), F32)
    w_router = w_router.at[:, 0:N_GROUPS].set(w_coarse[l])
    w_router = w_router.at[:, FINE_ROW0:FINE_ROW0 + N_EXPERTS].set(w_fine[l])
    wr_hi = w_router.astype(BF16)
    wr_lo = (w_router - wr_hi.astype(F32)).astype(BF16)
    b_router = jnp.zeros((ROUTER_ROWS, 1), F32)
    b_router = b_router.at[0:N_GROUPS, 0].set(b_coarse[l].astype(F32))
    b_router = b_router.at[FINE_ROW0:FINE_ROW0 + N_EXPERTS, 0].set(b_fine[l].astype(F32))
    p = {
        "emb_g": emb_ln_g.reshape(1, D_MODEL), "emb_b": emb_ln_b.reshape(1, D_MODEL),
        "w_qkv": w_qkv,
        "q_gain": _deinterleave_gain(q_norm_g[l]), "k_gain": _deinterleave_gain(k_norm_g[l]),
        "sink": sink_a[l].astype(F32) * LOG2E, "bias_a": _window_bias(rel_bias),
        "w_a": w_branch_a[l].astype(BF16), "w_b": w_branch_b[l].astype(BF16),
        "w_g": w_in_l[:, vb_end:].astype(BF16), "b_g": b_gate[l].reshape(1, 2 * D_MODEL),
        "w_o": w_out[l].astype(BF16),
        "ln1_g": ln1_g[l].reshape(1, D_MODEL), "ln1_b": ln1_b[l].reshape(1, D_MODEL),
        "wr_hi": wr_hi, "wr_lo": wr_lo, "b_r": b_router,
        "w_gate": w_gate[l], "w_up": w_up[l], "w_down": w_down[l],
        "ln2_g": ln2_g[l].reshape(1, D_MODEL), "ln2_b": ln2_b[l].reshape(1, D_MODEL),
    }
    return _layer(x_prompt, x_sample, p)
```

```python
import functools
import math

import jax
import jax.numpy as jnp
from jax import lax
from jax.experimental import pallas as pl
from jax.experimental.pallas import tpu as pltpu

F32 = jnp.float32
BF16 = jnp.bfloat16

D_MODEL = 2048
HEAD_DIM = 128
N_HEADS = 8
N_KV = 2
GROUP = N_HEADS // N_KV
Q_COLS = N_HEADS * HEAD_DIM
KV_COLS = N_KV * HEAD_DIM
QKV_COLS = 2 * (Q_COLS + 2 * KV_COLS)
WINDOW = 128
BLOCK = 128
GRID_W = 64
ROPE_THETA = 10000.0
N_BUCKETS = 32
MAX_DISTANCE = 128
N_GROUPS = 4
EXPERTS_PER_GROUP = 8
N_EXPERTS = N_GROUPS * EXPERTS_PER_GROUP
TOP_K = 2
D_EXPERT = 512
LN_EPS = 1e-5
RMS_EPS = 1e-6
DEPTH = 1
ALPHA = (2 * DEPTH) ** 0.25
SCALE = HEAD_DIM ** -0.5
LOG2E = math.log2(math.e)
NEG_BIG = -1e30

LANES = 128
VMEM_LIMIT = 56 * 1024 * 1024
MERGE_VMEM_LIMIT = 59 * 1024 * 1024

QKV_TM = 512
ATT_A_TQ = 1024
ATT_B_TQ = 256
ATT_B_TK = 512
MERGE_TM = 256
MERGE_TC = 1024
ROUTE_TN = 1024
MOE_RB = 256
FINAL_TM = 512
ROUTER_ROWS = 128
FINE_ROW0 = 8


def _layer_norm_rows(x, g, b):
    mu = jnp.mean(x, axis=-1, keepdims=True)
    xc = x - mu
    var = jnp.mean(xc * xc, axis=-1, keepdims=True)
    return xc * lax.rsqrt(var + LN_EPS) * g + b


def _dot(a, b):
    return jnp.dot(a, b, preferred_element_type=F32)


def _dot_nt(a, b):
    return lax.dot_general(a, b, (((1,), (1,)), ((), ())), preferred_element_type=F32)


def _qkv_kernel(x_ref, g_ref, b_ref, w_ref, cos_ref, sin_ref, qg_ref, kg_ref,
                xn_ref, qa_ref, ka_ref, va_ref, qb_ref, kb_ref, vb_ref):
    xn = _layer_norm_rows(x_ref[...], g_ref[...], b_ref[...])
    xn_ref[...] = xn
    xb = xn.astype(BF16)
    cos = cos_ref[...]
    sin = sin_ref[...]

    def proj(c0, width):
        return _dot(xb, w_ref[:, c0:c0 + width])

    def norm_rope(h, gain):
        y = h * lax.rsqrt(jnp.mean(h * h, axis=-1, keepdims=True) + RMS_EPS) * gain
        return y * cos + pltpu.roll(y, HEAD_DIM // 2, 1) * sin

    c = Q_COLS + 2 * KV_COLS
    qg = qg_ref[...]
    for hp in range(N_HEADS // 2):
        hq2 = proj(c + 2 * hp * HEAD_DIM, 2 * HEAD_DIM)
        for h in (2 * hp, 2 * hp + 1):
            hq = hq2[:, (h - 2 * hp) * HEAD_DIM:(h - 2 * hp + 1) * HEAD_DIM]
            qb_ref[:, h * HEAD_DIM:(h + 1) * HEAD_DIM] = (norm_rope(hq, qg) * (SCALE * LOG2E)).astype(BF16)
    c += Q_COLS
    kg = kg_ref[...]
    hk2 = proj(c, KV_COLS)
    for h in range(N_KV):
        kb_ref[:, h * HEAD_DIM:(h + 1) * HEAD_DIM] = norm_rope(hk2[:, h * HEAD_DIM:(h + 1) * HEAD_DIM], kg).astype(BF16)
    c += KV_COLS
    hv2 = proj(c, KV_COLS)
    ones = jnp.ones((x_ref.shape[0], HEAD_DIM), BF16)
    for h in range(N_KV):
        vb_ref[:, 2 * h * HEAD_DIM:(2 * h + 1) * HEAD_DIM] = hv2[:, h * HEAD_DIM:(h + 1) * HEAD_DIM].astype(BF16)
        vb_ref[:, (2 * h + 1) * HEAD_DIM:(2 * h + 2) * HEAD_DIM] = ones
    c = 0
    qa_ref[...] = (proj(c, Q_COLS) * (SCALE * LOG2E)).astype(BF16)
    c += Q_COLS
    ka_ref[...] = proj(c, KV_COLS).astype(BF16)
    c += KV_COLS
    va_ref[...] = proj(c, KV_COLS).astype(BF16)


def _qkv_call(x2, g, b, w_qkv, cos_t, sin_t, qg, kg, seq_len):
    n = x2.shape[0]
    tm = QKV_TM
    pos_blocks = seq_len // tm
    row = lambda i: (i, 0)
    fixed = lambda i: (0, 0)
    pos = lambda i: (i % pos_blocks, 0)
    out_shapes = (
        jax.ShapeDtypeStruct((n, D_MODEL), F32),
        jax.ShapeDtypeStruct((n, Q_COLS), BF16),
        jax.ShapeDtypeStruct((n, KV_COLS), BF16),
        jax.ShapeDtypeStruct((n, KV_COLS), BF16),
        jax.ShapeDtypeStruct((n, Q_COLS), BF16),
        jax.ShapeDtypeStruct((n, KV_COLS), BF16),
        jax.ShapeDtypeStruct((n, 2 * KV_COLS), BF16),
    )
    return pl.pallas_call(
        _qkv_kernel,
        grid=(n // tm,),
        in_specs=[
            pl.BlockSpec((tm, D_MODEL), row),
            pl.BlockSpec((1, D_MODEL), fixed),
            pl.BlockSpec((1, D_MODEL), fixed),
            pl.BlockSpec((D_MODEL, QKV_COLS), fixed, pipeline_mode=pl.Buffered(1)),
            pl.BlockSpec((tm, HEAD_DIM), pos),
            pl.BlockSpec((tm, HEAD_DIM), pos),
            pl.BlockSpec((1, HEAD_DIM), fixed),
            pl.BlockSpec((1, HEAD_DIM), fixed),
        ],
        out_specs=(
            pl.BlockSpec((tm, D_MODEL), row),
            pl.BlockSpec((tm, Q_COLS), row),
            pl.BlockSpec((tm, KV_COLS), row),
            pl.BlockSpec((tm, KV_COLS), row),
            pl.BlockSpec((tm, Q_COLS), row),
            pl.BlockSpec((tm, KV_COLS), row),
            pl.BlockSpec((tm, 2 * KV_COLS), row),
        ),
        out_shape=out_shapes,
        compiler_params=pltpu.CompilerParams(
            dimension_semantics=("arbitrary",), vmem_limit_bytes=VMEM_LIMIT),
        name="ln_qkv",
    )(x2, g, b, w_qkv, cos_t, sin_t, qg, kg)


def _attn_a_kernel(sink_ref, q_ref, kc_ref, kp_ref, kn_ref, vc_ref, vp_ref, vn_ref, bias_ref,
                   o_ref, kcat_ref, vcat_ref, sa_ref, sb_ref, *, seq_len):
    tq = ATT_A_TQ
    i = pl.program_id(1)
    kcat_ref[0:BLOCK, :] = kp_ref[...]
    kcat_ref[BLOCK:BLOCK + tq, :] = kc_ref[...]
    kcat_ref[BLOCK + tq:, :] = kn_ref[...]
    vcat_ref[0:BLOCK, :] = vp_ref[...]
    vcat_ref[BLOCK:BLOCK + tq, :] = vc_ref[...]
    vcat_ref[BLOCK + tq:, :] = vn_ref[...]
    s_refs = (sa_ref, sb_ref)
    n_seq_blocks = seq_len // BLOCK

    def block_scores(j, s_ref):
        qj = q_ref[j * BLOCK:(j + 1) * BLOCK, :]
        for g in range(N_KV):
            qs = jnp.concatenate(
                [qj[:, h * HEAD_DIM:(h + 1) * HEAD_DIM] for h in range(g * GROUP, (g + 1) * GROUP)], axis=0)
            kw = kcat_ref[j * BLOCK:(j + 3) * BLOCK, g * HEAD_DIM:(g + 1) * HEAD_DIM]
            s_ref[g * GROUP * BLOCK:(g + 1) * GROUP * BLOCK, :] = _dot_nt(qs, kw)

    def block_out(j, s_ref):
        blk = i * (tq // BLOCK) + j
        edge = (blk == 0).astype(jnp.int32) + 2 * (blk == n_seq_blocks - 1).astype(jnp.int32)
        for g in range(N_KV):
            probs = []
            for h in range(g * GROUP, (g + 1) * GROUP):
                s = s_ref[h * BLOCK:(h + 1) * BLOCK, :] + bias_ref[edge, h]
                sk = sink_ref[h]
                m = jnp.maximum(jnp.max(s, axis=-1, keepdims=True), sk)
                p = jnp.exp2(s - m)
                denom = jnp.sum(p, axis=-1, keepdims=True) + jnp.exp2(sk - m)
                probs.append((p / denom).astype(BF16))
            vw = vcat_ref[j * BLOCK:(j + 3) * BLOCK, g * HEAD_DIM:(g + 1) * HEAD_DIM]
            o = _dot(jnp.concatenate(probs, axis=0), vw)
            for hh in range(GROUP):
                h = g * GROUP + hh
                o_ref[j * BLOCK:(j + 1) * BLOCK, h * HEAD_DIM:(h + 1) * HEAD_DIM] = (
                    o[hh * BLOCK:(hh + 1) * BLOCK, :].astype(BF16))

    n_blk = tq // BLOCK
    block_scores(0, s_refs[0])
    for j in range(n_blk):
        if j + 1 < n_blk:
            block_scores(j + 1, s_refs[(j + 1) % 2])
        block_out(j, s_refs[j % 2])


def _attn_a_call(sink, qa, ka, va, bias, bsz, seq_len):
    tq = ATT_A_TQ
    nb = seq_len // BLOCK
    bpt = tq // BLOCK
    q3 = qa.reshape(bsz, seq_len, Q_COLS)
    k3 = ka.reshape(bsz, seq_len, KV_COLS)
    v3 = va.reshape(bsz, seq_len, KV_COLS)
    cur = lambda b, i: (b, i, 0)
    prev = lambda b, i: (b, jnp.maximum(i * bpt - 1, 0), 0)
    nxt = lambda b, i: (b, jnp.minimum((i + 1) * bpt, nb - 1), 0)
    kv_cur = pl.BlockSpec((None, tq, KV_COLS), cur)
    kv_prev = pl.BlockSpec((None, BLOCK, KV_COLS), prev)
    kv_next = pl.BlockSpec((None, BLOCK, KV_COLS), nxt)
    out = pl.pallas_call(
        functools.partial(_attn_a_kernel, seq_len=seq_len),
        grid=(bsz, seq_len // tq),
        in_specs=[
            pl.BlockSpec(memory_space=pltpu.SMEM),
            pl.BlockSpec((None, tq, Q_COLS), cur),
            kv_cur, kv_prev, kv_next,
            kv_cur, kv_prev, kv_next,
            pl.BlockSpec((4, N_HEADS, BLOCK, 3 * BLOCK), lambda b, i: (0, 0, 0, 0)),
        ],
        out_specs=pl.BlockSpec((None, tq, Q_COLS), cur),
        out_shape=jax.ShapeDtypeStruct((bsz, seq_len, Q_COLS), BF16),
        scratch_shapes=[
            pltpu.VMEM((tq + 2 * BLOCK, KV_COLS), BF16),
            pltpu.VMEM((tq + 2 * BLOCK, KV_COLS), BF16),
            pltpu.VMEM((N_HEADS * BLOCK, 3 * BLOCK), F32),
            pltpu.VMEM((N_HEADS * BLOCK, 3 * BLOCK), F32),
        ],
        compiler_params=pltpu.CompilerParams(
            dimension_semantics=("arbitrary", "arbitrary"), vmem_limit_bytes=VMEM_LIMIT),
        name="attn_window",
    )(sink, q3, k3, k3, k3, v3, v3, v3, bias)
    return out.reshape(bsz * seq_len, Q_COLS)


def _attn_b_kernel(q_ref, qn_ref, k_ref, v_ref, o_ref, qs_ref, qsn_ref, m_ref, acc_ref,
                   s0_ref, s1_ref, p0_ref, p1_ref, a0_ref, a1_ref, *, seq_len):
    tq = ATT_B_TQ
    tk = ATT_B_TK
    n_chunks = seq_len // tk
    rows = GROUP * tq
    for h in range(GROUP):
        qs_ref[h * tq:(h + 1) * tq, :] = q_ref[:, h * HEAD_DIM:(h + 1) * HEAD_DIM]
        qsn_ref[h * tq:(h + 1) * tq, :] = qn_ref[:, h * HEAD_DIM:(h + 1) * HEAD_DIM]

    def scores(q_src, c, s_ref):
        k0 = pl.multiple_of(c * tk, tk)
        s_ref[...] = _dot_nt(q_src[...], k_ref[pl.ds(k0, tk), :])

    def softmax(s_ref, p_ref, a_ref, first=False):
        m_tile = s_ref[:, 0:LANES]
        for j in range(1, tk // LANES):
            m_tile = jnp.maximum(m_tile, s_ref[:, j * LANES:(j + 1) * LANES])
        m_cur = jnp.broadcast_to(jnp.max(m_tile, axis=-1, keepdims=True), (rows, LANES))
        if first:
            m_new = m_cur
            a_ref[...] = jnp.zeros((rows, LANES), F32)
        else:
            m_prev = m_ref[...]
            m_new = jnp.maximum(m_prev, m_cur)
            a_ref[...] = jnp.exp2(m_prev - m_new)
        m_ref[...] = m_new
        for j in range(tk // LANES):
            p_ref[:, j * LANES:(j + 1) * LANES] = jnp.exp2(s_ref[:, j * LANES:(j + 1) * LANES] - m_new).astype(BF16)

    def weighted_values(c, p_ref, a_ref):
        k0 = pl.multiple_of(c * tk, tk)
        pv = _dot(p_ref[...], v_ref[pl.ds(k0, tk), :])
        a = a_ref[...]
        acc_ref[:, 0:HEAD_DIM] = a * acc_ref[:, 0:HEAD_DIM] + pv[:, 0:HEAD_DIM]
        acc_ref[:, HEAD_DIM:] = a * acc_ref[:, HEAD_DIM:] + pv[:, HEAD_DIM:]

    @pl.when(pl.program_id(2) == 0)
    def _():
        scores(qs_ref, 0, s0_ref)
        scores(qs_ref, 1, s1_ref)
        softmax(s0_ref, p0_ref, a0_ref, first=True)

    acc_ref[...] = jnp.zeros(acc_ref.shape, F32)

    def body(t, carry):
        c = 2 * t + 1
        scores(qs_ref, c + 1, s0_ref)
        softmax(s1_ref, p1_ref, a1_ref)
        weighted_values(c - 1, p0_ref, a0_ref)
        scores(qs_ref, c + 2, s1_ref)
        softmax(s0_ref, p0_ref, a0_ref)
        weighted_values(c, p1_ref, a1_ref)
        return carry

    lax.fori_loop(0, (n_chunks - 2) // 2, body, 0, unroll=True)
    scores(qsn_ref, 0, s0_ref)
    softmax(s1_ref, p1_ref, a1_ref)
    weighted_values(n_chunks - 2, p0_ref, a0_ref)
    scores(qsn_ref, 1, s1_ref)
    softmax(s0_ref, p0_ref, a0_ref, first=True)
    weighted_values(n_chunks - 1, p1_ref, a1_ref)
    o = acc_ref[:, 0:HEAD_DIM] / acc_ref[:, HEAD_DIM:]
    for h in range(GROUP):
        o_ref[:, h * HEAD_DIM:(h + 1) * HEAD_DIM] = o[h * tq:(h + 1) * tq, :].astype(BF16)


def _attn_b_call(qb, kb, vb, bsz, seq_len):
    tq = ATT_B_TQ
    tk = ATT_B_TK
    assert seq_len % tk == 0 and (seq_len // tk) % 2 == 0, "dense mixer pipeline needs an even number of key chunks"
    q3 = qb.reshape(bsz, seq_len, Q_COLS)
    k3 = kb.reshape(bsz, seq_len, KV_COLS)
    v3 = vb.reshape(bsz, seq_len, 2 * KV_COLS)
    gcols = GROUP * HEAD_DIM
    n_q = seq_len // tq
    rows = GROUP * tq
    out = pl.pallas_call(
        functools.partial(_attn_b_kernel, seq_len=seq_len),
        grid=(bsz, N_KV, n_q),
        in_specs=[
            pl.BlockSpec((None, tq, gcols), lambda b, g, i: (b, i, g)),
            pl.BlockSpec((None, tq, gcols), lambda b, g, i: (b, jnp.minimum(i + 1, n_q - 1), g)),
            pl.BlockSpec((None, seq_len, HEAD_DIM), lambda b, g, i: (b, 0, g)),
            pl.BlockSpec((None, seq_len, 2 * HEAD_DIM), lambda b, g, i: (b, 0, g)),
        ],
        out_specs=pl.BlockSpec((None, tq, gcols), lambda b, g, i: (b, i, g)),
        out_shape=jax.ShapeDtypeStruct((bsz, seq_len, Q_COLS), BF16),
        scratch_shapes=[
            pltpu.VMEM((rows, HEAD_DIM), BF16),
            pltpu.VMEM((rows, HEAD_DIM), BF16),
            pltpu.VMEM((rows, LANES), F32),
            pltpu.VMEM((rows, 2 * HEAD_DIM), F32),
            pltpu.VMEM((rows, tk), F32),
            pltpu.VMEM((rows, tk), F32),
            pltpu.VMEM((rows, tk), BF16),
            pltpu.VMEM((rows, tk), BF16),
            pltpu.VMEM((rows, LANES), F32),
            pltpu.VMEM((rows, LANES), F32),
        ],
        compiler_params=pltpu.CompilerParams(
            dimension_semantics=("arbitrary", "arbitrary", "arbitrary"), vmem_limit_bytes=VMEM_LIMIT),
        name="attn_dense",
    )(q3, q3, k3, v3)
    return out.reshape(bsz * seq_len, Q_COLS)


def _merge_kernel(xn0_ref, oa0_ref, ob0_ref, xn1_ref, oa1_ref, ob1_ref, *rest, tiles0):
    i = pl.program_id(0)

    @pl.when(i < tiles0)
    def _():
        _merge_tile(xn0_ref, oa0_ref, ob0_ref, *rest)

    @pl.when(i >= tiles0)
    def _():
        _merge_tile(xn1_ref, oa1_ref, ob1_ref, *rest)


def _merge_tile(xn_ref, oa_ref, ob_ref, wa_ref, wb_ref, wg_ref, bg_ref, wo_ref,
                g_ref, b_ref, wrh_ref, wrl_ref, br_ref, x1rows_ref, x1pack_ref, lt_ref, acc_ref):
    tc = MERGE_TC
    xb = xn_ref[...].astype(BF16)
    oa = oa_ref[...]
    ob = ob_ref[...]
    for c in range(D_MODEL // tc):
        lo, hi = c * tc, (c + 1) * tc
        ga = jax.nn.sigmoid(_dot(xb, wg_ref[:, lo:hi]) + bg_ref[:, lo:hi])
        gb = jax.nn.sigmoid(_dot(xb, wg_ref[:, D_MODEL + lo:D_MODEL + hi]) + bg_ref[:, D_MODEL + lo:D_MODEL + hi])
        mixed = (ga * _dot(oa, wa_ref[:, lo:hi]) + gb * _dot(ob, wb_ref[:, lo:hi])).astype(BF16)
        part = _dot(mixed, wo_ref[lo:hi, :])
        if c == 0:
            acc_ref[...] = part
        else:
            acc_ref[...] += part
    x1 = _layer_norm_rows(ALPHA * xn_ref[...] + acc_ref[...], g_ref[...], b_ref[...])
    x1rows_ref[...] = x1.reshape(x1.shape[0], 1, D_MODEL)
    xh = x1.astype(BF16)
    bits = lax.bitcast_convert_type(xh.astype(F32), jnp.uint32)
    half = D_MODEL // 2
    packed = (bits[:, :half] >> 16) | (bits[:, half:] & jnp.uint32(0xFFFF0000))
    x1pack_ref[...] = packed.reshape(packed.shape[0], 1, half)
    xl = (x1 - xh.astype(F32)).astype(BF16)
    wrh = wrh_ref[...]
    logits = _dot(xh, wrh) + (_dot(xl, wrh) + _dot(xh, wrl_ref[...]))
    lt_ref[...] = logits.T + br_ref[...]


def _merge_call(batch0, batch1, wa, wb, wg, bg, wo, g, b, wr_hi, wr_lo, br):
    tm = MERGE_TM
    tiles0 = batch0[0].shape[0] // tm
    n = batch0[0].shape[0] + batch1[0].shape[0]
    rows0 = lambda i: (jnp.minimum(i, tiles0 - 1), 0)
    rows1 = lambda i: (jnp.maximum(i - tiles0, 0), 0)
    fixed = lambda i: (0, 0)

    def resident(shape):
        return pl.BlockSpec(shape, fixed, pipeline_mode=pl.Buffered(1))

    return pl.pallas_call(
        functools.partial(_merge_kernel, tiles0=tiles0),
        grid=(n // tm,),
        in_specs=[
            pl.BlockSpec((tm, D_MODEL), rows0),
            pl.BlockSpec((tm, Q_COLS), rows0),
            pl.BlockSpec((tm, Q_COLS), rows0),
            pl.BlockSpec((tm, D_MODEL), rows1),
            pl.BlockSpec((tm, Q_COLS), rows1),
            pl.BlockSpec((tm, Q_COLS), rows1),
            resident((Q_COLS, D_MODEL)),
            resident((Q_COLS, D_MODEL)),
            resident((D_MODEL, 2 * D_MODEL)),
            resident((1, 2 * D_MODEL)),
            resident((D_MODEL, D_MODEL)),
            resident((1, D_MODEL)),
            resident((1, D_MODEL)),
            resident((D_MODEL, ROUTER_ROWS)),
            resident((D_MODEL, ROUTER_ROWS)),
            resident((ROUTER_ROWS, 1)),
        ],
        out_specs=(
            pl.BlockSpec((tm, 1, D_MODEL), lambda i: (i, 0, 0)),
            pl.BlockSpec((tm, 1, D_MODEL // 2), lambda i: (i, 0, 0)),
            pl.BlockSpec((ROUTER_ROWS, tm), lambda i: (0, i)),
        ),
        out_shape=(
            jax.ShapeDtypeStruct((n, 1, D_MODEL), F32),
            jax.ShapeDtypeStruct((n, 1, D_MODEL // 2), jnp.uint32),
            jax.ShapeDtypeStruct((ROUTER_ROWS, n), F32),
        ),
        scratch_shapes=[pltpu.VMEM((tm, D_MODEL), F32)],
        compiler_params=pltpu.CompilerParams(
            dimension_semantics=("arbitrary",), vmem_limit_bytes=MERGE_VMEM_LIMIT),
        name="merge_ln_router",
    )(*batch0, *batch1, wa, wb, wg, bg, wo, g, b, wr_hi, wr_lo, br)


def _route_kernel(lt_ref, e_ref, w_ref, cnt_ref, carry_ref):
    tn = lt_ref.shape[1]

    @pl.when(pl.program_id(0) == 0)
    def _():
        carry_ref[...] = jnp.zeros(carry_ref.shape, F32)

    cl = lt_ref[0:N_GROUPS, :]
    gi = lax.broadcasted_iota(jnp.int32, (N_GROUPS, tn), 0)
    cmax = jnp.max(cl, axis=0, keepdims=True)
    ce = jnp.exp(cl - cmax)
    cp = ce / jnp.sum(ce, axis=0, keepdims=True)
    g_idx = jnp.min(jnp.where(cl == cmax, gi, N_GROUPS), axis=0, keepdims=True)
    g_prob = jnp.sum(jnp.where(gi == g_idx, cp, 0.0), axis=0, keepdims=True)

    fl = jnp.zeros((EXPERTS_PER_GROUP, tn), F32)
    for g in range(N_GROUPS):
        r0 = FINE_ROW0 + g * EXPERTS_PER_GROUP
        fl = jnp.where(g_idx == g, lt_ref[r0:r0 + EXPERTS_PER_GROUP, :], fl)
    fmax = jnp.max(fl, axis=0, keepdims=True)
    fe = jnp.exp(fl - fmax)
    fp = fe / jnp.sum(fe, axis=0, keepdims=True)
    ei = lax.broadcasted_iota(jnp.int32, (EXPERTS_PER_GROUP, tn), 0)
    p1 = jnp.max(fp, axis=0, keepdims=True)
    i1 = jnp.min(jnp.where(fp == p1, ei, EXPERTS_PER_GROUP), axis=0, keepdims=True)
    fp2 = jnp.where(ei == i1, -1.0, fp)
    p2 = jnp.max(fp2, axis=0, keepdims=True)
    i2 = jnp.min(jnp.where(fp2 == p2, ei, EXPERTS_PER_GROUP), axis=0, keepdims=True)
    tot = p1 + p2
    w1 = g_prob * p1 / tot
    w2 = g_prob * p2 / tot
    e1 = g_idx * EXPERTS_PER_GROUP + i1
    e2 = g_idx * EXPERTS_PER_GROUP + i2
    ri = lax.broadcasted_iota(jnp.int32, (8, tn), 0)
    e_ref[...] = jnp.where(ri == 0, e1, jnp.where(ri == 1, e2, 0))
    w_ref[...] = jnp.where(ri == 0, w1, jnp.where(ri == 1, w2, 0.0))

    xi = lax.broadcasted_iota(jnp.int32, (N_EXPERTS, tn), 0)
    hits = jnp.where(xi == e1, 1.0, 0.0) + jnp.where(xi == e2, 1.0, 0.0)
    total = carry_ref[...] + jnp.sum(hits, axis=1, keepdims=True)
    carry_ref[...] = total
    cnt_ref[...] = jnp.broadcast_to(total, cnt_ref.shape)


def _route_call(lt):
    n = lt.shape[1]
    tn = min(ROUTE_TN, n)
    tok = lambda i: (0, i)
    return pl.pallas_call(
        _route_kernel,
        grid=(n // tn,),
        in_specs=[pl.BlockSpec((ROUTER_ROWS, tn), tok)],
        out_specs=(pl.BlockSpec((8, tn), tok), pl.BlockSpec((8, tn), tok),
                   pl.BlockSpec((N_EXPERTS, LANES), lambda i: (0, 0))),
        out_shape=(jax.ShapeDtypeStruct((8, n), jnp.int32), jax.ShapeDtypeStruct((8, n), F32),
                   jax.ShapeDtypeStruct((N_EXPERTS, LANES), F32)),
        scratch_shapes=[pltpu.VMEM((N_EXPERTS, 1), F32)],
        compiler_params=pltpu.CompilerParams(dimension_semantics=("arbitrary",)),
        name="router",
    )(lt)


def _plan_kernel(starts_ref, vb_ref, ve_ref, vlo_ref, vhi_ref, *, rb):
    n_visits = vb_ref.shape[0]
    shift = rb.bit_length() - 1

    def expert_body(e, v):
        end = starts_ref[e + 1]

        def cond(state):
            return state[0] < end

        def body(state):
            r, v = state
            b = lax.shift_right_logical(r, shift)
            hi = jnp.minimum(end, (b + 1) * rb)
            vb_ref[v] = b
            ve_ref[v] = e
            vlo_ref[v] = r - b * rb
            vhi_ref[v] = hi - b * rb
            return hi, v + 1

        return lax.while_loop(cond, body, (starts_ref[e], v))[1]

    used = lax.fori_loop(0, N_EXPERTS, expert_body, 0)
    last_b = vb_ref[used - 1]
    last_e = ve_ref[used - 1]

    def pad(i, carry):
        vb_ref[i] = last_b
        ve_ref[i] = last_e
        vlo_ref[i] = 0
        vhi_ref[i] = 0
        return carry

    lax.fori_loop(used, n_visits, pad, 0)


def _plan_call(starts, n_assign, rb):
    assert rb & (rb - 1) == 0 and n_assign % rb == 0
    n_visits = n_assign // rb + N_EXPERTS - 1
    smem = pl.BlockSpec(memory_space=pltpu.SMEM)
    sds = jax.ShapeDtypeStruct((n_visits,), jnp.int32)
    return pl.pallas_call(
        functools.partial(_plan_kernel, rb=rb),
        in_specs=[smem], out_specs=(smem,) * 4, out_shape=(sds,) * 4, name="visit_plan",
    )(starts)


def _expert_kernel(vb_ref, ve_ref, vlo_ref, vhi_ref,
                   tok0_ref, tokn_ref, slot_ref, slotp_ref, x_hbm, wg_ref, wu_ref, wd_ref, y_hbm,
                   xin_ref, xf_ref, xb_ref, ys_ref, yout_ref, wgb_ref, wub_ref, wdb_ref,
                   state_ref, gsem, ssem):
    rb = xb_ref.shape[0]
    v = pl.program_id(0)
    b = vb_ref[v]
    lo = vlo_ref[v]
    hi = vhi_ref[v]
    unsent, in_flight = 0, 1

    def gather_row(idx_ref, r):
        return pltpu.make_async_copy(x_hbm.at[pl.ds(idx_ref[0, 0, r], 1)], xin_ref.at[pl.ds(r, 1)], gsem)

    def scatter_row(buf, idx_ref, r):
        return pltpu.make_async_copy(yout_ref.at[buf, pl.ds(r, 1)], y_hbm.at[pl.ds(idx_ref[0, 0, r], 1)], ssem.at[buf])

    def start_gather(idx_ref):
        def body(r, carry):
            gather_row(idx_ref, r).start()
            return carry
        lax.fori_loop(0, rb, body, 0, unroll=8)

    def wait_gather():
        def body(r, carry):
            gather_row(tok0_ref, r).wait()
            return carry
        lax.fori_loop(0, rb, body, 0, unroll=8)

    def wait_scatter(buf):
        def body(r, carry):
            scatter_row(buf, slot_ref, r).wait()
            return carry
        lax.fori_loop(0, rb, body, 0, unroll=8)
        state_ref[in_flight + buf] = 0

    def park_block(get_rows):
        buf = b & 1

        @pl.when(state_ref[in_flight + buf] == 1)
        def _():
            wait_scatter(buf)

        yout_ref[buf] = get_rows().reshape(rb, 1, D_MODEL)
        state_ref[unsent] = 1

    def mark_sent(buf):
        state_ref[unsent] = 0
        state_ref[in_flight + buf] = 1

    @pl.when(v == 0)
    def _():
        state_ref[unsent] = 0
        state_ref[in_flight] = 0
        state_ref[in_flight + 1] = 0
        start_gather(tok0_ref)

    @pl.when(hi > lo)
    def _():
        @pl.when(lo == 0)
        def _():
            wait_gather()
            half = D_MODEL // 2
            xf_ref[...] = xin_ref[...].reshape(rb, half)
            words = xf_ref[...]
            xb_ref[:, :half] = lax.bitcast_convert_type(words << 16, F32).astype(BF16)
            xb_ref[:, half:] = lax.bitcast_convert_type(words & jnp.uint32(0xFFFF0000), F32).astype(BF16)

        @pl.when((v == 0) | (ve_ref[v] != ve_ref[jnp.maximum(v - 1, 0)]))
        def _():
            wgb_ref[...] = wg_ref[...].astype(BF16)
            wub_ref[...] = wu_ref[...].astype(BF16)
            wdb_ref[...] = wd_ref[...].astype(BF16)

        def expert_rows():
            x = xb_ref[...]
            hidden = jax.nn.silu(_dot(x, wgb_ref[...])) * _dot(x, wub_ref[...])
            return _dot(hidden.astype(BF16), wdb_ref[...])

        whole = (lo == 0) & (hi == rb)
        prev_buf = (b + 1) & 1

        @pl.when(whole & (b > 0))
        def _():
            for r in range(rb):
                gather_row(tokn_ref, r).start()
                scatter_row(prev_buf, slotp_ref, r).start()
            mark_sent(prev_buf)
            y = expert_rows()
            park_block(lambda: y)

        @pl.when(whole & (b == 0))
        def _():
            for r in range(rb):
                gather_row(tokn_ref, r).start()
            y = expert_rows()
            park_block(lambda: y)

        def my_rows():
            rows = lax.broadcasted_iota(jnp.int32, (rb, 1), 0)
            return (rows >= lo) & (rows < hi)

        first_part = (lo == 0) & (hi < rb)

        @pl.when(first_part & (b > 0))
        def _():
            for r in range(rb):
                gather_row(tokn_ref, r).start()
                scatter_row(prev_buf, slotp_ref, r).start()
            mark_sent(prev_buf)
            ys_ref[...] = jnp.where(my_rows(), expert_rows(), 0.0)

        @pl.when(first_part & (b == 0))
        def _():
            for r in range(rb):
                gather_row(tokn_ref, r).start()
            ys_ref[...] = jnp.where(my_rows(), expert_rows(), 0.0)

        @pl.when(lo > 0)
        def _():
            ys_ref[...] = jnp.where(my_rows(), expert_rows(), ys_ref[...])

            @pl.when(hi == rb)
            def _():
                park_block(lambda: ys_ref[...])

    @pl.when(v == pl.num_programs(0) - 1)
    def _():
        last_buf = b & 1

        @pl.when(state_ref[unsent] == 1)
        def _():
            def body(r, carry):
                scatter_row(last_buf, slot_ref, r).start()
                return carry
            lax.fori_loop(0, rb, body, 0, unroll=8)
            mark_sent(last_buf)

        for buf in range(2):
            @pl.when(state_ref[in_flight + buf] == 1)
            def _():
                wait_scatter(buf)

        wait_gather()


def _expert_call(plan, tok_sorted, slot_sorted, x1rows, w_gate, w_up, w_down, rb):
    n_assign = tok_sorted.shape[0]
    n_blocks = n_assign // rb
    n_visits = plan[0].shape[0]
    tok3 = tok_sorted.reshape(n_blocks, 1, rb)
    slot3 = slot_sorted.reshape(n_blocks, 1, rb)
    w_map = lambda v, vb, ve, vlo, vhi: (ve[v], 0, 0)
    idx_block = (1, 1, rb)
    grid_spec = pltpu.PrefetchScalarGridSpec(
        num_scalar_prefetch=4,
        grid=(n_visits,),
        in_specs=[
            pl.BlockSpec(idx_block, lambda v, vb, ve, vlo, vhi: (0, 0, 0), memory_space=pltpu.SMEM),
            pl.BlockSpec(idx_block, lambda v, vb, ve, vlo, vhi: (jnp.minimum(vb[v] + 1, n_blocks - 1), 0, 0),
                         memory_space=pltpu.SMEM),
            pl.BlockSpec(idx_block, lambda v, vb, ve, vlo, vhi: (vb[v], 0, 0), memory_space=pltpu.SMEM),
            pl.BlockSpec(idx_block, lambda v, vb, ve, vlo, vhi: (jnp.maximum(vb[v] - 1, 0), 0, 0),
                         memory_space=pltpu.SMEM),
            pl.BlockSpec(memory_space=pl.ANY),
            pl.BlockSpec((None, D_MODEL, D_EXPERT), w_map),
            pl.BlockSpec((None, D_MODEL, D_EXPERT), w_map),
            pl.BlockSpec((None, D_EXPERT, D_MODEL), w_map),
        ],
        out_specs=pl.BlockSpec(memory_space=pl.ANY),
        scratch_shapes=[
            pltpu.VMEM((rb, 1, D_MODEL // 2), jnp.uint32),
            pltpu.VMEM((rb, D_MODEL // 2), jnp.uint32),
            pltpu.VMEM((rb, D_MODEL), BF16),
            pltpu.VMEM((rb, D_MODEL), F32),
            pltpu.VMEM((2, rb, 1, D_MODEL), F32),
            pltpu.VMEM((D_MODEL, D_EXPERT), BF16),
            pltpu.VMEM((D_MODEL, D_EXPERT), BF16),
            pltpu.VMEM((D_EXPERT, D_MODEL), BF16),
            pltpu.SMEM((3,), jnp.int32),
            pltpu.SemaphoreType.DMA(()),
            pltpu.SemaphoreType.DMA((2,)),
        ],
    )
    return pl.pallas_call(
        _expert_kernel,
        grid_spec=grid_spec,
        out_shape=jax.ShapeDtypeStruct((n_assign, 1, D_MODEL), F32),
        compiler_params=pltpu.CompilerParams(
            dimension_semantics=("arbitrary",), vmem_limit_bytes=VMEM_LIMIT),
        name="experts",
    )(*plan, tok3, tok3, slot3, slot3, x1rows, w_gate, w_up, w_down)


def _final_kernel(x_ref, y0_ref, y1_ref, w_ref, g_ref, b_ref, o_ref, xs_ref, ya_ref, yb_ref):
    tm = o_ref.shape[0]
    xs_ref[...] = x_ref[...].reshape(tm, D_MODEL)
    ya_ref[...] = y0_ref[...].reshape(tm, D_MODEL)
    yb_ref[...] = y1_ref[...].reshape(tm, D_MODEL)
    w = w_ref[...]
    moe = ya_ref[...] * w[:, 0:1] + yb_ref[...] * w[:, 1:2]
    o_ref[...] = _layer_norm_rows(ALPHA * xs_ref[...] + moe, g_ref[...], b_ref[...])


def _final_call(x1rows, y2, w_tok, g, b, row0, n):
    tm = FINAL_TM
    assert row0 % tm == 0 and n % tm == 0
    t0 = row0 // tm
    fixed = lambda i: (0, 0)
    return pl.pallas_call(
        _final_kernel,
        grid=(n // tm,),
        in_specs=[
            pl.BlockSpec((tm, 1, D_MODEL), lambda i: (t0 + i, 0, 0)),
            pl.BlockSpec((None, tm, 1, D_MODEL), lambda i: (0, t0 + i, 0, 0)),
            pl.BlockSpec((None, tm, 1, D_MODEL), lambda i: (1, t0 + i, 0, 0)),
            pl.BlockSpec((tm, TOP_K), lambda i: (t0 + i, 0)),
            pl.BlockSpec((1, D_MODEL), fixed),
            pl.BlockSpec((1, D_MODEL), fixed),
        ],
        out_specs=pl.BlockSpec((tm, D_MODEL), lambda i: (i, 0)),
        out_shape=jax.ShapeDtypeStruct((n, D_MODEL), F32),
        scratch_shapes=[pltpu.VMEM((tm, D_MODEL), F32)] * 3,
        compiler_params=pltpu.CompilerParams(
            dimension_semantics=("arbitrary",), vmem_limit_bytes=VMEM_LIMIT),
        name="combine_ln",
    )(x1rows, y2, y2, w_tok, g, b)


def _t5_bucket(rel):
    half = N_BUCKETS // 2
    exact = half // 2
    n = jnp.abs(rel)
    nf = jnp.maximum(n, 1).astype(F32)
    large = exact + (jnp.log(nf / exact) / math.log(MAX_DISTANCE / exact) * (half - exact)).astype(jnp.int32)
    large = jnp.minimum(large, half - 1)
    return jnp.where(rel > 0, half, 0) + jnp.where(n < exact, n, large)


def _window_bias(rel_bias):
    rel = jnp.arange(3 * BLOCK)[None, :] - BLOCK - jnp.arange(BLOCK)[:, None]
    onehot = (_t5_bucket(rel)[..., None] == jnp.arange(N_BUCKETS)).astype(F32)
    bias = jnp.einsum("qkb,bh->hqk", onehot, rel_bias.astype(F32), precision=lax.Precision.HIGHEST)
    band = jnp.abs(rel) <= WINDOW
    col = jnp.arange(3 * BLOCK)
    tables = []
    for edge in range(4):
        keep = band & ((col >= BLOCK) | (edge & 1 == 0))[None, :] & ((col < 2 * BLOCK) | (edge & 2 == 0))[None, :]
        tables.append(jnp.where(keep[None], bias * LOG2E, NEG_BIG))
    return jnp.stack(tables)


def _rope_tables(seq_len):
    rows = seq_len // GRID_W
    row_ids = jnp.repeat(jnp.arange(rows), GRID_W).astype(F32)
    col_ids = jnp.tile(jnp.arange(GRID_W), rows).astype(F32)
    half = HEAD_DIM // 2
    inv = 1.0 / (ROPE_THETA ** (jnp.arange(0, half, 2, dtype=F32) / half))
    ang = jnp.concatenate([row_ids[:, None] * inv, col_ids[:, None] * inv], -1)
    cos, sin = jnp.cos(ang), jnp.sin(ang)
    return jnp.concatenate([cos, cos], -1), jnp.concatenate([-sin, sin], -1)


def _deinterleave_cols(w, n_heads):
    d = w.shape[0]
    return w.reshape(d, n_heads, HEAD_DIM // 2, 2).transpose(0, 1, 3, 2).reshape(d, n_heads * HEAD_DIM)


def _deinterleave_gain(g):
    return g.reshape(HEAD_DIM // 2, 2).T.reshape(1, HEAD_DIM)


def _expert_row_block(n_assign):
    return MOE_RB if n_assign // N_EXPERTS >= 8 * MOE_RB else MOE_RB // 2


def _mixers(x, p):
    bsz, seq_len, d = x.shape
    cos_t, sin_t = _rope_tables(seq_len)
    xn, qa, ka, va, qb, kb, vb = _qkv_call(x.reshape(bsz * seq_len, d), p["emb_g"], p["emb_b"], p["w_qkv"],
                                           cos_t, sin_t, p["q_gain"], p["k_gain"], seq_len)
    oa = _attn_a_call(p["sink"], qa, ka, va, p["bias_a"], bsz, seq_len)
    ob = _attn_b_call(qb, kb, vb, bsz, seq_len)
    return xn, oa, ob


def _layer(x0, x1, p):
    d = x0.shape[-1]
    n0 = x0.shape[0] * x0.shape[1]
    n1 = x1.shape[0] * x1.shape[1]
    n = n0 + n1
    n_assign = n * TOP_K
    x1rows, x1pack, lt = _merge_call(_mixers(x0, p), _mixers(x1, p), p["w_a"], p["w_b"], p["w_g"], p["b_g"], p["w_o"],
                             p["ln1_g"], p["ln1_b"], p["wr_hi"], p["wr_lo"], p["b_r"])
    e_rows, w_rows, cnt = _route_call(lt)
    e_flat = e_rows[:TOP_K].reshape(n_assign)
    assert N_EXPERTS * n_assign < 2 ** 31
    slot_sorted = lax.rem(lax.sort(e_flat * n_assign + jnp.arange(n_assign, dtype=jnp.int32)), n_assign)
    tok_sorted = jnp.where(slot_sorted >= n, slot_sorted - n, slot_sorted)
    ends = jnp.cumsum(cnt[:, 0].astype(jnp.int32))
    starts = jnp.concatenate([jnp.zeros((1,), jnp.int32), ends])
    rb = _expert_row_block(n_assign)
    plan = _plan_call(starts, n_assign, rb)
    y2 = _expert_call(plan, tok_sorted, slot_sorted, x1pack, p["w_gate"], p["w_up"], p["w_down"], rb)
    y2 = y2.reshape(TOP_K, n, 1, d)
    w_tok = w_rows[:TOP_K].T
    out0 = _final_call(x1rows, y2, w_tok, p["ln2_g"], p["ln2_b"], 0, n0)
    out1 = _final_call(x1rows, y2, w_tok, p["ln2_g"], p["ln2_b"], n0, n1)
    return out0.reshape(x0.shape), out1.reshape(x1.shape)


def kernel(x_prompt, x_sample, emb_ln_g, emb_ln_b, rel_bias, w_in, b_gate, sink_a, q_norm_g, k_norm_g,
           w_branch_a, w_branch_b, w_out, ln1_g, ln1_b, w_coarse, b_coarse, w_fine, b_fine,
           w_gate, w_up, w_down, ln2_g, ln2_b):
    l = 0
    w_in_l = w_in[l]
    qa_end = Q_COLS
    kva_end = qa_end + 2 * KV_COLS
    qb_end = kva_end + Q_COLS
    kb_end = qb_end + KV_COLS
    vb_end = kb_end + KV_COLS
    w_qkv = jnp.concatenate([
        w_in_l[:, :kva_end],
        _deinterleave_cols(w_in_l[:, kva_end:qb_end], N_HEADS),
        _deinterleave_cols(w_in_l[:, qb_end:kb_end], N_KV),
        w_in_l[:, kb_end:vb_end],
    ], axis=1).astype(BF16)
    w_router = jnp.zeros((D_MODEL, ROUTER_ROWS), F32)
    w_router = w_router.at[:, 0:N_GROUPS].set(w_coarse[l])
    w_router = w_router.at[:, FINE_ROW0:FINE_ROW0 + N_EXPERTS].set(w_fine[l])
    wr_hi = w_router.astype(BF16)
    wr_lo = (w_router - wr_hi.astype(F32)).astype(BF16)
    b_router = jnp.zeros((ROUTER_ROWS, 1), F32)
    b_router = b_router.at[0:N_GROUPS, 0].set(b_coarse[l].astype(F32))
    b_router = b_router.at[FINE_ROW0:FINE_ROW0 + N_EXPERTS, 0].set(b_fine[l].astype(F32))
    p = {
        "emb_g": emb_ln_g.reshape(1, D_MODEL), "emb_b": emb_ln_b.reshape(1, D_MODEL),
        "w_qkv": w_qkv,
        "q_gain": _deinterleave_gain(q_norm_g[l]), "k_gain": _deinterleave_gain(k_norm_g[l]),
        "sink": sink_a[l].astype(F32) * LOG2E, "bias_a": _window_bias(rel_bias),
        "w_a": w_branch_a[l].astype(BF16), "w_b": w_branch_b[l].astype(BF16),
        "w_g": w_in_l[:, vb_end:].astype(BF16), "b_g": b_gate[l].reshape(1, 2 * D_MODEL),
        "w_o": w_out[l].astype(BF16),
        "ln1_g": ln1_g[l].reshape(1, D_MODEL), "ln1_b": ln1_b[l].reshape(1, D_MODEL),
        "wr_hi": wr_hi, "wr_lo": wr_lo, "b_r": b_router,
        "w_gate": w_gate[l], "w_up": w_up[l], "w_down": w_down[l],
        "ln2_g": ln2_g[l].reshape(1, D_MODEL), "ln2_b": ln2_b[l].reshape(1, D_MODEL),
    }
    return _layer(x_prompt, x_sample, p)
```
